```python
import math
import jax
import jax.numpy as jnp
from jax import lax
import numpy as np

D_MODEL = 1024
BATCH = 8
SEQ = 2048
DEPTH = 1
DEC_BATCH = 128
DEC_SEQ = 4
PAST_LEN = 8192
PAGE_SIZE = 128

N_MEM = 256
FOX_HEADS = 8
FOX_HEAD_DIM = 64
FOX_WIDTH = FOX_HEADS * FOX_HEAD_DIM
Q_BLOCK = 128
FORGET_BIAS = 6.0
SSD_HEADS = 8
SSD_HEAD_DIM = 64
SSD_WIDTH = SSD_HEADS * SSD_HEAD_DIM
SSD_STATE = 128
SSD_GROUPS = 2
SSD_CONV = 4
SSD_CHUNK = 128
SSD_CONV_DIM = SSD_WIDTH + 2 * SSD_GROUPS * SSD_STATE
MIX_WIDTH = FOX_WIDTH + SSD_WIDTH
MEM_HEADS = 4
MEM_HEAD_DIM = 128
MEM_WIDTH = MEM_HEADS * MEM_HEAD_DIM
D_FF = 2816
FFN_CONV = 3
EPS = 1e-6
IN_SPLITS = (FOX_WIDTH, FOX_WIDTH, FOX_WIDTH, FOX_HEADS, SSD_WIDTH, SSD_CONV_DIM, SSD_HEADS)
IN_COLS = 3 * FOX_WIDTH + FOX_HEADS + SSD_WIDTH + SSD_CONV_DIM + SSD_HEADS

kernel_name = 'hybrid_fox_ssd_decoder_step'


def rms_norm(x, g):
    xf = x.astype(jnp.float32)
    y = xf * lax.rsqrt(jnp.mean(xf * xf, axis=-1, keepdims=True) + EPS)
    return (y * g.astype(jnp.float32)).astype(x.dtype)


def causal_dwconv(x, buf, w, b):
    k = w.shape[0]
    L = x.shape[1]
    xp = jnp.concatenate([buf.astype(x.dtype), x], axis=1)
    y = b.astype(x.dtype)
    for j in range(k):
        y = y + xp[:, j:j + L] * w[j]
    return y, xp[:, L:]


def mix_inputs(x, norm_g, w_in, b_forget, q_g, k_g):
    bsz, L = x.shape[:2]
    proj = rms_norm(x, norm_g) @ w_in
    offs = np.cumsum(IN_SPLITS)[:-1].tolist()
    q_raw, k_raw, v_raw, f_raw, z, xbc, dt_raw = jnp.split(proj, offs, axis=-1)
    shp = (bsz, L, FOX_HEADS, FOX_HEAD_DIM)
    q = rms_norm(q_raw.reshape(shp), q_g)
    k = rms_norm(k_raw.reshape(shp), k_g)
    v = v_raw.reshape(shp)
    logf = jax.nn.log_sigmoid(f_raw.astype(jnp.float32) + b_forget.astype(jnp.float32))
    return q, k, v, logf, z, xbc, dt_raw


def fox_prompt(q, k, v, logf):
    bsz, L = q.shape[:2]
    nb = L // Q_BLOCK
    scale = FOX_HEAD_DIM ** -0.5
    c = jnp.cumsum(logf, axis=1).transpose(0, 2, 1)
    kpos = jnp.arange(L)

    def block(i):
        start = i * Q_BLOCK
        qs = lax.dynamic_slice_in_dim(q, start, Q_BLOCK, axis=1)
        cq = lax.dynamic_slice_in_dim(c, start, Q_BLOCK, axis=2)
        s = jnp.einsum('bqhd,bkhd->bhqk', qs, k).astype(jnp.float32) * scale
        s = s + cq[:, :, :, None] - c[:, :, None, :]
        qpos = start + jnp.arange(Q_BLOCK)
        mask = kpos[None, :] <= qpos[:, None]
        s = jnp.where(mask[None, None], s, -jnp.inf)
        p = jax.nn.softmax(s, axis=-1).astype(v.dtype)
        return jnp.einsum('bhqk,bkhd->bqhd', p, v)

    out = lax.map(block, jnp.arange(nb))
    return out.transpose(1, 0, 2, 3, 4).reshape(bsz, L, FOX_WIDTH)


def fox_sample(q, k, v, logf, cache_k, cache_v, cache_logf, page_table, layer):
    bsz, T = q.shape[:2]
    scale = FOX_HEAD_DIM ** -0.5
    causal = jnp.tril(jnp.ones((T, T), dtype=bool))

    def one(args):
        qb, kb, vb, lb, pages = args
        kp = cache_k[layer, pages].reshape(-1, FOX_HEADS, FOX_HEAD_DIM)
        vp = cache_v[layer, pages].reshape(-1, FOX_HEADS, FOX_HEAD_DIM)
        lp = cache_logf[layer, pages].reshape(-1, FOX_HEADS).astype(jnp.float32)
        n_past = kp.shape[0]
        suffix = lax.cumsum(lp, axis=0, reverse=True) - lp
        cn = jnp.cumsum(lb, axis=0)
        s_past = jnp.einsum('thd,phd->htp', qb, kp.astype(qb.dtype)).astype(jnp.float32) * scale
        s_past = s_past + cn.T[:, :, None] + suffix.T[:, None, :]
        s_new = jnp.einsum('thd,shd->hts', qb, kb).astype(jnp.float32) * scale
        s_new = s_new + cn.T[:, :, None] - cn.T[:, None, :]
        s_new = jnp.where(causal[None], s_new, -jnp.inf)
        p = jax.nn.softmax(jnp.concatenate([s_past, s_new], axis=-1), axis=-1)
        o = jnp.einsum('htp,phd->thd', p[..., :n_past].astype(vp.dtype), vp)
        o = o + jnp.einsum('hts,shd->thd', p[..., n_past:].astype(vb.dtype), vb)
        return o.astype(qb.dtype)

    out = lax.map(one, (q, k, v, logf, page_table))
    return out.reshape(bsz, T, FOX_WIDTH)


def ssd_scan(x, dt, a, b_in, c_in, init_state):
    bsz, L = x.shape[:2]
    q = SSD_CHUNK if L % SSD_CHUNK == 0 else L
    nc = L // q
    rep = SSD_HEADS // SSD_GROUPS
    xc = x.reshape(bsz, nc, q, SSD_HEADS, SSD_HEAD_DIM)
    dtc = dt.reshape(bsz, nc, q, SSD_HEADS)
    bc = jnp.repeat(b_in, rep, axis=2).reshape(bsz, nc, q, SSD_HEADS, SSD_STATE)
    cc = jnp.repeat(c_in, rep, axis=2).reshape(bsz, nc, q, SSD_HEADS, SSD_STATE)
    acum = jnp.cumsum(dtc * a, axis=2)
    tri = jnp.tril(jnp.ones((q, q), dtype=bool))
    seg = acum[:, :, :, None, :] - acum[:, :, None, :, :]
    decay = jnp.exp(jnp.where(tri[None, None, :, :, None], seg, -jnp.inf))
    scores = jnp.einsum('bcthn,bcshn->bctsh', cc, bc) * decay * dtc[:, :, None, :, :]
    y_diag = jnp.einsum('bctsh,bcshp->bcthp', scores, xc)
    to_end = jnp.exp(acum[:, :, -1:, :] - acum) * dtc
    chunk_states = jnp.einsum('bcshn,bcsh,bcshp->bchpn', bc, to_end, xc)
    chunk_decay = jnp.exp(acum[:, :, -1, :])

    def step(state, inp):
        cs, cd = inp
        return state * cd[:, :, None, None] + cs, state

    final, prev = lax.scan(step, init_state,
                           (jnp.moveaxis(chunk_states, 1, 0), jnp.moveaxis(chunk_decay, 1, 0)))
    prev = jnp.moveaxis(prev, 0, 1)
    y_off = jnp.einsum('bcthn,bchpn->bcthp', cc, prev) * jnp.exp(acum)[..., None]
    y = (y_diag + y_off).reshape(bsz, L, SSD_HEADS, SSD_HEAD_DIM)
    return y, final


def ssd_mixer(z, xbc, dt_raw, conv_buf, init_state, conv_w, conv_b, dt_bias, a_log, d_skip, norm_g):
    bsz, L = xbc.shape[:2]
    xbc_c, new_buf = causal_dwconv(xbc, conv_buf, conv_w, conv_b)
    xbc_c = jax.nn.silu(xbc_c).astype(jnp.float32)
    gn = SSD_GROUPS * SSD_STATE
    xs = xbc_c[..., :SSD_WIDTH].reshape(bsz, L, SSD_HEADS, SSD_HEAD_DIM)
    b_in = xbc_c[..., SSD_WIDTH:SSD_WIDTH + gn].reshape(bsz, L, SSD_GROUPS, SSD_STATE)
    c_in = xbc_c[..., SSD_WIDTH + gn:].reshape(bsz, L, SSD_GROUPS, SSD_STATE)
    dt = jax.nn.softplus(dt_raw.astype(jnp.float32) + dt_bias.astype(jnp.float32))
    a = -jnp.exp(a_log.astype(jnp.float32))
    y, final = ssd_scan(xs, dt, a, b_in, c_in, init_state.astype(jnp.float32))
    y = (y + d_skip.astype(jnp.float32)[:, None] * xs).reshape(bsz, L, SSD_WIDTH)
    y = rms_norm(y * jax.nn.silu(z.astype(jnp.float32)), norm_g).astype(z.dtype)
    return y, final.astype(z.dtype), new_buf


def mem_kv(mem, in_norm_g, w_kv, k_g):
    bsz, m = mem.shape[:2]
    kv = rms_norm(mem, in_norm_g) @ w_kv
    k = rms_norm(kv[..., :MEM_WIDTH].reshape(bsz, m, MEM_HEADS, MEM_HEAD_DIM), k_g)
    v = kv[..., MEM_WIDTH:].reshape(bsz, m, MEM_HEADS, MEM_HEAD_DIM)
    return k, v


def mem_attn(xn, mk, mv, w_q, q_g, w_o):
    bsz, L = xn.shape[:2]
    q = rms_norm((xn @ w_q).reshape(bsz, L, MEM_HEADS, MEM_HEAD_DIM), q_g)
    s = jnp.einsum('blhd,bmhd->bhlm', q, mk.astype(q.dtype)).astype(jnp.float32) * MEM_HEAD_DIM ** -0.5
    p = jax.nn.softmax(s, axis=-1).astype(mv.dtype)
    o = jnp.einsum('bhlm,bmhd->blhd', p, mv).reshape(bsz, L, MEM_WIDTH).astype(xn.dtype)
    return o @ w_o


def conv_ffn(xn, buf, w_gate, w_up, conv_w, conv_b, w_down):
    g, new_buf = causal_dwconv(xn @ w_gate, buf, conv_w, conv_b)
    h = jax.nn.silu(g) * (xn @ w_up)
    return h @ w_down, new_buf


def setup_inputs(seed: int = 0) -> dict:
    key = jax.random.key(seed)
    ks = jax.random.split(key, 64)
    counter = [0]

    def nk():
        k = ks[counter[0]]
        counter[0] += 1
        return k

    def nrm(shape, scale=1.0):
        return scale * jax.random.normal(nk(), shape, jnp.float32)

    def gain(shape):
        return 1.0 + nrm(shape, 0.05)

    n_pages = PAST_LEN // PAGE_SIZE
    n_used = DEC_BATCH * n_pages
    n_phys = n_used + (n_used + 3) // 4
    page_table = jax.random.permutation(nk(), n_phys)[:n_used].reshape(DEC_BATCH, n_pages).astype(jnp.int32)
    dt0 = jnp.exp(jax.random.uniform(nk(), (DEPTH, SSD_HEADS), jnp.float32, math.log(1e-3), math.log(1e-1)))
    dt_bias = dt0 + jnp.log(-jnp.expm1(-dt0))
    a_log = jnp.log(jax.random.uniform(nk(), (DEPTH, SSD_HEADS), jnp.float32, 1.0, 16.0))
    return {
        'x_prompt': nrm((BATCH, SEQ, D_MODEL)),
        'mem_prompt': nrm((BATCH, N_MEM, D_MODEL)),
        'x_sample': nrm((DEC_BATCH, DEC_SEQ, D_MODEL)),
        'cache_fox_k': nrm((DEPTH, n_phys, PAGE_SIZE, FOX_HEADS, FOX_HEAD_DIM)),
        'cache_fox_v': nrm((DEPTH, n_phys, PAGE_SIZE, FOX_HEADS, FOX_HEAD_DIM)),
        'cache_fox_logf': jax.nn.log_sigmoid(FORGET_BIAS + nrm((DEPTH, n_phys, PAGE_SIZE, FOX_HEADS), 0.5)),
        'state_ssm': nrm((DEPTH, DEC_BATCH, SSD_HEADS, SSD_HEAD_DIM, SSD_STATE), 0.5),
        'state_ssm_conv': nrm((DEPTH, DEC_BATCH, SSD_CONV - 1, SSD_CONV_DIM)),
        'state_ffn_conv': nrm((DEPTH, DEC_BATCH, FFN_CONV - 1, D_FF)),
        'cache_mem_k': nrm((DEPTH, DEC_BATCH, N_MEM, MEM_HEADS, MEM_HEAD_DIM)),
        'cache_mem_v': nrm((DEPTH, DEC_BATCH, N_MEM, MEM_HEADS, MEM_HEAD_DIM)),
        'page_table': page_table,
        'norm_mix_g': gain((DEPTH, D_MODEL)),
        'w_in': nrm((DEPTH, D_MODEL, IN_COLS), D_MODEL ** -0.5),
        'b_forget': FORGET_BIAS + nrm((DEPTH, FOX_HEADS), 0.5),
        'fox_q_norm_g': gain((DEPTH, FOX_HEAD_DIM)),
        'fox_k_norm_g': gain((DEPTH, FOX_HEAD_DIM)),
        'ssd_conv_w': nrm((DEPTH, SSD_CONV, SSD_CONV_DIM), SSD_CONV ** -0.5),
        'ssd_conv_b': nrm((DEPTH, SSD_CONV_DIM), 0.02),
        'ssd_dt_bias': dt_bias,
        'ssd_a_log': a_log,
        'ssd_d': 1.0 + nrm((DEPTH, SSD_HEADS), 0.1),
        'ssd_norm_g': gain((DEPTH, SSD_WIDTH)),
        'w_out': nrm((DEPTH, MIX_WIDTH, D_MODEL), MIX_WIDTH ** -0.5),
        'norm_mem_g': gain((DEPTH, D_MODEL)),
        'mem_in_norm_g': gain((DEPTH, D_MODEL)),
        'w_mem_q': nrm((DEPTH, D_MODEL, MEM_WIDTH), D_MODEL ** -0.5),
        'w_mem_kv': nrm((DEPTH, D_MODEL, 2 * MEM_WIDTH), D_MODEL ** -0.5),
        'mem_q_norm_g': gain((DEPTH, MEM_HEAD_DIM)),
        'mem_k_norm_g': gain((DEPTH, MEM_HEAD_DIM)),
        'w_mem_o': nrm((DEPTH, MEM_WIDTH, D_MODEL), MEM_WIDTH ** -0.5),
        'norm_ffn_g': gain((DEPTH, D_MODEL)),
        'w_ffn_gate': nrm((DEPTH, D_MODEL, D_FF), D_MODEL ** -0.5),
        'w_ffn_up': nrm((DEPTH, D_MODEL, D_FF), D_MODEL ** -0.5),
        'ffn_conv_w': nrm((DEPTH, FFN_CONV, D_FF), FFN_CONV ** -0.5),
        'ffn_conv_b': nrm((DEPTH, D_FF), 0.02),
        'w_ffn_down': nrm((DEPTH, D_FF, D_MODEL), D_FF ** -0.5),
    }


def reference(x_prompt, mem_prompt, x_sample, cache_fox_k, cache_fox_v, cache_fox_logf, state_ssm,
              state_ssm_conv, state_ffn_conv, cache_mem_k, cache_mem_v, page_table,
              norm_mix_g, w_in, b_forget, fox_q_norm_g, fox_k_norm_g, ssd_conv_w, ssd_conv_b,
              ssd_dt_bias, ssd_a_log, ssd_d, ssd_norm_g, w_out, norm_mem_g, mem_in_norm_g, w_mem_q,
              w_mem_kv, mem_q_norm_g, mem_k_norm_g, w_mem_o, norm_ffn_g, w_ffn_gate, w_ffn_up,
              ffn_conv_w, ffn_conv_b, w_ffn_down):
    yp = x_prompt
    ys = x_sample
    bp = x_prompt.shape[0]
    p_k, p_v, p_lf, p_ssm, p_sconv, p_fconv, p_mk, p_mv = [], [], [], [], [], [], [], []
    s_k, s_v, s_lf, s_ssm, s_sconv, s_fconv = [], [], [], [], [], []
    for l in range(DEPTH):
        q, k, v, lf, z, xbc, dtr = mix_inputs(yp, norm_mix_g[l], w_in[l], b_forget[l],
                                              fox_q_norm_g[l], fox_k_norm_g[l])
        fox_o = fox_prompt(q, k, v, lf)
        ssd_o, ssm_fin, ssm_buf = ssd_mixer(
            z, xbc, dtr, jnp.zeros((bp, SSD_CONV - 1, SSD_CONV_DIM), yp.dtype),
            jnp.zeros((bp, SSD_HEADS, SSD_HEAD_DIM, SSD_STATE), jnp.float32),
            ssd_conv_w[l], ssd_conv_b[l], ssd_dt_bias[l], ssd_a_log[l], ssd_d[l], ssd_norm_g[l])
        yp = yp + jnp.concatenate([fox_o, ssd_o], axis=-1) @ w_out[l]
        mk, mv = mem_kv(mem_prompt, mem_in_norm_g[l], w_mem_kv[l], mem_k_norm_g[l])
        yp = yp + mem_attn(rms_norm(yp, norm_mem_g[l]), mk, mv, w_mem_q[l], mem_q_norm_g[l], w_mem_o[l])
        f_o, ffn_buf = conv_ffn(rms_norm(yp, norm_ffn_g[l]), jnp.zeros((bp, FFN_CONV - 1, D_FF), yp.dtype),
                                w_ffn_gate[l], w_ffn_up[l], ffn_conv_w[l], ffn_conv_b[l], w_ffn_down[l])
        yp = yp + f_o
        p_k.append(k)
        p_v.append(v)
        p_lf.append(lf)
        p_ssm.append(ssm_fin)
        p_sconv.append(ssm_buf)
        p_fconv.append(ffn_buf)
        p_mk.append(mk)
        p_mv.append(mv)
        q, k, v, lf, z, xbc, dtr = mix_inputs(ys, norm_mix_g[l], w_in[l], b_forget[l],
                                              fox_q_norm_g[l], fox_k_norm_g[l])
        fox_o = fox_sample(q, k, v, lf, cache_fox_k, cache_fox_v, cache_fox_logf, page_table, l)
        ssd_o, ssm_new, ssm_buf = ssd_mixer(
            z, xbc, dtr, state_ssm_conv[l], state_ssm[l],
            ssd_conv_w[l], ssd_conv_b[l], ssd_dt_bias[l], ssd_a_log[l], ssd_d[l], ssd_norm_g[l])
        ys = ys + jnp.concatenate([fox_o, ssd_o], axis=-1) @ w_out[l]
        ys = ys + mem_attn(rms_norm(ys, norm_mem_g[l]), cache_mem_k[l], cache_mem_v[l],
                           w_mem_q[l], mem_q_norm_g[l], w_mem_o[l])
        f_o, ffn_buf = conv_ffn(rms_norm(ys, norm_ffn_g[l]), state_ffn_conv[l],
                                w_ffn_gate[l], w_ffn_up[l], ffn_conv_w[l], ffn_conv_b[l], w_ffn_down[l])
        ys = ys + f_o
        s_k.append(k)
        s_v.append(v)
        s_lf.append(lf)
        s_ssm.append(ssm_new)
        s_sconv.append(ssm_buf)
        s_fconv.append(ffn_buf)
    return (yp, ys,
            jnp.stack(p_k), jnp.stack(p_v), jnp.stack(p_lf), jnp.stack(p_ssm), jnp.stack(p_sconv),
            jnp.stack(p_fconv), jnp.stack(p_mk), jnp.stack(p_mv),
            jnp.stack(s_k), jnp.stack(s_v), jnp.stack(s_lf), jnp.stack(s_ssm), jnp.stack(s_sconv),
            jnp.stack(s_fconv))
```

```python
import functools
import math

import jax
import jax.numpy as jnp
from jax import lax
from jax.experimental import pallas as pl
from jax.experimental.pallas import tpu as pltpu

F32 = jnp.float32
BF16 = jnp.bfloat16
EPS = 1e-6
FORGET_HEADS = 8
FOX_DIM = 64
SSD_HEADS = 8
SSD_DIM = 64
SSD_STATE = 128
SSD_CONV = 4
CHUNK = 128
MEM_HEADS = 4
MEM_DIM = 128
FFN_CONV = 3
LANES = 128
SUBLANES = 8
VMEM_LIMIT = 56 * 1024 * 1024
NT = (((1,), (1,)), ((), ()))
TN = (((0,), (0,)), ((), ()))


def _dot(a, b):
    return jnp.dot(a, b, preferred_element_type=F32)


def _dot_nt(a, b):
    return lax.dot_general(a, b, NT, preferred_element_type=F32)


def _dot_tn(a, b):
    return lax.dot_general(a, b, TN, preferred_element_type=F32)


def _split3(x):
    hi = x.astype(BF16)
    r1 = x - hi.astype(F32)
    mid = r1.astype(BF16)
    lo = (r1 - mid.astype(F32)).astype(BF16)
    return hi, mid, lo


def _dot3_left(x, m):
    hi, mid, lo = _split3(x)
    return _dot(hi, m) + _dot(mid, m) + _dot(lo, m)


def _dot3_right(m, x):
    hi, mid, lo = _split3(x)
    return _dot(m, hi) + _dot(m, mid) + _dot(m, lo)


def _rms(x, g):
    return x * lax.rsqrt(jnp.mean(x * x, axis=-1, keepdims=True) + EPS) * g


def _softplus(x):
    return jnp.maximum(x, 0.0) + jnp.log1p(jnp.exp(-jnp.abs(x)))


def _log_sigmoid(x):
    return -_softplus(-x)


def _silu(x):
    return x * (1.0 / (1.0 + jnp.exp(-x)))


def _iota(shape, dim):
    return lax.broadcasted_iota(jnp.int32, shape, dim)


def _div_pow2(x, d):
    assert d & (d - 1) == 0
    return lax.shift_right_logical(x, jnp.int32(d.bit_length() - 1))


def _mod_pow2(x, d):
    assert d & (d - 1) == 0
    return lax.bitwise_and(x, jnp.int32(d - 1))


def _cparams(sem):
    return pltpu.CompilerParams(dimension_semantics=sem, vmem_limit_bytes=VMEM_LIMIT)


def _const_spec(shape):
    n = len(shape)
    return pl.BlockSpec(shape, lambda *_: (0,) * n)


def _inproj_kernel(x_ref, g_ref, w_ref, wsm_ref, wsmt_ref, gq_ref, gk_ref, grp_ref, bsm_ref, bsmt_ref,
                   q_ref, k_ref, v_ref, z_ref, xbc_ref, sm_ref, smt_ref):
    fw = FORGET_HEADS * FOX_DIM
    xn = _rms(x_ref[...], g_ref[...]).astype(BF16)

    def head_norm(y, gain):
        ms = _dot3_left(y * y, grp_ref[...])
        return y * lax.rsqrt(ms + EPS) * gain

    q = head_norm(_dot(xn, w_ref[:, 0:fw]), gq_ref[...])
    q_ref[...] = (q * FOX_DIM ** -0.5).astype(BF16)
    k_ref[...] = head_norm(_dot(xn, w_ref[:, fw:2 * fw]), gk_ref[...])
    v_ref[...] = _dot(xn, w_ref[:, 2 * fw:3 * fw])
    z_ref[...] = _dot(xn, w_ref[:, 3 * fw:4 * fw])
    xbc_ref[...] = _dot(xn, w_ref[:, 4 * fw:])
    sm = _dot(xn, wsm_ref[...]) + bsm_ref[...]
    sm_ref[...] = jnp.where(_iota(sm.shape, 1) < FORGET_HEADS, _log_sigmoid(sm), _softplus(sm))
    smt = _dot_nt(wsmt_ref[...], xn) + bsmt_ref[...]
    smt_ref[...] = jnp.where(_iota(smt.shape, 0) < FORGET_HEADS, _log_sigmoid(smt), _softplus(smt))


def _inproj(x, p, tm):
    n, d = x.shape
    fw = FORGET_HEADS * FOX_DIM
    nbig = p['w_big'].shape[1]
    row = lambda i: (i, 0)
    return pl.pallas_call(
        _inproj_kernel,
        grid=(n // tm,),
        in_specs=[
            pl.BlockSpec((tm, d), row),
            _const_spec((1, d)),
            _const_spec((d, nbig)),
            _const_spec((d, LANES)),
            _const_spec((2 * SUBLANES, d)),
            _const_spec((1, fw)),
            _const_spec((1, fw)),
            _const_spec((fw, fw)),
            _const_spec((1, LANES)),
            _const_spec((2 * SUBLANES, 1)),
        ],
        out_specs=[
            pl.BlockSpec((tm, fw), row),
            pl.BlockSpec((tm, fw), row),
            pl.BlockSpec((tm, fw), row),
            pl.BlockSpec((tm, fw), row),
            pl.BlockSpec((tm, nbig - 4 * fw), row),
            pl.BlockSpec((tm, LANES), row),
            pl.BlockSpec((2 * SUBLANES, tm), lambda i: (0, i)),
        ],
        out_shape=[
            jax.ShapeDtypeStruct((n, fw), BF16),
            jax.ShapeDtypeStruct((n, fw), F32),
            jax.ShapeDtypeStruct((n, fw), F32),
            jax.ShapeDtypeStruct((n, fw), F32),
            jax.ShapeDtypeStruct((n, nbig - 4 * fw), F32),
            jax.ShapeDtypeStruct((n, LANES), F32),
            jax.ShapeDtypeStruct((2 * SUBLANES, n), F32),
        ],
        compiler_params=_cparams(("parallel",)),
        name="inproj",
    )(x, p['norm_mix_g'], p['w_big'], p['w_sm'], p['w_smt'], p['gq'], p['gk'], p['grp'], p['b_sm'], p['b_smt'])


def _cumsum_kernel(sm_ref, smt_ref, col_ref, row_ref):
    length = sm_ref.shape[0]
    r = _iota((LANES, LANES), 0)
    c = _iota((LANES, LANES), 1)
    lower = (c <= r).astype(BF16)
    upper = (r <= c).astype(BF16)
    carry_c = jnp.zeros((1, LANES), F32)
    carry_r = jnp.zeros((smt_ref.shape[0], 1), F32)
    for b in range(length // LANES):
        sl = slice(b * LANES, (b + 1) * LANES)
        cs = _dot3_right(lower, sm_ref[sl, :]) + carry_c
        col_ref[sl, :] = cs
        carry_c = cs[LANES - 1:LANES, :]
        rs = _dot3_left(smt_ref[:, sl], upper) + carry_r
        row_ref[:, sl] = rs
        carry_r = rs[:, LANES - 1:LANES]


def _cumsum(sm, smt, batch, length):
    n = sm.shape[0]
    return pl.pallas_call(
        _cumsum_kernel,
        grid=(batch,),
        in_specs=[pl.BlockSpec((length, LANES), lambda b: (b, 0)),
                  pl.BlockSpec((2 * SUBLANES, length), lambda b: (0, b))],
        out_specs=[pl.BlockSpec((length, LANES), lambda b: (b, 0)),
                   pl.BlockSpec((2 * SUBLANES, length), lambda b: (0, b))],
        out_shape=[jax.ShapeDtypeStruct((n, LANES), F32),
                   jax.ShapeDtypeStruct((2 * SUBLANES, n), F32)],
        compiler_params=_cparams(("parallel",)),
        name="logf_cumsum",
    )(sm, smt)


def _fox_prompt_kernel(q_ref, k_ref, v_ref, cc_ref, cr_ref, o_ref, *, tq):
    hp = pl.program_id(1)
    i = pl.program_id(2)
    q = q_ref[...]
    lane = _iota((1, LANES), 1)
    first = lane < FOX_DIM
    zero = jnp.zeros_like(q)
    qm = (jnp.where(first, q, zero), jnp.where(first, zero, q))
    cc = cc_ref[...]
    lane_c = _iota(cc.shape, 1)
    cq = tuple(jnp.sum(jnp.where(lane_c == 2 * hp + hh, cc, 0.0), axis=1, keepdims=True) for hh in range(2))
    sub = _iota((cr_ref.shape[1], tq), 0)

    def step(j, carry, masked):
        ks = k_ref[pl.ds(j * tq, tq), :].astype(BF16)
        vs = v_ref[pl.ds(j * tq, tq), :].astype(BF16)
        cr = cr_ref[j]
        new = []
        for hh in range(2):
            m, l, acc = carry[hh]
            ck = jnp.sum(jnp.where(sub == 2 * hp + hh, cr, 0.0), axis=0, keepdims=True)
            s = _dot_nt(qm[hh], ks) + (cq[hh] - ck)
            if masked:
                s = jnp.where(_iota(s.shape, 1) <= _iota(s.shape, 0), s, -jnp.inf)
            m_new = jnp.maximum(m, jnp.max(s, axis=1, keepdims=True))
            alpha = jnp.exp(m - m_new)
            p = jnp.exp(s - m_new)
            l = alpha * l + jnp.sum(p, axis=1, keepdims=True)
            acc = alpha * acc + _dot(p.astype(BF16), vs)
            new.append((m_new, l, acc))
        return tuple(new)

    init = tuple((jnp.full((tq, 1), -jnp.inf, F32), jnp.zeros((tq, 1), F32), jnp.zeros((tq, LANES), F32))
                 for _ in range(2))
    carry = lax.fori_loop(0, i, lambda j, c: step(j, c, False), init)
    carry = step(i, carry, True)
    o_ref[...] = jnp.where(first, carry[0][2] / carry[0][1], carry[1][2] / carry[1][1])


def _fox_prompt(q, k, v, ccol, crow4, batch, length, tq):
    n, fw = q.shape
    nq = length // tq
    return pl.pallas_call(
        functools.partial(_fox_prompt_kernel, tq=tq),
        grid=(batch, fw // LANES, nq),
        in_specs=[
            pl.BlockSpec((tq, LANES), lambda b, h, i: (b * nq + i, h)),
            pl.BlockSpec((length, LANES), lambda b, h, i: (b, h)),
            pl.BlockSpec((length, LANES), lambda b, h, i: (b, h)),
            pl.BlockSpec((tq, LANES), lambda b, h, i: (b * nq + i, 0)),
            pl.BlockSpec((None, nq, 2 * SUBLANES, tq), lambda b, h, i: (b, 0, 0, 0)),
        ],
        out_specs=pl.BlockSpec((tq, LANES), lambda b, h, i: (b * nq + i, h)),
        out_shape=jax.ShapeDtypeStruct((n, fw), F32),
        compiler_params=_cparams(("parallel", "parallel", "arbitrary")),
        name="fox_prompt",
    )(q, k, v, ccol, crow4)


def _ssd_kernel(xbc_ref, z_ref, sm_ref, smt_ref, st0_ref, tail0_ref, cw_ref, cb_ref, arow_ref, acol_ref,
                dsk_ref, ng_ref, y_ref, fin_ref, state, tail, *pads, t_real):
    q = CHUNK
    c = pl.program_id(1)
    hw = SSD_HEADS * SSD_DIM
    gs = SSD_STATE

    @pl.when(c == 0)
    def _():
        state[...] = st0_ref[...]
        tail[...] = tail0_ref[...]

    if t_real == q:
        x = xbc_ref[...]
        z = z_ref[...]
        sm = sm_ref[...]
        smt = smt_ref[...]
    else:
        xp, zp, smp, smtp = pads
        xp[...] = jnp.zeros_like(xp)
        zp[...] = jnp.zeros_like(zp)
        smp[...] = jnp.zeros_like(smp)
        smtp[...] = jnp.zeros_like(smtp)
        xp[0:t_real, :] = xbc_ref[...]
        zp[0:t_real, :] = z_ref[...]
        smp[0:t_real, :] = sm_ref[...]
        smtp[:, 0:t_real] = smt_ref[...]
        x, z, sm, smt = xp[...], zp[...], smp[...], smtp[...]

    prev = tail[...]
    row8 = _iota(prev.shape, 0)
    conv = cb_ref[...] + cw_ref[SSD_CONV - 1:SSD_CONV, :] * x
    for sh in range(1, SSD_CONV):
        rx = pltpu.roll(x, sh, 0)
        fix = jnp.where(row8 < sh, pltpu.roll(prev, sh, 0), rx[0:SUBLANES, :])
        xs_sh = jnp.concatenate([fix, rx[SUBLANES:, :]], axis=0)
        conv = conv + cw_ref[SSD_CONV - 1 - sh:SSD_CONV - sh, :] * xs_sh
    tail[...] = x[q - SUBLANES:q, :]
    xc = _silu(conv)
    xs = xc[:, 0:hw]

    r = _iota((q, q), 0)
    cidx = _iota((q, q), 1)
    tril = cidx <= r
    lower = tril.astype(BF16)
    upper = (r <= cidx).astype(BF16)
    acum_c = _dot3_right(lower, sm * arow_ref[...])
    acum_r = _dot3_left(smt * acol_ref[...], upper)
    lane = _iota((1, LANES), 1)
    first = lane < SSD_DIM
    rowp = _iota((LANES, 1), 0) < SSD_DIM

    ys = []
    for pr in range(SSD_HEADS // 2):
        g = pr // (SSD_HEADS // 4)
        bg = xc[:, hw + g * gs: hw + (g + 1) * gs].astype(BF16)
        cg = xc[:, hw + 2 * gs + g * gs: hw + 2 * gs + (g + 1) * gs].astype(BF16)
        gmat = _dot_nt(cg, bg)
        xpair = xs[:, pr * LANES:(pr + 1) * LANES]
        xpb = xpair.astype(BF16)
        yd, te, ea, cd = [], [], [], []
        for hh in range(2):
            h = FORGET_HEADS + 2 * pr + hh
            ac = acum_c[:, h:h + 1]
            ar = acum_r[h:h + 1, :]
            dtr = smt[h:h + 1, :]
            dtc = sm[:, h:h + 1]
            last = acum_c[q - 1:q, h:h + 1]
            decay = jnp.exp(jnp.where(tril, ac - ar, -jnp.inf))
            sc = gmat * decay * dtr
            yd.append(_dot(sc.astype(BF16), xpb))
            te.append(jnp.exp(last - ac) * dtc)
            ea.append(jnp.exp(ac))
            cd.append(jnp.exp(last))
        ydiag = jnp.where(first, yd[0], yd[1])
        xsc = (xpair * jnp.where(first, te[0], te[1])).astype(BF16)
        cstate = _dot_tn(xsc, bg)
        prev_st = state[pr * LANES:(pr + 1) * LANES, :]
        yoff = _dot_nt(cg, prev_st.astype(BF16)) * jnp.where(first, ea[0], ea[1])
        state[pr * LANES:(pr + 1) * LANES, :] = prev_st * jnp.where(rowp, cd[0], cd[1]) + cstate
        ys.append(ydiag + yoff + dsk_ref[:, pr * LANES:(pr + 1) * LANES] * xpair)
    y = jnp.concatenate(ys, axis=1) * _silu(z)
    y = _rms(y, ng_ref[...])
    y_ref[...] = y[0:t_real, :]

    @pl.when(c == pl.num_programs(1) - 1)
    def _():
        fin_ref[...] = state[...]


def _ssd(xbc, z, sm, smt4, st0, tail0, p):
    batch, length, cdim = xbc.shape
    hw = z.shape[2]
    t_real = min(length, CHUNK)
    nc = length // t_real
    blk = lambda b, c: (b, c, 0)
    per_b = lambda b, c: (b, 0, 0)
    pads = []
    if t_real != CHUNK:
        pads = [pltpu.VMEM((CHUNK, cdim), F32), pltpu.VMEM((CHUNK, hw), F32),
                pltpu.VMEM((CHUNK, LANES), F32), pltpu.VMEM((2 * SUBLANES, CHUNK), F32)]
    return pl.pallas_call(
        functools.partial(_ssd_kernel, t_real=t_real),
        grid=(batch, nc),
        in_specs=[
            pl.BlockSpec((None, t_real, cdim), blk),
            pl.BlockSpec((None, t_real, hw), blk),
            pl.BlockSpec((None, t_real, LANES), blk),
            pl.BlockSpec((None, None, 2 * SUBLANES, t_real), lambda b, c: (b, c, 0, 0)),
            pl.BlockSpec((None, hw, SSD_STATE), per_b),
            pl.BlockSpec((None, SUBLANES, cdim), per_b),
            _const_spec((SSD_CONV, cdim)),
            _const_spec((1, cdim)),
            _const_spec((1, LANES)),
            _const_spec((2 * SUBLANES, 1)),
            _const_spec((1, hw)),
            _const_spec((1, hw)),
        ],
        out_specs=[pl.BlockSpec((None, t_real, hw), blk),
                   pl.BlockSpec((None, hw, SSD_STATE), per_b)],
        out_shape=[jax.ShapeDtypeStruct((batch, length, hw), F32),
                   jax.ShapeDtypeStruct((batch, hw, SSD_STATE), F32)],
        scratch_shapes=[pltpu.VMEM((hw, SSD_STATE), F32), pltpu.VMEM((SUBLANES, cdim), F32)] + pads,
        compiler_params=_cparams(("parallel", "arbitrary")),
        name="ssd",
    )(xbc, z, sm, smt4, st0, tail0, p['ssd_conv_w'], p['ssd_conv_b'], p['a_row'], p['a_col'], p['d_row'],
      p['ssd_norm_g'])


PAGES_PER_CHUNK = 16


def _fox_sample_kernel(pt_ref, qb_ref, kn_ref, vn_ref, lft_ref, kc_ref, vc_ref, lc_ref, o_ref,
                       kbuf, vbuf, lbuf, suf, m_s, l_s, acc_s, cn_s, ksem, vsem, lsem, *, n_chunks, t_new):
    b = pl.program_id(0)
    c = pl.program_id(1)
    nb = pl.num_programs(0)
    step = b * n_chunks + c
    slot = step % 2
    ppc = PAGES_PER_CHUNK
    n_pages = n_chunks * ppc
    page = kbuf.shape[-1] // ppc
    heads = FORGET_HEADS
    rows = t_new * heads

    def kv_copies(bb, cc, sl):
        cps = []
        for j in range(ppc):
            pg = pt_ref[bb, cc * ppc + j]
            cps.append(pltpu.make_async_copy(kc_ref.at[pg], kbuf.at[sl, :, :, pl.ds(j * page, page)], ksem.at[sl]))
            cps.append(pltpu.make_async_copy(vc_ref.at[pg], vbuf.at[sl, :, :, pl.ds(j * page, page)], vsem.at[sl]))
        return cps

    def lf_copies(bb, sl):
        return [pltpu.make_async_copy(lc_ref.at[pt_ref[bb, j]], lbuf.at[sl, pl.ds(j * heads, heads), :], lsem.at[sl])
                for j in range(n_pages)]

    @pl.when(step == 0)
    def _():
        for cp in lf_copies(0, 0):
            cp.start()
        for cp in kv_copies(0, 0, 0):
            cp.start()

    last_c = c == n_chunks - 1
    nb_b = jnp.where(last_c, b + 1, b)
    nb_c = jnp.where(last_c, 0, c + 1)

    @pl.when(step + 1 < nb * n_chunks)
    def _():
        for cp in kv_copies(nb_b, nb_c, 1 - slot):
            cp.start()

    bslot = b % 2

    @pl.when(jnp.logical_and(c == 0, b + 1 < nb))
    def _():
        for cp in lf_copies(b + 1, 1 - bslot):
            cp.start()

    @pl.when(c == 0)
    def _():
        for cp in lf_copies(b, bslot):
            cp.wait()
        lf = lbuf[bslot]
        r = _iota((page, page), 0)
        cidx = _iota((page, page), 1)
        later = (r > cidx).astype(BF16)
        within = _dot3_left(lf, later)
        tot = within[:, 0:1] + lf[:, 0:1]
        nr = n_pages * heads
        rr = _iota((nr, nr), 0)
        cc2 = _iota((nr, nr), 1)
        sel = jnp.logical_and(_mod_pow2(rr, heads) == _mod_pow2(cc2, heads),
                              _div_pow2(cc2, heads) > _div_pow2(rr, heads)).astype(BF16)
        suf[...] = within + _dot3_right(sel, jnp.broadcast_to(tot, lf.shape))
        rl = _iota((LANES, LANES), 0)
        cl = _iota((LANES, LANES), 1)
        cn_s[...] = _dot3_left(lft_ref[...], (rl <= cl).astype(BF16))
        m_s[...] = jnp.full(m_s.shape, -jnp.inf, F32)
        l_s[...] = jnp.zeros(l_s.shape, F32)
        acc_s[...] = jnp.zeros(acc_s.shape, F32)

    cn = cn_s[...]
    cn_col = jnp.concatenate([cn[:, t:t + 1] for t in range(t_new)], axis=0)
    qb = qb_ref[...]

    for cp in kv_copies(b, c, slot):
        cp.wait()

    def online(s, pv):
        m = m_s[...]
        m_new = jnp.maximum(m, jnp.max(s, axis=1, keepdims=True))
        alpha = jnp.exp(m - m_new)
        p = jnp.exp(s - m_new)
        l_s[...] = alpha * l_s[...] + jnp.sum(p, axis=1, keepdims=True)
        acc_s[...] = alpha * acc_s[...] + pv(p.astype(BF16))
        m_s[...] = m_new

    fdim = heads * FOX_DIM
    kt = kbuf[slot].reshape(fdim, ppc * page).astype(BF16)
    vt = vbuf[slot].reshape(fdim, ppc * page).astype(BF16)
    sfx = jnp.concatenate([suf[pl.ds((c * ppc + j) * heads, heads), :] for j in range(ppc)], axis=1)
    s = _dot(qb, kt) + jnp.concatenate([sfx] * t_new, axis=0) + cn_col
    online(s, lambda p: _dot_nt(p, vt))

    @pl.when(last_c)
    def _():
        kn = kn_ref[...].astype(BF16)
        vn = vn_ref[...].astype(BF16)
        s2 = _dot_nt(qb, kn) + cn_col - jnp.concatenate([cn] * t_new, axis=0)
        tq = _div_pow2(_iota(s2.shape, 0), heads)
        s2 = jnp.where(_iota(s2.shape, 1) <= tq, s2, -jnp.inf)
        online(s2, lambda p: _dot(p, vn))
        o = acc_s[...] / l_s[...]
        keep = _div_pow2(_iota((heads, fdim), 1), FOX_DIM) == _iota((heads, fdim), 0)
        o_ref[...] = jnp.concatenate(
            [jnp.sum(jnp.where(keep, o[t * heads:(t + 1) * heads, :], 0.0), axis=0, keepdims=True)
             for t in range(t_new)], axis=0)


def _fox_sample(page_table, qblk, kn_pad, vn_pad, lft, kc, vc, lc):
    batch, rows, fdim = qblk.shape
    t_new = rows // FORGET_HEADS
    n_pages = page_table.shape[1]
    page = kc.shape[-1]
    n_chunks = n_pages // PAGES_PER_CHUNK
    pc = PAGES_PER_CHUNK * page
    per_b = lambda b, c, pt: (b, 0, 0)
    grid_spec = pltpu.PrefetchScalarGridSpec(
        num_scalar_prefetch=1,
        grid=(batch, n_chunks),
        in_specs=[
            pl.BlockSpec((None, rows, fdim), per_b),
            pl.BlockSpec((None, LANES, fdim), per_b),
            pl.BlockSpec((None, LANES, fdim), per_b),
            pl.BlockSpec((None, FORGET_HEADS, LANES), per_b),
            pl.BlockSpec(memory_space=pl.ANY),
            pl.BlockSpec(memory_space=pl.ANY),
            pl.BlockSpec(memory_space=pl.ANY),
        ],
        out_specs=pl.BlockSpec((None, t_new, fdim), per_b),
        scratch_shapes=[
            pltpu.VMEM((2, FORGET_HEADS, FOX_DIM, pc), F32),
            pltpu.VMEM((2, FORGET_HEADS, FOX_DIM, pc), F32),
            pltpu.VMEM((2, n_pages * FORGET_HEADS, page), F32),
            pltpu.VMEM((n_pages * FORGET_HEADS, page), F32),
            pltpu.VMEM((rows, 1), F32),
            pltpu.VMEM((rows, 1), F32),
            pltpu.VMEM((rows, fdim), F32),
            pltpu.VMEM((FORGET_HEADS, LANES), F32),
            pltpu.SemaphoreType.DMA((2,)),
            pltpu.SemaphoreType.DMA((2,)),
            pltpu.SemaphoreType.DMA((2,)),
        ],
    )
    return pl.pallas_call(
        functools.partial(_fox_sample_kernel, n_chunks=n_chunks, t_new=t_new),
        grid_spec=grid_spec,
        out_shape=jax.ShapeDtypeStruct((batch, t_new, fdim), F32),
        compiler_params=_cparams(("arbitrary", "arbitrary")),
        name="fox_sample",
    )(page_table, qblk, kn_pad, vn_pad, lft, kc, vc, lc)


def _mem_kv_kernel(m_ref, g_ref, w_ref, gk_ref, k_ref, v_ref):
    mw = MEM_HEADS * MEM_DIM
    xn = _rms(m_ref[...], g_ref[...]).astype(BF16)
    kv = _dot(xn, w_ref[...])
    for h in range(MEM_HEADS):
        sl = slice(h * MEM_DIM, (h + 1) * MEM_DIM)
        k_ref[:, sl] = _rms(kv[:, sl], gk_ref[...])
    v_ref[...] = kv[:, mw:]


def _mem_kv(mem, p, tm):
    n, d = mem.shape
    mw = MEM_HEADS * MEM_DIM
    row = lambda i: (i, 0)
    return pl.pallas_call(
        _mem_kv_kernel,
        grid=(n // tm,),
        in_specs=[pl.BlockSpec((tm, d), row), _const_spec((1, d)), _const_spec((d, 2 * mw)),
                  _const_spec((1, MEM_DIM))],
        out_specs=[pl.BlockSpec((tm, mw), row), pl.BlockSpec((tm, mw), row)],
        out_shape=[jax.ShapeDtypeStruct((n, mw), F32), jax.ShapeDtypeStruct((n, mw), F32)],
        compiler_params=_cparams(("parallel",)),
        name="mem_kv",
    )(mem, p['mem_in_norm_g'], p['w_mem_kv'], p['mem_k_norm_g'])


def _mix_out(x_ref, fo_ref, so_ref, wout_ref, g_ref, wq_ref, gq_ref):
    fw = fo_ref.shape[-1]
    y1 = x_ref[...] + _dot(fo_ref[...].astype(BF16), wout_ref[0:fw, :]) + _dot(so_ref[...].astype(BF16), wout_ref[fw:, :])
    xn = _rms(y1, g_ref[...]).astype(BF16)
    q = _dot(xn, wq_ref[...])
    qs = []
    for h in range(MEM_HEADS):
        qh = _rms(q[:, h * MEM_DIM:(h + 1) * MEM_DIM], gq_ref[...]) * MEM_DIM ** -0.5
        qs.append(qh.astype(BF16))
    return y1, qs


def _memattn_prompt_kernel(x_ref, fo_ref, so_ref, wout_ref, g_ref, wq_ref, gq_ref, mk_ref, mv_ref, wo_ref, o_ref):
    y1, qs = _mix_out(x_ref, fo_ref, so_ref, wout_ref, g_ref, wq_ref, gq_ref)
    outs = []
    for h in range(MEM_HEADS):
        sl = slice(h * MEM_DIM, (h + 1) * MEM_DIM)
        s = _dot_nt(qs[h], mk_ref[:, sl].astype(BF16))
        p = jnp.exp(s - jnp.max(s, axis=1, keepdims=True))
        o = _dot(p.astype(BF16), mv_ref[:, sl].astype(BF16)) / jnp.sum(p, axis=1, keepdims=True)
        outs.append(o.astype(BF16))
    o_ref[...] = y1 + _dot(jnp.concatenate(outs, axis=1), wo_ref[...])


def _memattn_prompt(x, fo, so, mk, mv, p, tm, length, n_mem):
    n, d = x.shape
    fw = fo.shape[1]
    mw = MEM_HEADS * MEM_DIM
    per = length // tm
    row = lambda i: (i, 0)
    return pl.pallas_call(
        _memattn_prompt_kernel,
        grid=(n // tm,),
        in_specs=[pl.BlockSpec((tm, d), row), pl.BlockSpec((tm, fw), row), pl.BlockSpec((tm, fw), row),
                  _const_spec((2 * fw, d)), _const_spec((1, d)), _const_spec((d, mw)), _const_spec((1, MEM_DIM)),
                  pl.BlockSpec((n_mem, mw), lambda i: (i // per, 0)),
                  pl.BlockSpec((n_mem, mw), lambda i: (i // per, 0)),
                  _const_spec((mw, d))],
        out_specs=pl.BlockSpec((tm, d), row),
        out_shape=jax.ShapeDtypeStruct((n, d), F32),
        compiler_params=_cparams(("parallel",)),
        name="memattn_prompt",
    )(x, fo, so, p['w_out'], p['norm_mem_g'], p['w_mem_q'], p['mem_q_norm_g'], mk, mv, p['w_mem_o'])


def _memattn_sample_kernel(x_ref, fo_ref, so_ref, wout_ref, g_ref, wq_ref, gq_ref, mk_ref, mv_ref, wo_ref, o_ref,
                           *, t_new):
    y1, qs = _mix_out(x_ref, fo_ref, so_ref, wout_ref, g_ref, wq_ref, gq_ref)
    tm = y1.shape[0]
    per = SUBLANES // t_new
    mine = [_div_pow2(_iota((SUBLANES, 1), 0), t_new) == u for u in range(per)]
    tiles = []
    for r in range(tm // SUBLANES):
        outs = []
        for h in range(MEM_HEADS):
            qh = qs[h][r * SUBLANES:(r + 1) * SUBLANES, :]
            s = None
            for u in range(per):
                su = _dot_nt(qh, mk_ref[r * per + u, :, h, :].astype(BF16))
                s = su if s is None else jnp.where(mine[u], su, s)
            p = jnp.exp(s - jnp.max(s, axis=1, keepdims=True))
            pb = p.astype(BF16)
            o = None
            for u in range(per):
                ou = _dot(pb, mv_ref[r * per + u, :, h, :].astype(BF16))
                o = ou if o is None else jnp.where(mine[u], ou, o)
            outs.append((o / jnp.sum(p, axis=1, keepdims=True)).astype(BF16))
        tiles.append(jnp.concatenate(outs, axis=1))
    o_ref[...] = y1 + _dot(jnp.concatenate(tiles, axis=0), wo_ref[...])


def _memattn_sample(x, fo, so, mk, mv, p, tm, t_new):
    n, d = x.shape
    fw = fo.shape[1]
    mw = MEM_HEADS * MEM_DIM
    n_mem = mk.shape[1]
    bb = tm // t_new
    row = lambda i: (i, 0)
    return pl.pallas_call(
        functools.partial(_memattn_sample_kernel, t_new=t_new),
        grid=(n // tm,),
        in_specs=[pl.BlockSpec((tm, d), row), pl.BlockSpec((tm, fw), row), pl.BlockSpec((tm, fw), row),
                  _const_spec((2 * fw, d)), _const_spec((1, d)), _const_spec((d, mw)), _const_spec((1, MEM_DIM)),
                  pl.BlockSpec((bb, n_mem, MEM_HEADS, MEM_DIM), lambda i: (i, 0, 0, 0)),
                  pl.BlockSpec((bb, n_mem, MEM_HEADS, MEM_DIM), lambda i: (i, 0, 0, 0)),
                  _const_spec((mw, d))],
        out_specs=pl.BlockSpec((tm, d), row),
        out_shape=jax.ShapeDtypeStruct((n, d), F32),
        compiler_params=_cparams(("parallel",)),
        name="memattn_sample",
    )(x, fo, so, p['w_out'], p['norm_mem_g'], p['w_mem_q'], p['mem_q_norm_g'], mk, mv, p['w_mem_o'])


FF_CHUNK = 1024


def _ff_chunks(dff):
    return [(s, min(s + FF_CHUNK, dff)) for s in range(0, dff, FF_CHUNK)]


def _ffn_prompt_kernel(x_ref, g_ref, wg_ref, wu_ref, cw_ref, cb_ref, wd_ref, o_ref, tail_ref, halo, *, per):
    i = pl.program_id(0)
    tm = x_ref.shape[0]
    dff = wg_ref.shape[1]

    @pl.when(i % per == 0)
    def _():
        halo[...] = jnp.zeros_like(halo)

    x = x_ref[...]
    xn = _rms(x, g_ref[...]).astype(BF16)
    row8 = _iota((SUBLANES, 1), 0)
    acc = x
    for lo, hi in _ff_chunks(dff):
        gate = _dot(xn, wg_ref[:, lo:hi])
        up = _dot(xn, wu_ref[:, lo:hi])
        prev = halo[:, lo:hi]
        conv = cb_ref[:, lo:hi] + cw_ref[FFN_CONV - 1:FFN_CONV, lo:hi] * gate
        for sh in range(1, FFN_CONV):
            rg = pltpu.roll(gate, sh, 0)
            fix = jnp.where(row8 < sh, pltpu.roll(prev, sh, 0), rg[0:SUBLANES, :])
            conv = conv + cw_ref[FFN_CONV - 1 - sh:FFN_CONV - sh, lo:hi] * jnp.concatenate([fix, rg[SUBLANES:, :]], axis=0)
        halo[:, lo:hi] = gate[tm - SUBLANES:tm, :]
        hmid = (_silu(conv) * up).astype(BF16)
        acc = acc + _dot(hmid, wd_ref[lo:hi, :])
    o_ref[...] = acc
    tail_ref[...] = halo[...]


def _ffn_prompt(x, p, tm, batch, length):
    n, d = x.shape
    dff = p['w_ffn_gate'].shape[1]
    per = length // tm
    row = lambda i: (i, 0)
    return pl.pallas_call(
        functools.partial(_ffn_prompt_kernel, per=per),
        grid=(n // tm,),
        in_specs=[pl.BlockSpec((tm, d), row), _const_spec((1, d)),
                  pl.BlockSpec((d, dff), lambda i: (0, 0), pipeline_mode=pl.Buffered(1)),
                  pl.BlockSpec((d, dff), lambda i: (0, 0), pipeline_mode=pl.Buffered(1)),
                  _const_spec((FFN_CONV, dff)), _const_spec((1, dff)),
                  pl.BlockSpec((dff, d), lambda i: (0, 0), pipeline_mode=pl.Buffered(1))],
        out_specs=[pl.BlockSpec((tm, d), row),
                   pl.BlockSpec((None, SUBLANES, dff), lambda i: (i // per, 0, 0))],
        out_shape=[jax.ShapeDtypeStruct((n, d), F32),
                   jax.ShapeDtypeStruct((batch, SUBLANES, dff), F32)],
        scratch_shapes=[pltpu.VMEM((SUBLANES, dff), F32)],
        compiler_params=_cparams(("arbitrary",)),
        name="ffn_prompt",
    )(x, p['norm_ffn_g'], p['w_ffn_gate'], p['w_ffn_up'], p['ffn_conv_w'], p['ffn_conv_b'], p['w_ffn_down'])


def _ffn_sample_kernel(x_ref, buf_ref, g_ref, wg_ref, wu_ref, cw_ref, cb_ref, wd_ref, o_ref, nbuf_ref, *, t_new):
    nb = x_ref.shape[0] // t_new
    dff = wg_ref.shape[1]
    x = x_ref[...]
    xn = _rms(x, g_ref[...]).astype(BF16)
    acc = x
    for lo, hi in _ff_chunks(dff):
        gate = _dot(xn, wg_ref[:, lo:hi])
        up = _dot(xn, wu_ref[:, lo:hi])
        ext = jnp.concatenate([buf_ref[:, lo:hi], gate], axis=0)
        conv = cb_ref[:, lo:hi]
        for j in range(FFN_CONV):
            conv = conv + cw_ref[j:j + 1, lo:hi] * ext[j * nb:(j + t_new) * nb, :]
        nbuf_ref[:, lo:hi] = ext[t_new * nb:, :]
        hmid = (_silu(conv) * up).astype(BF16)
        acc = acc + _dot(hmid, wd_ref[lo:hi, :])
    o_ref[...] = acc


def _ffn_sample(x_tm, buf_tm, p, t_new):
    n, d = x_tm.shape
    dff = p['w_ffn_gate'].shape[1]
    nbr = buf_tm.shape[0]
    return pl.pallas_call(
        functools.partial(_ffn_sample_kernel, t_new=t_new),
        grid=(1,),
        in_specs=[_const_spec((n, d)), _const_spec((nbr, dff)), _const_spec((1, d)),
                  pl.BlockSpec((d, dff), lambda i: (0, 0), pipeline_mode=pl.Buffered(1)),
                  pl.BlockSpec((d, dff), lambda i: (0, 0), pipeline_mode=pl.Buffered(1)),
                  _const_spec((FFN_CONV, dff)), _const_spec((1, dff)),
                  pl.BlockSpec((dff, d), lambda i: (0, 0), pipeline_mode=pl.Buffered(1))],
        out_specs=[_const_spec((n, d)), _const_spec((nbr, dff))],
        out_shape=[jax.ShapeDtypeStruct((n, d), F32), jax.ShapeDtypeStruct((nbr, dff), F32)],
        compiler_params=_cparams(("arbitrary",)),
        name="ffn_sample",
    )(x_tm, buf_tm, p['norm_ffn_g'], p['w_ffn_gate'], p['w_ffn_up'], p['ffn_conv_w'], p['ffn_conv_b'],
      p['w_ffn_down'])


def _prep_layer(l, norm_mix_g, w_in, b_forget, fox_q_norm_g, fox_k_norm_g, ssd_conv_w, ssd_conv_b, ssd_dt_bias,
                ssd_a_log, ssd_d, ssd_norm_g, w_out, norm_mem_g, mem_in_norm_g, w_mem_q, w_mem_kv, mem_q_norm_g,
                mem_k_norm_g, w_mem_o, norm_ffn_g, w_ffn_gate, w_ffn_up, ffn_conv_w, ffn_conv_b, w_ffn_down):
    fw = FORGET_HEADS * FOX_DIM
    hw = SSD_HEADS * SSD_DIM
    w = w_in[l]
    o_f = 3 * fw
    o_z = o_f + FORGET_HEADS
    o_x = o_z + hw
    o_dt = w.shape[1] - SSD_HEADS
    w_small = jnp.concatenate([w[:, o_f:o_z], w[:, o_dt:]], axis=1)
    b_small = jnp.concatenate([b_forget[l], ssd_dt_bias[l]]).astype(F32)
    a_neg = -jnp.exp(ssd_a_log[l].astype(F32))
    a16 = jnp.concatenate([jnp.zeros((FORGET_HEADS,), F32), a_neg])
    head_of = jnp.arange(fw) // FOX_DIM
    row = lambda v: v.reshape(1, -1).astype(F32)
    return {
        'norm_mix_g': row(norm_mix_g[l]),
        'w_big': jnp.concatenate([w[:, :o_f], w[:, o_z:o_x], w[:, o_x:o_dt]], axis=1).astype(BF16),
        'w_sm': jnp.pad(w_small, ((0, 0), (0, LANES - w_small.shape[1]))).astype(BF16),
        'w_smt': w_small.T.astype(BF16),
        'b_sm': jnp.pad(b_small, (0, LANES - b_small.shape[0])).reshape(1, LANES),
        'b_smt': b_small.reshape(-1, 1),
        'gq': row(jnp.tile(fox_q_norm_g[l], FORGET_HEADS)),
        'gk': row(jnp.tile(fox_k_norm_g[l], FORGET_HEADS)),
        'grp': ((head_of[:, None] == head_of[None, :]).astype(F32) / FOX_DIM).astype(BF16),
        'ssd_conv_w': ssd_conv_w[l].astype(F32),
        'ssd_conv_b': row(ssd_conv_b[l]),
        'a_row': jnp.pad(a16, (0, LANES - a16.shape[0])).reshape(1, LANES),
        'a_col': a16.reshape(-1, 1),
        'd_row': row(jnp.repeat(ssd_d[l], SSD_DIM)),
        'ssd_norm_g': row(ssd_norm_g[l]),
        'w_out': w_out[l].astype(BF16),
        'norm_mem_g': row(norm_mem_g[l]),
        'mem_in_norm_g': row(mem_in_norm_g[l]),
        'w_mem_q': w_mem_q[l].astype(BF16),
        'w_mem_kv': w_mem_kv[l].astype(BF16),
        'mem_q_norm_g': row(mem_q_norm_g[l]),
        'mem_k_norm_g': row(mem_k_norm_g[l]),
        'w_mem_o': w_mem_o[l].astype(BF16),
        'norm_ffn_g': row(norm_ffn_g[l]),
        'w_ffn_gate': w_ffn_gate[l].astype(BF16),
        'w_ffn_up': w_ffn_up[l].astype(BF16),
        'ffn_conv_w': ffn_conv_w[l].astype(F32),
        'ffn_conv_b': row(ffn_conv_b[l]),
        'w_ffn_down': w_ffn_down[l].astype(BF16),
    }


def _row_tile(n, want):
    t = min(n, want)
    while n % t:
        t //= 2
    return t


def _prompt_layer(x, mem, p):
    batch, length, d = x.shape
    n = batch * length
    n_mem = mem.shape[1]
    fw = FORGET_HEADS * FOX_DIM
    hw = SSD_HEADS * SSD_DIM
    tm = _row_tile(length, 512)
    xf = x.reshape(n, d)
    q, k, v, z, xbc, sm, smt = _inproj(xf, p, tm)
    ccol, crow = _cumsum(sm, smt, batch, length)
    tq = _row_tile(length, 512)
    nq = length // tq
    crow4 = crow.reshape(2 * SUBLANES, batch, nq, tq).transpose(1, 2, 0, 3)
    fox_o = _fox_prompt(q, k, v, ccol, crow4, batch, length, tq)
    nc = length // CHUNK
    smt4 = smt.reshape(2 * SUBLANES, batch, nc, CHUNK).transpose(1, 2, 0, 3)
    cdim = xbc.shape[1]
    xbc3 = xbc.reshape(batch, length, cdim)
    ssd_o, ssm_fin = _ssd(xbc3, z.reshape(batch, length, hw), sm.reshape(batch, length, LANES), smt4,
                          jnp.zeros((batch, hw, SSD_STATE), F32), jnp.zeros((batch, SUBLANES, cdim), F32), p)
    mk, mv = _mem_kv(mem.reshape(batch * n_mem, d), p, _row_tile(n_mem, 512))
    y2 = _memattn_prompt(xf, fox_o, ssd_o.reshape(n, hw), mk, mv, p, tm, length, n_mem)
    y3, ffn_tail = _ffn_prompt(y2, p, tm, batch, length)
    return (y3.reshape(batch, length, d),
            k.reshape(batch, length, FORGET_HEADS, FOX_DIM),
            v.reshape(batch, length, FORGET_HEADS, FOX_DIM),
            sm.reshape(batch, length, LANES)[:, :, :FORGET_HEADS],
            ssm_fin.reshape(batch, SSD_HEADS, SSD_DIM, SSD_STATE),
            xbc3[:, length - (SSD_CONV - 1):, :],
            ffn_tail[:, SUBLANES - (FFN_CONV - 1):, :],
            mk.reshape(batch, n_mem, MEM_HEADS, MEM_DIM),
            mv.reshape(batch, n_mem, MEM_HEADS, MEM_DIM))


def _sample_layer(x, kc, vc, lc, st_ssm, st_conv, st_ffn, mem_k, mem_v, page_table, p):
    batch, t_new, d = x.shape
    n = batch * t_new
    fw = FORGET_HEADS * FOX_DIM
    hw = SSD_HEADS * SSD_DIM
    xf = x.reshape(n, d)
    q, k, v, z, xbc, sm, smt = _inproj(xf, p, _row_tile(n, 512))
    q3 = q.reshape(batch, t_new, 1, fw)
    own = (jnp.arange(fw) // FOX_DIM)[None, :] == jnp.arange(FORGET_HEADS)[:, None]
    qblk = jnp.where(own[None, None], q3, jnp.zeros((), q.dtype)).reshape(batch, t_new * FORGET_HEADS, fw)
    pad_rows = lambda a: jnp.pad(a.reshape(batch, t_new, fw), ((0, 0), (0, LANES - t_new), (0, 0)))
    lft = jnp.pad(smt[:FORGET_HEADS].reshape(FORGET_HEADS, batch, t_new).transpose(1, 0, 2),
                  ((0, 0), (0, 0), (0, LANES - t_new)))
    fox_o = _fox_sample(page_table, qblk, pad_rows(k), pad_rows(v), lft, kc, vc, lc)
    cdim = xbc.shape[1]
    xbc3 = xbc.reshape(batch, t_new, cdim)
    smt4 = smt.reshape(2 * SUBLANES, batch, 1, t_new).transpose(1, 2, 0, 3)
    tail0 = jnp.pad(st_conv, ((0, 0), (SUBLANES - (SSD_CONV - 1), 0), (0, 0)))
    ssd_o, ssm_new = _ssd(xbc3, z.reshape(batch, t_new, hw), sm.reshape(batch, t_new, LANES), smt4,
                          st_ssm.reshape(batch, hw, SSD_STATE), tail0, p)
    y2 = _memattn_sample(xf, fox_o.reshape(n, fw), ssd_o.reshape(n, hw), mem_k, mem_v, p, _row_tile(n, 32), t_new)
    x_tm = y2.reshape(batch, t_new, d).transpose(1, 0, 2).reshape(n, d)
    dff = st_ffn.shape[-1]
    buf_tm = st_ffn.transpose(1, 0, 2).reshape((FFN_CONV - 1) * batch, dff)
    y3_tm, nbuf_tm = _ffn_sample(x_tm, buf_tm, p, t_new)
    conv_new = jnp.concatenate([st_conv, xbc3], axis=1)[:, t_new:, :]
    return (y3_tm.reshape(t_new, batch, d).transpose(1, 0, 2),
            k.reshape(batch, t_new, FORGET_HEADS, FOX_DIM),
            v.reshape(batch, t_new, FORGET_HEADS, FOX_DIM),
            sm.reshape(batch, t_new, LANES)[:, :, :FORGET_HEADS],
            ssm_new.reshape(batch, SSD_HEADS, SSD_DIM, SSD_STATE),
            conv_new,
            nbuf_tm.reshape(FFN_CONV - 1, batch, dff).transpose(1, 0, 2))


def kernel(x_prompt, mem_prompt, x_sample, cache_fox_k, cache_fox_v, cache_fox_logf, state_ssm, state_ssm_conv, state_ffn_conv, cache_mem_k, cache_mem_v, page_table, norm_mix_g, w_in, b_forget, fox_q_norm_g, fox_k_norm_g, ssd_conv_w, ssd_conv_b, ssd_dt_bias, ssd_a_log, ssd_d, ssd_norm_g, w_out, norm_mem_g, mem_in_norm_g, w_mem_q, w_mem_kv, mem_q_norm_g, mem_k_norm_g, w_mem_o, norm_ffn_g, w_ffn_gate, w_ffn_up, ffn_conv_w, ffn_conv_b, w_ffn_down):
    depth = w_in.shape[0]
    yp, ys = x_prompt, x_sample
    kc_all = jnp.transpose(cache_fox_k, (0, 1, 3, 4, 2))
    vc_all = jnp.transpose(cache_fox_v, (0, 1, 3, 4, 2))
    lc_all = jnp.transpose(cache_fox_logf, (0, 1, 3, 2))
    pouts, souts = [], []
    for l in range(depth):
        p = _prep_layer(l, norm_mix_g, w_in, b_forget, fox_q_norm_g, fox_k_norm_g, ssd_conv_w, ssd_conv_b,
                        ssd_dt_bias, ssd_a_log, ssd_d, ssd_norm_g, w_out, norm_mem_g, mem_in_norm_g, w_mem_q,
                        w_mem_kv, mem_q_norm_g, mem_k_norm_g, w_mem_o, norm_ffn_g, w_ffn_gate, w_ffn_up,
                        ffn_conv_w, ffn_conv_b, w_ffn_down)
        po = _prompt_layer(yp, mem_prompt, p)
        yp = po[0]
        pouts.append(po[1:])
        so = _sample_layer(ys, kc_all[l], vc_all[l], lc_all[l], state_ssm[l], state_ssm_conv[l], state_ffn_conv[l],
                           cache_mem_k[l], cache_mem_v[l], page_table, p)
        ys = so[0]
        souts.append(so[1:])
    stack = lambda outs, i: jnp.stack([o[i] for o in outs])
    return (yp, ys) + tuple(stack(pouts, i) for i in range(8)) + tuple(stack(souts, i) for i in range(6))
```

```python
import functools

import numpy as np
import jax
import jax.numpy as jnp
from jax import lax
from jax.experimental import pallas as pl
from jax.experimental.pallas import tpu as pltpu

F32 = jnp.float32
BF16 = jnp.bfloat16
EPS = 1e-6
LOG2E = 1.4426950408889634
FORGET_HEADS = 8
FOX_DIM = 64
SSD_HEADS = 8
SSD_DIM = 64
SSD_STATE = 128
SSD_CONV = 4
CHUNK = 128
MEM_HEADS = 4
MEM_DIM = 128
FFN_CONV = 3
LANES = 128
SUBLANES = 8
VMEM_LIMIT = 56 * 1024 * 1024
NT = (((1,), (1,)), ((), ()))
TN = (((0,), (0,)), ((), ()))


def _dot(a, b):
    return jnp.dot(a, b, preferred_element_type=F32)


def _dot_nt(a, b):
    return lax.dot_general(a, b, NT, preferred_element_type=F32)


def _dot_tn(a, b):
    return lax.dot_general(a, b, TN, preferred_element_type=F32)


def _split3(x):
    hi = x.astype(BF16)
    r1 = x - hi.astype(F32)
    mid = r1.astype(BF16)
    lo = (r1 - mid.astype(F32)).astype(BF16)
    return hi, mid, lo


def _dot3_left(x, m):
    hi, mid, lo = _split3(x)
    return _dot(hi, m) + _dot(mid, m) + _dot(lo, m)


def _dot3_right(m, x):
    hi, mid, lo = _split3(x)
    return _dot(m, hi) + _dot(m, mid) + _dot(m, lo)


def _dot2_left(x, m):
    hi = x.astype(BF16)
    lo = (x - hi.astype(F32)).astype(BF16)
    return _dot(hi, m) + _dot(lo, m)


def _rms(x, g):
    return x * lax.rsqrt(jnp.mean(x * x, axis=-1, keepdims=True) + EPS) * g


def _softplus(x):
    return jnp.maximum(x, 0.0) + jnp.log1p(jnp.exp(-jnp.abs(x)))


def _log_sigmoid(x):
    return -_softplus(-x)


def _silu(x):
    return x * (1.0 / (1.0 + jnp.exp(-x)))


def _iota(shape, dim):
    return lax.broadcasted_iota(jnp.int32, shape, dim)


def _div_pow2(x, d):
    assert d & (d - 1) == 0
    return lax.shift_right_logical(x, jnp.int32(d.bit_length() - 1))


def _cparams(sem):
    return pltpu.CompilerParams(dimension_semantics=sem, vmem_limit_bytes=VMEM_LIMIT)


def _const_spec(shape):
    n = len(shape)
    return pl.BlockSpec(shape, lambda *_: (0,) * n)


def _inproj_kernel(x_ref, g_ref, wq_ref, wkvt_ref, wzx_ref, wsm_ref, wsmt_ref, gq_ref, gkc_ref, red_ref, exp_ref,
                   bsm_ref, bsmt_ref,
                   q_ref, kt_ref, vt_ref, ktb_ref, vtb_ref, z_ref, xbc_ref, sm_ref, smt_ref):
    fw = FORGET_HEADS * FOX_DIM
    tm = x_ref.shape[0]
    xn = _rms(x_ref[...], g_ref[...]).astype(BF16)

    q = _dot(xn, wq_ref[...])
    rs = lax.rsqrt(_dot2_left(q * q, red_ref[...]) + EPS)
    q_ref[...] = (q * _dot2_left(rs, exp_ref[...]) * gq_ref[...] * (FOX_DIM ** -0.5 * LOG2E)).astype(BF16)

    kv = _dot_nt(wkvt_ref[...], xn)
    k3 = kv[0:fw, :].reshape(FORGET_HEADS, FOX_DIM, tm)
    k3 = k3 * lax.rsqrt(jnp.mean(k3 * k3, axis=1, keepdims=True) + EPS)
    kn = k3.reshape(fw, tm) * gkc_ref[...]
    kt_ref[...] = kn
    ktb_ref[...] = kn.astype(BF16)
    vt = kv[fw:, :]
    vt_ref[...] = vt
    vtb_ref[...] = vt.astype(BF16)

    z_ref[...] = _dot(xn, wzx_ref[:, 0:fw])
    xbc_ref[...] = _dot(xn, wzx_ref[:, fw:])
    sm = _dot(xn, wsm_ref[...]) + bsm_ref[...]
    sm_ref[...] = jnp.where(_iota(sm.shape, 1) < FORGET_HEADS, _log_sigmoid(sm), _softplus(sm))
    smt = _dot_nt(wsmt_ref[...], xn) + bsmt_ref[...]
    smt_ref[...] = jnp.where(_iota(smt.shape, 0) < FORGET_HEADS, _log_sigmoid(smt), _softplus(smt))


def _inproj(x, p, tm, batch, length):
    n, d = x.shape
    fw = FORGET_HEADS * FOX_DIM
    nzx = p['w_zx'].shape[1]
    per = length // tm
    row = lambda i: (i, 0)
    tr = lambda i: (i // per, 0, i % per)
    return pl.pallas_call(
        _inproj_kernel,
        grid=(n // tm,),
        in_specs=[
            pl.BlockSpec((tm, d), row),
            _const_spec((1, d)),
            _const_spec((d, fw)),
            _const_spec((2 * fw, d)),
            _const_spec((d, nzx)),
            _const_spec((d, LANES)),
            _const_spec((2 * SUBLANES, d)),
            _const_spec((1, fw)),
            _const_spec((fw, 1)),
            _const_spec((fw, LANES)),
            _const_spec((LANES, fw)),
            _const_spec((1, LANES)),
            _const_spec((2 * SUBLANES, 1)),
        ],
        out_specs=[
            pl.BlockSpec((tm, fw), row),
            pl.BlockSpec((None, fw, tm), tr),
            pl.BlockSpec((None, fw, tm), tr),
            pl.BlockSpec((None, fw, tm), tr),
            pl.BlockSpec((None, fw, tm), tr),
            pl.BlockSpec((tm, fw), row),
            pl.BlockSpec((tm, nzx - fw), row),
            pl.BlockSpec((tm, LANES), row),
            pl.BlockSpec((2 * SUBLANES, tm), lambda i: (0, i)),
        ],
        out_shape=[
            jax.ShapeDtypeStruct((n, fw), BF16),
            jax.ShapeDtypeStruct((batch, fw, length), F32),
            jax.ShapeDtypeStruct((batch, fw, length), F32),
            jax.ShapeDtypeStruct((batch, fw, length), BF16),
            jax.ShapeDtypeStruct((batch, fw, length), BF16),
            jax.ShapeDtypeStruct((n, fw), F32),
            jax.ShapeDtypeStruct((n, nzx - fw), F32),
            jax.ShapeDtypeStruct((n, LANES), F32),
            jax.ShapeDtypeStruct((2 * SUBLANES, n), F32),
        ],
        compiler_params=_cparams(("parallel",)),
        name="inproj",
    )(x, p['norm_mix_g'], p['w_q'], p['w_kvt'], p['w_zx'], p['w_sm'], p['w_smt'], p['gq'], p['gk_col'],
      p['head_reduce'], p['head_expand'], p['b_sm'], p['b_smt'])


def _cumsum_kernel(sm_ref, smt_ref, col_ref, row_ref):
    length = sm_ref.shape[0]
    r = _iota((LANES, LANES), 0)
    c = _iota((LANES, LANES), 1)
    lower = (c <= r).astype(BF16)
    upper = (r <= c).astype(BF16)
    carry_c = jnp.zeros((1, LANES), F32)
    carry_r = jnp.zeros((smt_ref.shape[0], 1), F32)
    for b in range(length // LANES):
        sl = slice(b * LANES, (b + 1) * LANES)
        cs = _dot3_right(lower, sm_ref[sl, :]) + carry_c
        col_ref[sl, :] = cs * LOG2E
        carry_c = cs[LANES - 1:LANES, :]
        rs = _dot3_left(smt_ref[:, sl], upper) + carry_r
        row_ref[:, sl] = rs * LOG2E
        carry_r = rs[:, LANES - 1:LANES]


def _cumsum(sm, smt, batch, length):
    n = sm.shape[0]
    return pl.pallas_call(
        _cumsum_kernel,
        grid=(batch,),
        in_specs=[pl.BlockSpec((length, LANES), lambda b: (b, 0)),
                  pl.BlockSpec((2 * SUBLANES, length), lambda b: (0, b))],
        out_specs=[pl.BlockSpec((length, LANES), lambda b: (b, 0)),
                   pl.BlockSpec((2 * SUBLANES, length), lambda b: (0, b))],
        out_shape=[jax.ShapeDtypeStruct((n, LANES), F32),
                   jax.ShapeDtypeStruct((2 * SUBLANES, n), F32)],
        compiler_params=_cparams(("parallel",)),
        name="logf_cumsum",
    )(sm, smt)


PAGES_PER_CHUNK = 16


def _fox_kernel(pt_ref, pb_ref, ph_ref, pi_ref, pj_ref, pv_ref,
                qb_ref, kn_ref, vn_ref, lft_ref, later_ref, lpages_ref, kc_ref, vc_ref, lc_ref,
                q_ref, kt_ref, vt_ref, cc_ref, cr_ref,
                os_ref, op_ref,
                kbuf, vbuf, lbuf, suf, m_s, l_s, acc_s, cn_s, knp, vnp, pm, pls, pacc, pcq, ksem, vsem, lsem,
                *, n_chunks, t_new):
    g = pl.program_id(0)
    n_steps = pl.num_programs(0)
    b = g // n_chunks
    c = g % n_chunks
    slot = g % 2
    ppc = PAGES_PER_CHUNK
    n_pages = n_chunks * ppc
    page = kbuf.shape[-1] // ppc
    heads = FORGET_HEADS
    fdim = heads * FOX_DIM

    def kv_copies(bb, cc, sl):
        cps = []
        for j in range(ppc):
            pg = pt_ref[bb, cc * ppc + j]
            cps.append(pltpu.make_async_copy(kc_ref.at[pg], kbuf.at[sl, :, :, pl.ds(j * page, page)], ksem.at[sl]))
            cps.append(pltpu.make_async_copy(vc_ref.at[pg], vbuf.at[sl, :, :, pl.ds(j * page, page)], vsem.at[sl]))
        return cps

    def lf_copies(bb, sl):
        return [pltpu.make_async_copy(lc_ref.at[pt_ref[bb, j]], lbuf.at[sl, pl.ds(j * heads, heads), :], lsem.at[sl])
                for j in range(n_pages)]

    @pl.when(g == 0)
    def _():
        for cp in lf_copies(0, 0):
            cp.start()
        for cp in kv_copies(0, 0, 0):
            cp.start()
        pm[...] = jnp.zeros_like(pm)
        pls[...] = jnp.zeros_like(pls)
        pacc[...] = jnp.zeros_like(pacc)
        pcq[...] = jnp.zeros_like(pcq)

    last_c = c == n_chunks - 1
    nb_b = jnp.where(last_c, b + 1, b)
    nb_c = jnp.where(last_c, 0, c + 1)

    @pl.when(g + 1 < n_steps)
    def _():
        for cp in kv_copies(nb_b, nb_c, 1 - slot):
            cp.start()

    bslot = b % 2

    @pl.when(jnp.logical_and(c == 0, g + n_chunks < n_steps))
    def _():
        for cp in lf_copies(b + 1, 1 - bslot):
            cp.start()

    @pl.when(c == 0)
    def _():
        pltpu.make_async_copy(lbuf.at[1 - bslot], lbuf.at[bslot], lsem.at[bslot]).wait()
        lf = lbuf[bslot]
        within = _dot3_left(lf, later_ref[...])
        tot = within[:, 0:1] + lf[:, 0:1]
        suf[...] = (within + _dot3_right(lpages_ref[...], jnp.broadcast_to(tot, lf.shape))) * LOG2E
        rl = _iota((LANES, LANES), 0)
        cl = _iota((LANES, LANES), 1)
        cn_s[...] = _dot3_left(lft_ref[...], (rl <= cl).astype(BF16)) * LOG2E
        m_s[...] = jnp.full(m_s.shape, -jnp.inf, F32)
        l_s[...] = jnp.zeros(l_s.shape, F32)
        acc_s[...] = jnp.zeros(acc_s.shape, F32)

    cn = cn_s[...]
    cn_col = jnp.concatenate([cn[:, t:t + 1] for t in range(t_new)], axis=0)
    qb = qb_ref[...]

    pltpu.make_async_copy(kbuf.at[1 - slot], kbuf.at[slot], ksem.at[slot]).wait()
    pltpu.make_async_copy(vbuf.at[1 - slot], vbuf.at[slot], vsem.at[slot]).wait()

    def online(s, pv):
        m = m_s[...]
        m_new = jnp.maximum(m, jnp.max(s, axis=1, keepdims=True))
        alpha = jnp.exp2(m - m_new)
        p = jnp.exp2(s - m_new)
        l_s[...] = alpha * l_s[...] + jnp.sum(p, axis=1, keepdims=True)
        acc_s[...] = alpha * acc_s[...] + pv(p.astype(BF16))
        m_s[...] = m_new

    def sample_chunk():
        kt = kbuf[slot].reshape(fdim, ppc * page).astype(BF16)
        vt = vbuf[slot].reshape(fdim, ppc * page).astype(BF16)
        sfx = jnp.concatenate([suf[pl.ds((c * ppc + j) * heads, heads), :] for j in range(ppc)], axis=1)
        s = _dot(qb, kt) + jnp.concatenate([sfx] * t_new, axis=0) + cn_col
        online(s, lambda p: _dot_nt(p, vt))

    hp = ph_ref[g]
    pi = pi_ref[g]
    pj = pj_ref[g]
    valid = pv_ref[g] == 1
    diag = pj == pi
    tq = q_ref.shape[0]
    first = _iota((1, LANES), 1) < FOX_DIM

    def prompt_pair(masked):
        q = q_ref[...]
        zero = jnp.zeros_like(q)
        qm = (jnp.where(first, q, zero), jnp.where(first, zero, q))
        kt = kt_ref[...]
        vt = vt_ref[...]
        cr = cr_ref[...]
        tk = kt.shape[1]
        sub = _iota(cr.shape, 0)
        fresh = pj == 0

        @pl.when(fresh)
        def _():
            cc = cc_ref[...]
            lane_c = _iota(cc.shape, 1)
            for hh in range(2):
                cq = jnp.sum(jnp.where(lane_c == 2 * hp + hh, cc, 0.0), axis=1, keepdims=True)
                pcq[hh] = jnp.broadcast_to(cq, (tq, LANES))

        wide = lambda a: jnp.concatenate([a] * (tk // LANES), axis=1)
        outs = []
        for hh in range(2):
            ck = jnp.sum(jnp.where(sub == 2 * hp + hh, cr, 0.0), axis=0, keepdims=True)
            s = _dot(qm[hh], kt) + (wide(pcq[hh]) - ck)
            if masked:
                s = jnp.where(_iota(s.shape, 1) <= _iota(s.shape, 0), s, -jnp.inf)
            m = jnp.where(fresh, -jnp.inf, pm[hh])
            l = jnp.where(fresh, 0.0, pls[hh])
            acc = jnp.where(fresh, 0.0, pacc[hh])
            m_new = jnp.maximum(m, jnp.max(s, axis=1, keepdims=True))
            alpha = jnp.exp2(m - m_new)
            p = jnp.exp2(s - wide(m_new))
            l = alpha * l + jnp.sum(p, axis=1, keepdims=True)
            acc = alpha * acc + _dot_nt(p.astype(BF16), vt)
            if masked:
                outs.append(acc / l)
            else:
                pm[hh] = m_new
                pls[hh] = l
                pacc[hh] = acc
        if masked:
            op_ref[...] = jnp.where(first, outs[0], outs[1])

    @pl.when(jnp.logical_and(valid, diag))
    def _():
        sample_chunk()
        prompt_pair(True)

    @pl.when(jnp.logical_and(valid, jnp.logical_not(diag)))
    def _():
        sample_chunk()
        prompt_pair(False)

    @pl.when(jnp.logical_not(valid))
    def _():
        sample_chunk()

    @pl.when(last_c)
    def _():
        knp[...] = jnp.zeros_like(knp)
        vnp[...] = jnp.zeros_like(vnp)
        knp[0:t_new, :] = kn_ref[...]
        vnp[0:t_new, :] = vn_ref[...]
        kn = knp[...].astype(BF16)
        vn = vnp[...].astype(BF16)
        s2 = _dot_nt(qb, kn) + cn_col - jnp.concatenate([cn] * t_new, axis=0)
        tok = _div_pow2(_iota(s2.shape, 0), heads)
        s2 = jnp.where(_iota(s2.shape, 1) <= tok, s2, -jnp.inf)
        online(s2, lambda p: _dot(p, vn))
        o = acc_s[...] / l_s[...]
        keep = _div_pow2(_iota((heads, fdim), 1), FOX_DIM) == _iota((heads, fdim), 0)
        os_ref[...] = jnp.concatenate(
            [jnp.sum(jnp.where(keep, o[t * heads:(t + 1) * heads, :], 0.0), axis=0, keepdims=True)
             for t in range(t_new)], axis=0)


def _pair_schedule(batch, n_hp, nq, n_steps):
    rows = [(b, h, i, j, 1) for b in range(batch) for h in range(n_hp) for i in range(nq) for j in range(i + 1)]
    assert len(rows) <= n_steps, "more prompt attention pairs than sample chunks"
    rows += [(batch - 1, n_hp - 1, nq - 1, 0, 0)] * (n_steps - len(rows))
    return [jnp.asarray(np.array(col, np.int32)) for col in zip(*rows)]


def _fox(page_table, qblk, kn, vn, lft, kc, vc, lc, q, ktb, vtb, ccol, crow4, tq):
    bs, rows, fdim = qblk.shape
    t_new = rows // FORGET_HEADS
    n_pages = page_table.shape[1]
    page = kc.shape[-1]
    n_chunks = n_pages // PAGES_PER_CHUNK
    pc = PAGES_PER_CHUNK * page
    n_steps = bs * n_chunks
    bp, _, length = ktb.shape
    nq = length // tq
    n_hp = fdim // LANES
    sched = _pair_schedule(bp, n_hp, nq, n_steps)
    pos = np.arange(page)
    later = jnp.asarray(pos[:, None] > pos[None, :], BF16)
    rows_ph = np.arange(n_pages * FORGET_HEADS)
    lpages = jnp.asarray((rows_ph[None, :] % FORGET_HEADS == rows_ph[:, None] % FORGET_HEADS)
                         & (rows_ph[None, :] // FORGET_HEADS > rows_ph[:, None] // FORGET_HEADS), BF16)
    per_s = lambda g, *_: (g // n_chunks, 0, 0)
    grid_spec = pltpu.PrefetchScalarGridSpec(
        num_scalar_prefetch=6,
        grid=(n_steps,),
        in_specs=[
            pl.BlockSpec((None, rows, fdim), per_s),
            pl.BlockSpec((None, t_new, fdim), per_s),
            pl.BlockSpec((None, t_new, fdim), per_s),
            pl.BlockSpec((None, FORGET_HEADS, LANES), per_s),
            pl.BlockSpec(later.shape, lambda g, *_: (0, 0)),
            pl.BlockSpec(lpages.shape, lambda g, *_: (0, 0)),
            pl.BlockSpec(memory_space=pl.ANY),
            pl.BlockSpec(memory_space=pl.ANY),
            pl.BlockSpec(memory_space=pl.ANY),
            pl.BlockSpec((tq, LANES), lambda g, pt, pb, ph, pi, pj, pv: (pb[g] * nq + pi[g], ph[g])),
            pl.BlockSpec((None, LANES, tq), lambda g, pt, pb, ph, pi, pj, pv: (pb[g], ph[g], pj[g])),
            pl.BlockSpec((None, LANES, tq), lambda g, pt, pb, ph, pi, pj, pv: (pb[g], ph[g], pj[g])),
            pl.BlockSpec((tq, LANES), lambda g, pt, pb, ph, pi, pj, pv: (pb[g] * nq + pi[g], 0)),
            pl.BlockSpec((None, None, 2 * SUBLANES, tq), lambda g, pt, pb, ph, pi, pj, pv: (pb[g], pj[g], 0, 0)),
        ],
        out_specs=[
            pl.BlockSpec((None, t_new, fdim), per_s),
            pl.BlockSpec((tq, LANES), lambda g, pt, pb, ph, pi, pj, pv: (pb[g] * nq + pi[g], ph[g])),
        ],
        scratch_shapes=[
            pltpu.VMEM((2, FORGET_HEADS, FOX_DIM, pc), F32),
            pltpu.VMEM((2, FORGET_HEADS, FOX_DIM, pc), F32),
            pltpu.VMEM((2, n_pages * FORGET_HEADS, page), F32),
            pltpu.VMEM((n_pages * FORGET_HEADS, page), F32),
            pltpu.VMEM((rows, 1), F32),
            pltpu.VMEM((rows, 1), F32),
            pltpu.VMEM((rows, fdim), F32),
            pltpu.VMEM((FORGET_HEADS, LANES), F32),
            pltpu.VMEM((LANES, fdim), F32),
            pltpu.VMEM((LANES, fdim), F32),
            pltpu.VMEM((2, tq, LANES), F32),
            pltpu.VMEM((2, tq, LANES), F32),
            pltpu.VMEM((2, tq, LANES), F32),
            pltpu.VMEM((2, tq, LANES), F32),
            pltpu.SemaphoreType.DMA((2,)),
            pltpu.SemaphoreType.DMA((2,)),
            pltpu.SemaphoreType.DMA((2,)),
        ],
    )
    return pl.pallas_call(
        functools.partial(_fox_kernel, n_chunks=n_chunks, t_new=t_new),
        grid_spec=grid_spec,
        out_shape=[jax.ShapeDtypeStruct((bs, t_new, fdim), F32),
                   jax.ShapeDtypeStruct((bp * length, fdim), F32)],
        compiler_params=_cparams(("arbitrary",)),
        name="fox",
    )(page_table, *sched, qblk, kn, vn, lft, later, lpages, kc, vc, lc, q, ktb, vtb, ccol, crow4)


def _ssd_kernel(xbc_ref, z_ref, sm_ref, smt_ref, st0_ref, tail0_ref, cw_ref, cb_ref, arow_ref, acol_ref,
                dsk_ref, ng_ref, y_ref, fin_ref, state, tail, *pads, t_real):
    q = CHUNK
    c = pl.program_id(1)
    hw = SSD_HEADS * SSD_DIM
    gs = SSD_STATE

    @pl.when(c == 0)
    def _():
        state[...] = st0_ref[...]
        tail[...] = tail0_ref[...]

    if t_real == q:
        x = xbc_ref[...]
        z = z_ref[...]
        sm = sm_ref[...]
        smt = smt_ref[...]
    else:
        xp, zp, smp, smtp = pads
        xp[...] = jnp.zeros_like(xp)
        zp[...] = jnp.zeros_like(zp)
        smp[...] = jnp.zeros_like(smp)
        smtp[...] = jnp.zeros_like(smtp)
        xp[0:t_real, :] = xbc_ref[...]
        zp[0:t_real, :] = z_ref[...]
        smp[0:t_real, :] = sm_ref[...]
        smtp[:, 0:t_real] = smt_ref[...]
        x, z, sm, smt = xp[...], zp[...], smp[...], smtp[...]

    prev = tail[...]
    row8 = _iota(prev.shape, 0)
    conv = cb_ref[...] + cw_ref[SSD_CONV - 1:SSD_CONV, :] * x
    for sh in range(1, SSD_CONV):
        rx = pltpu.roll(x, sh, 0)
        fix = jnp.where(row8 < sh, pltpu.roll(prev, sh, 0), rx[0:SUBLANES, :])
        xs_sh = jnp.concatenate([fix, rx[SUBLANES:, :]], axis=0)
        conv = conv + cw_ref[SSD_CONV - 1 - sh:SSD_CONV - sh, :] * xs_sh
    tail[...] = x[q - SUBLANES:q, :]
    xc = _silu(conv)
    xs = xc[:, 0:hw]

    r = _iota((q, q), 0)
    cidx = _iota((q, q), 1)
    tril = cidx <= r
    lower = tril.astype(BF16)
    upper = (r <= cidx).astype(BF16)
    acum_c = _dot3_right(lower, sm * arow_ref[...])
    acum_r = _dot3_left(smt * acol_ref[...], upper)
    lane = _iota((1, LANES), 1)
    first = lane < SSD_DIM
    rowp = _iota((LANES, 1), 0) < SSD_DIM

    ys = []
    for pr in range(SSD_HEADS // 2):
        g = pr // (SSD_HEADS // 4)
        bg = xc[:, hw + g * gs: hw + (g + 1) * gs].astype(BF16)
        cg = xc[:, hw + 2 * gs + g * gs: hw + 2 * gs + (g + 1) * gs].astype(BF16)
        gmat = _dot_nt(cg, bg)
        xpair = xs[:, pr * LANES:(pr + 1) * LANES]
        xpb = xpair.astype(BF16)
        yd, te, ea, cd = [], [], [], []
        for hh in range(2):
            h = FORGET_HEADS + 2 * pr + hh
            ac = acum_c[:, h:h + 1]
            ar = acum_r[h:h + 1, :]
            dtr = smt[h:h + 1, :]
            dtc = sm[:, h:h + 1]
            last = acum_c[q - 1:q, h:h + 1]
            decay = jnp.exp(jnp.where(tril, ac - ar, -jnp.inf))
            sc = gmat * decay * dtr
            yd.append(_dot(sc.astype(BF16), xpb))
            te.append(jnp.exp(last - ac) * dtc)
            ea.append(jnp.exp(ac))
            cd.append(jnp.exp(last))
        ydiag = jnp.where(first, yd[0], yd[1])
        xsc = (xpair * jnp.where(first, te[0], te[1])).astype(BF16)
        cstate = _dot_tn(xsc, bg)
        prev_st = state[pr * LANES:(pr + 1) * LANES, :]
        yoff = _dot_nt(cg, prev_st.astype(BF16)) * jnp.where(first, ea[0], ea[1])
        state[pr * LANES:(pr + 1) * LANES, :] = prev_st * jnp.where(rowp, cd[0], cd[1]) + cstate
        ys.append(ydiag + yoff + dsk_ref[:, pr * LANES:(pr + 1) * LANES] * xpair)
    y = jnp.concatenate(ys, axis=1) * _silu(z)
    y = _rms(y, ng_ref[...])
    y_ref[...] = y[0:t_real, :]

    @pl.when(c == pl.num_programs(1) - 1)
    def _():
        fin_ref[...] = state[...]


def _ssd(xbc, z, sm, smt4, st0, tail0, p):
    batch, length, cdim = xbc.shape
    hw = z.shape[2]
    t_real = min(length, CHUNK)
    nc = length // t_real
    blk = lambda b, c: (b, c, 0)
    per_b = lambda b, c: (b, 0, 0)
    pads = []
    if t_real != CHUNK:
        pads = [pltpu.VMEM((CHUNK, cdim), F32), pltpu.VMEM((CHUNK, hw), F32),
                pltpu.VMEM((CHUNK, LANES), F32), pltpu.VMEM((2 * SUBLANES, CHUNK), F32)]
    return pl.pallas_call(
        functools.partial(_ssd_kernel, t_real=t_real),
        grid=(batch, nc),
        in_specs=[
            pl.BlockSpec((None, t_real, cdim), blk),
            pl.BlockSpec((None, t_real, hw), blk),
            pl.BlockSpec((None, t_real, LANES), blk),
            pl.BlockSpec((None, None, 2 * SUBLANES, t_real), lambda b, c: (b, c, 0, 0)),
            pl.BlockSpec((None, hw, SSD_STATE), per_b),
            pl.BlockSpec((None, SUBLANES, cdim), per_b),
            _const_spec((SSD_CONV, cdim)),
            _const_spec((1, cdim)),
            _const_spec((1, LANES)),
            _const_spec((2 * SUBLANES, 1)),
            _const_spec((1, hw)),
            _const_spec((1, hw)),
        ],
        out_specs=[pl.BlockSpec((None, t_real, hw), blk),
                   pl.BlockSpec((None, hw, SSD_STATE), per_b)],
        out_shape=[jax.ShapeDtypeStruct((batch, length, hw), F32),
                   jax.ShapeDtypeStruct((batch, hw, SSD_STATE), F32)],
        scratch_shapes=[pltpu.VMEM((hw, SSD_STATE), F32), pltpu.VMEM((SUBLANES, cdim), F32)] + pads,
        compiler_params=_cparams(("parallel", "arbitrary")),
        name="ssd",
    )(xbc, z, sm, smt4, st0, tail0, p['ssd_conv_w'], p['ssd_conv_b'], p['a_row'], p['a_col'], p['d_row'],
      p['ssd_norm_g'])


def _mem_kv_kernel(m_ref, g_ref, w_ref, gk_ref, k_ref, v_ref):
    mw = MEM_HEADS * MEM_DIM
    xn = _rms(m_ref[...], g_ref[...]).astype(BF16)
    kv = _dot(xn, w_ref[...])
    for h in range(MEM_HEADS):
        sl = slice(h * MEM_DIM, (h + 1) * MEM_DIM)
        k_ref[:, sl] = _rms(kv[:, sl], gk_ref[...])
    v_ref[...] = kv[:, mw:]


def _mem_kv(mem, p, tm):
    n, d = mem.shape
    mw = MEM_HEADS * MEM_DIM
    row = lambda i: (i, 0)
    return pl.pallas_call(
        _mem_kv_kernel,
        grid=(n // tm,),
        in_specs=[pl.BlockSpec((tm, d), row), _const_spec((1, d)), _const_spec((d, 2 * mw)),
                  _const_spec((1, MEM_DIM))],
        out_specs=[pl.BlockSpec((tm, mw), row), pl.BlockSpec((tm, mw), row)],
        out_shape=[jax.ShapeDtypeStruct((n, mw), F32), jax.ShapeDtypeStruct((n, mw), F32)],
        compiler_params=_cparams(("parallel",)),
        name="mem_kv",
    )(mem, p['mem_in_norm_g'], p['w_mem_kv'], p['mem_k_norm_g'])


def _mix_out(x_ref, fo_ref, so_ref, wout_ref, g_ref, wq_ref, gq_ref):
    fw = fo_ref.shape[-1]
    y1 = x_ref[...] + _dot(fo_ref[...].astype(BF16), wout_ref[0:fw, :]) + _dot(so_ref[...].astype(BF16), wout_ref[fw:, :])
    xn = _rms(y1, g_ref[...]).astype(BF16)
    q = _dot(xn, wq_ref[...])
    qs = []
    for h in range(MEM_HEADS):
        qh = _rms(q[:, h * MEM_DIM:(h + 1) * MEM_DIM], gq_ref[...]) * MEM_DIM ** -0.5
        qs.append(qh.astype(BF16))
    return y1, qs


def _memattn_prompt_kernel(x_ref, fo_ref, so_ref, wout_ref, g_ref, wq_ref, gq_ref, mk_ref, mv_ref, wo_ref, o_ref):
    y1, qs = _mix_out(x_ref, fo_ref, so_ref, wout_ref, g_ref, wq_ref, gq_ref)
    outs = []
    for h in range(MEM_HEADS):
        sl = slice(h * MEM_DIM, (h + 1) * MEM_DIM)
        s = _dot_nt(qs[h], mk_ref[:, sl].astype(BF16))
        p = jnp.exp(s - jnp.max(s, axis=1, keepdims=True))
        o = _dot(p.astype(BF16), mv_ref[:, sl].astype(BF16)) / jnp.sum(p, axis=1, keepdims=True)
        outs.append(o.astype(BF16))
    o_ref[...] = y1 + _dot(jnp.concatenate(outs, axis=1), wo_ref[...])


def _memattn_prompt(x, fo, so, mk, mv, p, tm, length, n_mem):
    n, d = x.shape
    fw = fo.shape[1]
    mw = MEM_HEADS * MEM_DIM
    per = length // tm
    row = lambda i: (i, 0)
    return pl.pallas_call(
        _memattn_prompt_kernel,
        grid=(n // tm,),
        in_specs=[pl.BlockSpec((tm, d), row), pl.BlockSpec((tm, fw), row), pl.BlockSpec((tm, fw), row),
                  _const_spec((2 * fw, d)), _const_spec((1, d)), _const_spec((d, mw)), _const_spec((1, MEM_DIM)),
                  pl.BlockSpec((n_mem, mw), lambda i: (i // per, 0)),
                  pl.BlockSpec((n_mem, mw), lambda i: (i // per, 0)),
                  _const_spec((mw, d))],
        out_specs=pl.BlockSpec((tm, d), row),
        out_shape=jax.ShapeDtypeStruct((n, d), F32),
        compiler_params=_cparams(("parallel",)),
        name="memattn_prompt",
    )(x, fo, so, p['w_out'], p['norm_mem_g'], p['w_mem_q'], p['mem_q_norm_g'], mk, mv, p['w_mem_o'])


def _memattn_sample_kernel(x_ref, fo_ref, so_ref, wout_ref, g_ref, wq_ref, gq_ref, mk_ref, mv_ref, wo_ref, o_ref,
                           *, t_new):
    y1, qs = _mix_out(x_ref, fo_ref, so_ref, wout_ref, g_ref, wq_ref, gq_ref)
    tm = y1.shape[0]
    per = SUBLANES // t_new
    mine = [_div_pow2(_iota((SUBLANES, 1), 0), t_new) == u for u in range(per)]
    tiles = []
    for r in range(tm // SUBLANES):
        outs = []
        for h in range(MEM_HEADS):
            qh = qs[h][r * SUBLANES:(r + 1) * SUBLANES, :]
            s = None
            for u in range(per):
                su = _dot_nt(qh, mk_ref[r * per + u, :, h, :].astype(BF16))
                s = su if s is None else jnp.where(mine[u], su, s)
            p = jnp.exp(s - jnp.max(s, axis=1, keepdims=True))
            pb = p.astype(BF16)
            o = None
            for u in range(per):
                ou = _dot(pb, mv_ref[r * per + u, :, h, :].astype(BF16))
                o = ou if o is None else jnp.where(mine[u], ou, o)
            outs.append((o / jnp.sum(p, axis=1, keepdims=True)).astype(BF16))
        tiles.append(jnp.concatenate(outs, axis=1))
    o_ref[...] = y1 + _dot(jnp.concatenate(tiles, axis=0), wo_ref[...])


def _memattn_sample(x, fo, so, mk, mv, p, tm, t_new):
    n, d = x.shape
    fw = fo.shape[1]
    mw = MEM_HEADS * MEM_DIM
    n_mem = mk.shape[1]
    bb = tm // t_new
    row = lambda i: (i, 0)
    return pl.pallas_call(
        functools.partial(_memattn_sample_kernel, t_new=t_new),
        grid=(n // tm,),
        in_specs=[pl.BlockSpec((tm, d), row), pl.BlockSpec((tm, fw), row), pl.BlockSpec((tm, fw), row),
                  _const_spec((2 * fw, d)), _const_spec((1, d)), _const_spec((d, mw)), _const_spec((1, MEM_DIM)),
                  pl.BlockSpec((bb, n_mem, MEM_HEADS, MEM_DIM), lambda i: (i, 0, 0, 0)),
                  pl.BlockSpec((bb, n_mem, MEM_HEADS, MEM_DIM), lambda i: (i, 0, 0, 0)),
                  _const_spec((mw, d))],
        out_specs=pl.BlockSpec((tm, d), row),
        out_shape=jax.ShapeDtypeStruct((n, d), F32),
        compiler_params=_cparams(("parallel",)),
        name="memattn_sample",
    )(x, fo, so, p['w_out'], p['norm_mem_g'], p['w_mem_q'], p['mem_q_norm_g'], mk, mv, p['w_mem_o'])


FF_CHUNK = 1024


def _ff_chunks(dff):
    return [(s, min(s + FF_CHUNK, dff)) for s in range(0, dff, FF_CHUNK)]


def _ffn_prompt_kernel(x_ref, g_ref, wg_ref, wu_ref, cw_ref, cb_ref, wd_ref, o_ref, tail_ref, halo, *, per):
    i = pl.program_id(0)
    tm = x_ref.shape[0]
    dff = wg_ref.shape[1]

    @pl.when(i % per == 0)
    def _():
        halo[...] = jnp.zeros_like(halo)

    x = x_ref[...]
    xn = _rms(x, g_ref[...]).astype(BF16)
    row8 = _iota((SUBLANES, 1), 0)
    acc = x
    for lo, hi in _ff_chunks(dff):
        gate = _dot(xn, wg_ref[:, lo:hi])
        up = _dot(xn, wu_ref[:, lo:hi])
        prev = halo[:, lo:hi]
        conv = cb_ref[:, lo:hi] + cw_ref[FFN_CONV - 1:FFN_CONV, lo:hi] * gate
        for sh in range(1, FFN_CONV):
            rg = pltpu.roll(gate, sh, 0)
            fix = jnp.where(row8 < sh, pltpu.roll(prev, sh, 0), rg[0:SUBLANES, :])
            conv = conv + cw_ref[FFN_CONV - 1 - sh:FFN_CONV - sh, lo:hi] * jnp.concatenate([fix, rg[SUBLANES:, :]], axis=0)
        halo[:, lo:hi] = gate[tm - SUBLANES:tm, :]
        hmid = (_silu(conv) * up).astype(BF16)
        acc = acc + _dot(hmid, wd_ref[lo:hi, :])
    o_ref[...] = acc
    tail_ref[...] = halo[...]


def _ffn_prompt(x, p, tm, batch, length):
    n, d = x.shape
    dff = p['w_ffn_gate'].shape[1]
    per = length // tm
    row = lambda i: (i, 0)
    return pl.pallas_call(
        functools.partial(_ffn_prompt_kernel, per=per),
        grid=(n // tm,),
        in_specs=[pl.BlockSpec((tm, d), row), _const_spec((1, d)),
                  pl.BlockSpec((d, dff), lambda i: (0, 0), pipeline_mode=pl.Buffered(1)),
                  pl.BlockSpec((d, dff), lambda i: (0, 0), pipeline_mode=pl.Buffered(1)),
                  _const_spec((FFN_CONV, dff)), _const_spec((1, dff)),
                  pl.BlockSpec((dff, d), lambda i: (0, 0), pipeline_mode=pl.Buffered(1))],
        out_specs=[pl.BlockSpec((tm, d), row),
                   pl.BlockSpec((None, SUBLANES, dff), lambda i: (i // per, 0, 0))],
        out_shape=[jax.ShapeDtypeStruct((n, d), F32),
                   jax.ShapeDtypeStruct((batch, SUBLANES, dff), F32)],
        scratch_shapes=[pltpu.VMEM((SUBLANES, dff), F32)],
        compiler_params=_cparams(("arbitrary",)),
        name="ffn_prompt",
    )(x, p['norm_ffn_g'], p['w_ffn_gate'], p['w_ffn_up'], p['ffn_conv_w'], p['ffn_conv_b'], p['w_ffn_down'])


def _ffn_sample_kernel(x_ref, buf_ref, g_ref, wg_ref, wu_ref, cw_ref, cb_ref, wd_ref, o_ref, nbuf_ref, *, t_new):
    nb = x_ref.shape[0] // t_new
    dff = wg_ref.shape[1]
    x = x_ref[...]
    xn = _rms(x, g_ref[...]).astype(BF16)
    acc = x
    for lo, hi in _ff_chunks(dff):
        gate = _dot(xn, wg_ref[:, lo:hi])
        up = _dot(xn, wu_ref[:, lo:hi])
        ext = jnp.concatenate([buf_ref[:, lo:hi], gate], axis=0)
        conv = cb_ref[:, lo:hi]
        for j in range(FFN_CONV):
            conv = conv + cw_ref[j:j + 1, lo:hi] * ext[j * nb:(j + t_new) * nb, :]
        nbuf_ref[:, lo:hi] = ext[t_new * nb:, :]
        hmid = (_silu(conv) * up).astype(BF16)
        acc = acc + _dot(hmid, wd_ref[lo:hi, :])
    o_ref[...] = acc


def _ffn_sample(x_tm, buf_tm, p, t_new):
    n, d = x_tm.shape
    dff = p['w_ffn_gate'].shape[1]
    nbr = buf_tm.shape[0]
    return pl.pallas_call(
        functools.partial(_ffn_sample_kernel, t_new=t_new),
        grid=(1,),
        in_specs=[_const_spec((n, d)), _const_spec((nbr, dff)), _const_spec((1, d)),
                  pl.BlockSpec((d, dff), lambda i: (0, 0), pipeline_mode=pl.Buffered(1)),
                  pl.BlockSpec((d, dff), lambda i: (0, 0), pipeline_mode=pl.Buffered(1)),
                  _const_spec((FFN_CONV, dff)), _const_spec((1, dff)),
                  pl.BlockSpec((dff, d), lambda i: (0, 0), pipeline_mode=pl.Buffered(1))],
        out_specs=[_const_spec((n, d)), _const_spec((nbr, dff))],
        out_shape=[jax.ShapeDtypeStruct((n, d), F32), jax.ShapeDtypeStruct((nbr, dff), F32)],
        compiler_params=_cparams(("arbitrary",)),
        name="ffn_sample",
    )(x_tm, buf_tm, p['norm_ffn_g'], p['w_ffn_gate'], p['w_ffn_up'], p['ffn_conv_w'], p['ffn_conv_b'],
      p['w_ffn_down'])


def _prep_layer(l, norm_mix_g, w_in, b_forget, fox_q_norm_g, fox_k_norm_g, ssd_conv_w, ssd_conv_b, ssd_dt_bias,
                ssd_a_log, ssd_d, ssd_norm_g, w_out, norm_mem_g, mem_in_norm_g, w_mem_q, w_mem_kv, mem_q_norm_g,
                mem_k_norm_g, w_mem_o, norm_ffn_g, w_ffn_gate, w_ffn_up, ffn_conv_w, ffn_conv_b, w_ffn_down):
    fw = FORGET_HEADS * FOX_DIM
    hw = SSD_HEADS * SSD_DIM
    w = w_in[l]
    o_f = 3 * fw
    o_z = o_f + FORGET_HEADS
    o_x = o_z + hw
    o_dt = w.shape[1] - SSD_HEADS
    w_small = jnp.concatenate([w[:, o_f:o_z], w[:, o_dt:]], axis=1)
    b_small = jnp.concatenate([b_forget[l], ssd_dt_bias[l]]).astype(F32)
    a_neg = -jnp.exp(ssd_a_log[l].astype(F32))
    a16 = jnp.concatenate([jnp.zeros((FORGET_HEADS,), F32), a_neg])
    head_of = jnp.arange(fw) // FOX_DIM
    onehot = (head_of[:, None] == jnp.arange(LANES)[None, :]).astype(F32)
    row = lambda v: v.reshape(1, -1).astype(F32)
    return {
        'norm_mix_g': row(norm_mix_g[l]),
        'w_q': w[:, :fw].astype(BF16),
        'w_kvt': w[:, fw:o_f].T.astype(BF16),
        'w_zx': w[:, o_z:o_dt].astype(BF16),
        'w_sm': jnp.pad(w_small, ((0, 0), (0, LANES - w_small.shape[1]))).astype(BF16),
        'w_smt': w_small.T.astype(BF16),
        'b_sm': jnp.pad(b_small, (0, LANES - b_small.shape[0])).reshape(1, LANES),
        'b_smt': b_small.reshape(-1, 1),
        'gq': row(jnp.tile(fox_q_norm_g[l], FORGET_HEADS)),
        'gk_col': jnp.tile(fox_k_norm_g[l], FORGET_HEADS).reshape(-1, 1).astype(F32),
        'head_reduce': (onehot / FOX_DIM).astype(BF16),
        'head_expand': onehot.T.astype(BF16),
        'ssd_conv_w': ssd_conv_w[l].astype(F32),
        'ssd_conv_b': row(ssd_conv_b[l]),
        'a_row': jnp.pad(a16, (0, LANES - a16.shape[0])).reshape(1, LANES),
        'a_col': a16.reshape(-1, 1),
        'd_row': row(jnp.repeat(ssd_d[l], SSD_DIM)),
        'ssd_norm_g': row(ssd_norm_g[l]),
        'w_out': w_out[l].astype(BF16),
        'norm_mem_g': row(norm_mem_g[l]),
        'mem_in_norm_g': row(mem_in_norm_g[l]),
        'w_mem_q': w_mem_q[l].astype(BF16),
        'w_mem_kv': w_mem_kv[l].astype(BF16),
        'mem_q_norm_g': row(mem_q_norm_g[l]),
        'mem_k_norm_g': row(mem_k_norm_g[l]),
        'w_mem_o': w_mem_o[l].astype(BF16),
        'norm_ffn_g': row(norm_ffn_g[l]),
        'w_ffn_gate': w_ffn_gate[l].astype(BF16),
        'w_ffn_up': w_ffn_up[l].astype(BF16),
        'ffn_conv_w': ffn_conv_w[l].astype(F32),
        'ffn_conv_b': row(ffn_conv_b[l]),
        'w_ffn_down': w_ffn_down[l].astype(BF16),
    }


def _row_tile(n, want):
    t = min(n, want)
    while n % t:
        t //= 2
    return t


def _layer(xp, mem, xs, kc, vc, lc, st_ssm, st_conv, st_ffn, mem_k, mem_v, page_table, p):
    fw = FORGET_HEADS * FOX_DIM
    hw = SSD_HEADS * SSD_DIM
    heads = lambda a, b_, t_: a.reshape(b_, FORGET_HEADS, FOX_DIM, t_).transpose(0, 3, 1, 2)

    bp, length, d = xp.shape
    n_p = bp * length
    n_mem = mem.shape[1]
    tm = _row_tile(length, 512)
    xpf = xp.reshape(n_p, d)
    q_p, kt_p, vt_p, ktb_p, vtb_p, z_p, xbc_p, sm_p, smt_p = _inproj(xpf, p, tm, bp, length)
    bs, t_new, _ = xs.shape
    n_s = bs * t_new
    xsf = xs.reshape(n_s, d)
    q_s, kt_s, vt_s, _, _, z_s, xbc_s, sm_s, smt_s = _inproj(xsf, p, _row_tile(n_s, 512), 1, n_s)

    ccol, crow = _cumsum(sm_p, smt_p, bp, length)
    tq = _row_tile(length, 512)
    nq = length // tq
    crow4 = crow.reshape(2 * SUBLANES, bp, nq, tq).transpose(1, 2, 0, 3)
    k_s = kt_s[0].T.reshape(bs, t_new, fw)
    v_s = vt_s[0].T.reshape(bs, t_new, fw)
    own = (jnp.arange(fw) // FOX_DIM)[None, :] == jnp.arange(FORGET_HEADS)[:, None]
    qblk = jnp.where(own[None, None], q_s.reshape(bs, t_new, 1, fw), jnp.zeros((), q_s.dtype))
    qblk = qblk.reshape(bs, t_new * FORGET_HEADS, fw)
    lft = jnp.pad(smt_s[:FORGET_HEADS].reshape(FORGET_HEADS, bs, t_new).transpose(1, 0, 2),
                  ((0, 0), (0, 0), (0, LANES - t_new)))
    fox_s, fox_p = _fox(page_table, qblk, k_s, v_s, lft, kc, vc, lc, q_p, ktb_p, vtb_p, ccol, crow4, tq)

    cdim = xbc_p.shape[1]
    nc = length // CHUNK
    smt4_p = smt_p.reshape(2 * SUBLANES, bp, nc, CHUNK).transpose(1, 2, 0, 3)
    xbc3_p = xbc_p.reshape(bp, length, cdim)
    ssd_p, ssm_fin = _ssd(xbc3_p, z_p.reshape(bp, length, hw), sm_p.reshape(bp, length, LANES), smt4_p,
                          jnp.zeros((bp, hw, SSD_STATE), F32), jnp.zeros((bp, SUBLANES, cdim), F32), p)
    xbc3_s = xbc_s.reshape(bs, t_new, cdim)
    smt4_s = smt_s.reshape(2 * SUBLANES, bs, 1, t_new).transpose(1, 2, 0, 3)
    tail0 = jnp.pad(st_conv, ((0, 0), (SUBLANES - (SSD_CONV - 1), 0), (0, 0)))
    ssd_s, ssm_new = _ssd(xbc3_s, z_s.reshape(bs, t_new, hw), sm_s.reshape(bs, t_new, LANES), smt4_s,
                          st_ssm.reshape(bs, hw, SSD_STATE), tail0, p)

    mk, mv = _mem_kv(mem.reshape(bp * n_mem, d), p, _row_tile(n_mem, 512))
    y2_p = _memattn_prompt(xpf, fox_p, ssd_p.reshape(n_p, hw), mk, mv, p, tm, length, n_mem)
    y2_s = _memattn_sample(xsf, fox_s.reshape(n_s, fw), ssd_s.reshape(n_s, hw), mem_k, mem_v, p,
                           _row_tile(n_s, 32), t_new)

    y3_p, ffn_tail = _ffn_prompt(y2_p, p, tm, bp, length)
    x_tm = y2_s.reshape(bs, t_new, d).transpose(1, 0, 2).reshape(n_s, d)
    dff = st_ffn.shape[-1]
    buf_tm = st_ffn.transpose(1, 0, 2).reshape((FFN_CONV - 1) * bs, dff)
    y3_tm, nbuf_tm = _ffn_sample(x_tm, buf_tm, p, t_new)

    prompt = (y3_p.reshape(bp, length, d),
              heads(kt_p, bp, length),
              heads(vt_p, bp, length),
              sm_p.reshape(bp, length, LANES)[:, :, :FORGET_HEADS],
              ssm_fin.reshape(bp, SSD_HEADS, SSD_DIM, SSD_STATE),
              xbc3_p[:, length - (SSD_CONV - 1):, :],
              ffn_tail[:, SUBLANES - (FFN_CONV - 1):, :],
              mk.reshape(bp, n_mem, MEM_HEADS, MEM_DIM),
              mv.reshape(bp, n_mem, MEM_HEADS, MEM_DIM))
    sample = (y3_tm.reshape(t_new, bs, d).transpose(1, 0, 2),
              k_s.reshape(bs, t_new, FORGET_HEADS, FOX_DIM),
              v_s.reshape(bs, t_new, FORGET_HEADS, FOX_DIM),
              sm_s.reshape(bs, t_new, LANES)[:, :, :FORGET_HEADS],
              ssm_new.reshape(bs, SSD_HEADS, SSD_DIM, SSD_STATE),
              jnp.concatenate([st_conv, xbc3_s], axis=1)[:, t_new:, :],
              nbuf_tm.reshape(FFN_CONV - 1, bs, dff).transpose(1, 0, 2))
    return prompt, sample


def kernel(x_prompt, mem_prompt, x_sample, cache_fox_k, cache_fox_v, cache_fox_logf, state_ssm, state_ssm_conv, state_ffn_conv, cache_mem_k, cache_mem_v, page_table, norm_mix_g, w_in, b_forget, fox_q_norm_g, fox_k_norm_g, ssd_conv_w, ssd_conv_b, ssd_dt_bias, ssd_a_log, ssd_d, ssd_norm_g, w_out, norm_mem_g, mem_in_norm_g, w_mem_q, w_mem_kv, mem_q_norm_g, mem_k_norm_g, w_mem_o, norm_ffn_g, w_ffn_gate, w_ffn_up, ffn_conv_w, ffn_conv_b, w_ffn_down):
    depth = w_in.shape[0]
    yp, ys = x_prompt, x_sample
    kc_all = jnp.transpose(cache_fox_k, (0, 1, 3, 4, 2))
    vc_all = jnp.transpose(cache_fox_v, (0, 1, 3, 4, 2))
    lc_all = jnp.transpose(cache_fox_logf, (0, 1, 3, 2))
    pouts, souts = [], []
    for l in range(depth):
        p = _prep_layer(l, norm_mix_g, w_in, b_forget, fox_q_norm_g, fox_k_norm_g, ssd_conv_w, ssd_conv_b,
                        ssd_dt_bias, ssd_a_log, ssd_d, ssd_norm_g, w_out, norm_mem_g, mem_in_norm_g, w_mem_q,
                        w_mem_kv, mem_q_norm_g, mem_k_norm_g, w_mem_o, norm_ffn_g, w_ffn_gate, w_ffn_up,
                        ffn_conv_w, ffn_conv_b, w_ffn_down)
        po, so = _layer(yp, mem_prompt, ys, kc_all[l], vc_all[l], lc_all[l], state_ssm[l], state_ssm_conv[l],
                        state_ffn_conv[l], cache_mem_k[l], cache_mem_v[l], page_table, p)
        yp, ys = po[0], so[0]
        pouts.append(po[1:])
        souts.append(so[1:])
    stack = lambda outs, i: jnp.stack([o[i] for o in outs])
    return (yp, ys) + tuple(stack(pouts, i) for i in range(8)) + tuple(stack(souts, i) for i in range(6))
```

```python
import functools

import numpy as np
import jax
import jax.numpy as jnp
from jax import lax
from jax.experimental import pallas as pl
from jax.experimental.pallas import tpu as pltpu

F32 = jnp.float32
BF16 = jnp.bfloat16
EPS = 1e-6
LOG2E = 1.4426950408889634
FORGET_HEADS = 8
FOX_DIM = 64
SSD_HEADS = 8
SSD_DIM = 64
SSD_STATE = 128
SSD_CONV = 4
CHUNK = 128
MEM_HEADS = 4
MEM_DIM = 128
FFN_CONV = 3
LANES = 128
SUBLANES = 8
VMEM_LIMIT = 56 * 1024 * 1024
NT = (((1,), (1,)), ((), ()))
TN = (((0,), (0,)), ((), ()))


def _dot(a, b):
    return jnp.dot(a, b, preferred_element_type=F32)


def _dot_nt(a, b):
    return lax.dot_general(a, b, NT, preferred_element_type=F32)


def _dot_tn(a, b):
    return lax.dot_general(a, b, TN, preferred_element_type=F32)


def _split3(x):
    hi = x.astype(BF16)
    r1 = x - hi.astype(F32)
    mid = r1.astype(BF16)
    lo = (r1 - mid.astype(F32)).astype(BF16)
    return hi, mid, lo


def _dot3_left(x, m):
    hi, mid, lo = _split3(x)
    return _dot(hi, m) + _dot(mid, m) + _dot(lo, m)


def _dot3_right(m, x):
    hi, mid, lo = _split3(x)
    return _dot(m, hi) + _dot(m, mid) + _dot(m, lo)


def _dot2_left(x, m):
    hi = x.astype(BF16)
    lo = (x - hi.astype(F32)).astype(BF16)
    return _dot(hi, m) + _dot(lo, m)


def _rms(x, g):
    return x * lax.rsqrt(jnp.mean(x * x, axis=-1, keepdims=True) + EPS) * g


def _softplus(x):
    return jnp.maximum(x, 0.0) + jnp.log1p(jnp.exp(-jnp.abs(x)))


def _log_sigmoid(x):
    return -_softplus(-x)


def _silu(x):
    return x * (1.0 / (1.0 + jnp.exp(-x)))


def _iota(shape, dim):
    return lax.broadcasted_iota(jnp.int32, shape, dim)


def _div_pow2(x, d):
    assert d & (d - 1) == 0
    return lax.shift_right_logical(x, jnp.int32(d.bit_length() - 1))


def _cparams(sem):
    return pltpu.CompilerParams(dimension_semantics=sem, vmem_limit_bytes=VMEM_LIMIT)


def _const_spec(shape):
    n = len(shape)
    return pl.BlockSpec(shape, lambda *_: (0,) * n)


def _inproj_kernel(x_ref, g_ref, wq_ref, wkvt_ref, wzx_ref, wsm_ref, wsmt_ref, gq_ref, gkc_ref, red_ref, exp_ref,
                   bsm_ref, bsmt_ref,
                   q_ref, kt_ref, vt_ref, ktb_ref, vtb_ref, z_ref, xbc_ref, sm_ref, smt_ref):
    fw = FORGET_HEADS * FOX_DIM
    tm = x_ref.shape[0]
    xn = _rms(x_ref[...], g_ref[...]).astype(BF16)

    q = _dot(xn, wq_ref[...])
    rs = lax.rsqrt(_dot2_left(q * q, red_ref[...]) + EPS)
    q_ref[...] = (q * _dot2_left(rs, exp_ref[...]) * gq_ref[...] * (FOX_DIM ** -0.5 * LOG2E)).astype(BF16)

    kv = _dot_nt(wkvt_ref[...], xn)
    k3 = kv[0:fw, :].reshape(FORGET_HEADS, FOX_DIM, tm)
    k3 = k3 * lax.rsqrt(jnp.mean(k3 * k3, axis=1, keepdims=True) + EPS)
    kn = k3.reshape(fw, tm) * gkc_ref[...]
    kt_ref[...] = kn
    ktb_ref[...] = kn.astype(BF16)
    vt = kv[fw:, :]
    vt_ref[...] = vt
    vtb_ref[...] = vt.astype(BF16)

    z_ref[...] = _dot(xn, wzx_ref[:, 0:fw])
    xbc_ref[...] = _dot(xn, wzx_ref[:, fw:])
    sm = _dot(xn, wsm_ref[...]) + bsm_ref[...]
    sm_ref[...] = jnp.where(_iota(sm.shape, 1) < FORGET_HEADS, _log_sigmoid(sm), _softplus(sm))
    smt = _dot_nt(wsmt_ref[...], xn) + bsmt_ref[...]
    smt_ref[...] = jnp.where(_iota(smt.shape, 0) < FORGET_HEADS, _log_sigmoid(smt), _softplus(smt))


def _inproj(x, p, tm, batch, length):
    n, d = x.shape
    fw = FORGET_HEADS * FOX_DIM
    nzx = p['w_zx'].shape[1]
    per = length // tm
    row = lambda i: (i, 0)
    tr = lambda i: (i // per, 0, i % per)
    return pl.pallas_call(
        _inproj_kernel,
        grid=(n // tm,),
        in_specs=[
            pl.BlockSpec((tm, d), row),
            _const_spec((1, d)),
            _const_spec((d, fw)),
            _const_spec((2 * fw, d)),
            _const_spec((d, nzx)),
            _const_spec((d, LANES)),
            _const_spec((2 * SUBLANES, d)),
            _const_spec((1, fw)),
            _const_spec((fw, 1)),
            _const_spec((fw, LANES)),
            _const_spec((LANES, fw)),
            _const_spec((1, LANES)),
            _const_spec((2 * SUBLANES, 1)),
        ],
        out_specs=[
            pl.BlockSpec((tm, fw), row),
            pl.BlockSpec((None, fw, tm), tr),
            pl.BlockSpec((None, fw, tm), tr),
            pl.BlockSpec((None, fw, tm), tr),
            pl.BlockSpec((None, fw, tm), tr),
            pl.BlockSpec((tm, fw), row),
            pl.BlockSpec((tm, nzx - fw), row),
            pl.BlockSpec((tm, LANES), row),
            pl.BlockSpec((2 * SUBLANES, tm), lambda i: (0, i)),
        ],
        out_shape=[
            jax.ShapeDtypeStruct((n, fw), BF16),
            jax.ShapeDtypeStruct((batch, fw, length), F32),
            jax.ShapeDtypeStruct((batch, fw, length), F32),
            jax.ShapeDtypeStruct((batch, fw, length), BF16),
            jax.ShapeDtypeStruct((batch, fw, length), BF16),
            jax.ShapeDtypeStruct((n, fw), F32),
            jax.ShapeDtypeStruct((n, nzx - fw), F32),
            jax.ShapeDtypeStruct((n, LANES), F32),
            jax.ShapeDtypeStruct((2 * SUBLANES, n), F32),
        ],
        compiler_params=_cparams(("parallel",)),
        name="inproj",
    )(x, p['norm_mix_g'], p['w_q'], p['w_kvt'], p['w_zx'], p['w_sm'], p['w_smt'], p['gq'], p['gk_col'],
      p['head_reduce'], p['head_expand'], p['b_sm'], p['b_smt'])


def _cumsum_kernel(sm_ref, smt_ref, col_ref, row_ref):
    length = sm_ref.shape[0]
    r = _iota((LANES, LANES), 0)
    c = _iota((LANES, LANES), 1)
    lower = (c <= r).astype(BF16)
    upper = (r <= c).astype(BF16)
    carry_c = jnp.zeros((1, LANES), F32)
    carry_r = jnp.zeros((smt_ref.shape[0], 1), F32)
    for b in range(length // LANES):
        sl = slice(b * LANES, (b + 1) * LANES)
        cs = _dot3_right(lower, sm_ref[sl, :]) + carry_c
        col_ref[sl, :] = cs * LOG2E
        carry_c = cs[LANES - 1:LANES, :]
        rs = _dot3_left(smt_ref[:, sl], upper) + carry_r
        row_ref[:, sl] = rs * LOG2E
        carry_r = rs[:, LANES - 1:LANES]


def _cumsum(sm, smt, batch, length):
    n = sm.shape[0]
    return pl.pallas_call(
        _cumsum_kernel,
        grid=(batch,),
        in_specs=[pl.BlockSpec((length, LANES), lambda b: (b, 0)),
                  pl.BlockSpec((2 * SUBLANES, length), lambda b: (0, b))],
        out_specs=[pl.BlockSpec((length, LANES), lambda b: (b, 0)),
                   pl.BlockSpec((2 * SUBLANES, length), lambda b: (0, b))],
        out_shape=[jax.ShapeDtypeStruct((n, LANES), F32),
                   jax.ShapeDtypeStruct((2 * SUBLANES, n), F32)],
        compiler_params=_cparams(("parallel",)),
        name="logf_cumsum",
    )(sm, smt)


PAGES_PER_CHUNK = 16
KV_SLOTS = 3


def _fox_kernel(pt_ref, pb_ref, ph_ref, pi_ref, pj_ref, pv_ref,
                qb_ref, kn_ref, vn_ref, lft_ref, later_ref, lpages_ref, kc_ref, vc_ref, lc_ref,
                q_ref, kt_ref, vt_ref, cc_ref, cr_ref,
                os_ref, op_ref,
                kbuf, vbuf, lbuf, suf, m_s, l_s, acc_s, cn_s, knp, vnp, pm, pacc, pqa, ksem, vsem, lsem,
                *, n_chunks, t_new):
    g = pl.program_id(0)
    n_steps = pl.num_programs(0)
    b = g // n_chunks
    c = g % n_chunks
    slot = g % KV_SLOTS
    ppc = PAGES_PER_CHUNK
    n_pages = n_chunks * ppc
    page = kbuf.shape[-1] // ppc
    heads = FORGET_HEADS
    fdim = heads * FOX_DIM

    def kv_copies(bb, cc, sl):
        cps = []
        for j in range(ppc):
            pg = pt_ref[bb, cc * ppc + j]
            cps.append(pltpu.make_async_copy(kc_ref.at[pg], kbuf.at[sl, :, :, pl.ds(j * page, page)], ksem.at[sl]))
            cps.append(pltpu.make_async_copy(vc_ref.at[pg], vbuf.at[sl, :, :, pl.ds(j * page, page)], vsem.at[sl]))
        return cps

    def lf_copies(bb, sl):
        return [pltpu.make_async_copy(lc_ref.at[pt_ref[bb, j]], lbuf.at[sl, pl.ds(j * heads, heads), :], lsem.at[sl])
                for j in range(n_pages)]

    @pl.when(g == 0)
    def _():
        for cp in lf_copies(0, 0):
            cp.start()
        for ahead in range(KV_SLOTS - 1):
            for cp in kv_copies(ahead // n_chunks, ahead % n_chunks, ahead):
                cp.start()
        pm[...] = jnp.zeros_like(pm)
        pacc[...] = jnp.zeros_like(pacc)
        pqa[...] = jnp.zeros_like(pqa)

    last_c = c == n_chunks - 1
    g_ahead = g + (KV_SLOTS - 1)

    @pl.when(g_ahead < n_steps)
    def _():
        for cp in kv_copies(g_ahead // n_chunks, g_ahead % n_chunks, g_ahead % KV_SLOTS):
            cp.start()

    bslot = b % 2

    @pl.when(jnp.logical_and(c == 0, g + n_chunks < n_steps))
    def _():
        for cp in lf_copies(b + 1, 1 - bslot):
            cp.start()

    @pl.when(c == 0)
    def _():
        pltpu.make_async_copy(lbuf.at[1 - bslot], lbuf.at[bslot], lsem.at[bslot]).wait()
        lf = lbuf[bslot]
        within = _dot3_left(lf, later_ref[...])
        tot = within[:, 0:1] + lf[:, 0:1]
        suf[...] = (within + _dot3_right(lpages_ref[...], jnp.broadcast_to(tot, lf.shape))) * LOG2E
        rl = _iota((LANES, LANES), 0)
        cl = _iota((LANES, LANES), 1)
        cn_s[...] = _dot3_left(lft_ref[...], (rl <= cl).astype(BF16)) * LOG2E
        m_s[...] = jnp.full(m_s.shape, -jnp.inf, F32)
        l_s[...] = jnp.zeros(l_s.shape, F32)
        acc_s[...] = jnp.zeros(acc_s.shape, F32)

    cn = cn_s[...]
    cn_col = jnp.concatenate([cn[:, t:t + 1] for t in range(t_new)], axis=0)
    qb = qb_ref[...]

    other = (g + 1) % KV_SLOTS
    pltpu.make_async_copy(kbuf.at[other], kbuf.at[slot], ksem.at[slot]).wait()
    pltpu.make_async_copy(vbuf.at[other], vbuf.at[slot], vsem.at[slot]).wait()

    def online(s, pv):
        m = m_s[...]
        m_new = jnp.maximum(m, jnp.max(s, axis=1, keepdims=True))
        alpha = jnp.exp2(m - m_new)
        p = jnp.exp2(s - m_new)
        l_s[...] = alpha * l_s[...] + jnp.sum(p, axis=1, keepdims=True)
        acc_s[...] = alpha * acc_s[...] + pv(p.astype(BF16))
        m_s[...] = m_new

    def sample_chunk():
        kt = kbuf[slot].reshape(fdim, ppc * page).astype(BF16)
        vt = vbuf[slot].reshape(fdim, ppc * page).astype(BF16)
        sfx = jnp.concatenate([suf[pl.ds((c * ppc + j) * heads, heads), :] for j in range(ppc)], axis=1)
        s = _dot(qb, kt) + jnp.concatenate([sfx] * t_new, axis=0) + cn_col
        online(s, lambda p: _dot_nt(p, vt))

    hp = ph_ref[g]
    pi = pi_ref[g]
    pj = pj_ref[g]
    valid = pv_ref[g] == 1
    diag = pj == pi
    tq = q_ref.shape[0]
    first = _iota((1, LANES), 1) < FOX_DIM

    def prompt_pair(masked):
        kt = kt_ref[...]
        vt = vt_ref[...]
        cr = cr_ref[...]
        tk = kt.shape[1]
        half = FOX_DIM
        fresh = pj == 0

        def terms(x):
            hi, mid, lo = _split3(x)
            return hi.astype(F32), mid.astype(F32), lo.astype(F32)

        @pl.when(fresh)
        def _():
            cc = cc_ref[...]
            qf = q_ref[...].astype(F32)
            lane = _iota((tq, LANES), 1)
            for hh in range(2):
                cq = jnp.sum(jnp.where(lane == 2 * hp + hh, cc, 0.0), axis=1, keepdims=True)
                hi, mid, lo = terms(cq)
                e0 = half * (1 - hh)
                ext = jnp.where(lane == e0, hi, jnp.where(lane == e0 + 1, mid, jnp.where(lane == e0 + 2, lo,
                      jnp.where(jnp.logical_and(lane >= e0 + 3, lane < e0 + 6), 1.0, 0.0))))
                own = (lane < half) if hh == 0 else (lane >= half)
                pqa[hh] = jnp.where(own, qf, ext).astype(BF16)

        sub = _iota(cr.shape, 0)
        r16 = _iota((2 * SUBLANES, tk), 0)
        pad = jnp.zeros((half - 2 * SUBLANES, tk), BF16)
        ones_row = jnp.where(r16 == 0, 1.0, 0.0).astype(BF16)
        wide = lambda a: jnp.concatenate([a] * (tk // LANES), axis=1)
        outs = []
        for hh in range(2):
            ck = jnp.sum(jnp.where(sub == 2 * hp + hh, cr, 0.0), axis=0, keepdims=True)
            hi, mid, lo = terms(ck)
            kext = jnp.where(r16 < 3, 1.0, jnp.where(r16 == 3, -hi, jnp.where(r16 == 4, -mid,
                   jnp.where(r16 == 5, -lo, 0.0)))).astype(BF16)
            if hh == 0:
                kta = jnp.concatenate([kt[0:half, :], kext, pad], axis=0)
                vta = jnp.concatenate([vt[0:half, :], ones_row, pad], axis=0)
            else:
                kta = jnp.concatenate([kext, pad, kt[half:, :]], axis=0)
                vta = jnp.concatenate([ones_row, pad, vt[half:, :]], axis=0)
            s = _dot(pqa[hh], kta)
            if masked:
                s = jnp.where(_iota(s.shape, 1) <= _iota(s.shape, 0), s, -jnp.inf)
            m = jnp.where(fresh, -jnp.inf, pm[hh])
            acc = jnp.where(fresh, 0.0, pacc[hh])
            m_new = jnp.maximum(m, jnp.max(s, axis=1, keepdims=True))
            p = jnp.exp2(s - wide(m_new)).astype(BF16)
            acc = jnp.exp2(m - m_new) * acc + _dot_nt(p, vta)
            if masked:
                e0 = half * (1 - hh)
                outs.append(acc / acc[:, e0:e0 + 1])
            else:
                pm[hh] = m_new
                pacc[hh] = acc
        if masked:
            op_ref[...] = jnp.where(first, outs[0], outs[1])

    sample_chunk()

    @pl.when(jnp.logical_and(valid, diag))
    def _():
        prompt_pair(True)

    @pl.when(jnp.logical_and(valid, jnp.logical_not(diag)))
    def _():
        prompt_pair(False)

    @pl.when(last_c)
    def _():
        knp[...] = jnp.zeros_like(knp)
        vnp[...] = jnp.zeros_like(vnp)
        knp[0:t_new, :] = kn_ref[...]
        vnp[0:t_new, :] = vn_ref[...]
        kn = knp[...].astype(BF16)
        vn = vnp[...].astype(BF16)
        s2 = _dot_nt(qb, kn) + cn_col - jnp.concatenate([cn] * t_new, axis=0)
        tok = _div_pow2(_iota(s2.shape, 0), heads)
        s2 = jnp.where(_iota(s2.shape, 1) <= tok, s2, -jnp.inf)
        online(s2, lambda p: _dot(p, vn))
        o = acc_s[...] / l_s[...]
        keep = _div_pow2(_iota((heads, fdim), 1), FOX_DIM) == _iota((heads, fdim), 0)
        os_ref[...] = jnp.concatenate(
            [jnp.sum(jnp.where(keep, o[t * heads:(t + 1) * heads, :], 0.0), axis=0, keepdims=True)
             for t in range(t_new)], axis=0)


def _pair_schedule(batch, n_hp, nq, n_steps, n_chunks):
    pairs = [(b, h, i, j, 1) for b in range(batch) for h in range(n_hp) for i in range(nq) for j in range(i + 1)]
    n_idle = n_steps - len(pairs)
    assert n_idle >= 0, "more prompt attention pairs than sample chunks"
    steps = np.arange(n_steps)
    chunk = steps % n_chunks
    seq = steps // n_chunks
    cost = np.where(chunk == 0, 0, np.where(chunk == n_chunks - 1, 1 + seq % 2, 3))
    idle = np.zeros(n_steps, bool)
    idle[np.argsort(cost, kind="stable")[:n_idle]] = True
    rows, k = [], 0
    for g in range(n_steps):
        if idle[g]:
            rows.append(pairs[min(k, len(pairs) - 1)][:4] + (0,))
        else:
            rows.append(pairs[k])
            k += 1
    return [jnp.asarray(np.array(col, np.int32)) for col in zip(*rows)]


def _fox(page_table, qblk, kn, vn, lft, kc, vc, lc, q, ktb, vtb, ccol, crow4, tq):
    bs, rows, fdim = qblk.shape
    t_new = rows // FORGET_HEADS
    n_pages = page_table.shape[1]
    page = kc.shape[-1]
    n_chunks = n_pages // PAGES_PER_CHUNK
    pc = PAGES_PER_CHUNK * page
    n_steps = bs * n_chunks
    bp, _, length = ktb.shape
    nq = length // tq
    n_hp = fdim // LANES
    sched = _pair_schedule(bp, n_hp, nq, n_steps, n_chunks)
    pos = np.arange(page)
    later = jnp.asarray(pos[:, None] > pos[None, :], BF16)
    rows_ph = np.arange(n_pages * FORGET_HEADS)
    lpages = jnp.asarray((rows_ph[None, :] % FORGET_HEADS == rows_ph[:, None] % FORGET_HEADS)
                         & (rows_ph[None, :] // FORGET_HEADS > rows_ph[:, None] // FORGET_HEADS), BF16)
    per_s = lambda g, *_: (g // n_chunks, 0, 0)
    grid_spec = pltpu.PrefetchScalarGridSpec(
        num_scalar_prefetch=6,
        grid=(n_steps,),
        in_specs=[
            pl.BlockSpec((None, rows, fdim), per_s),
            pl.BlockSpec((None, t_new, fdim), per_s),
            pl.BlockSpec((None, t_new, fdim), per_s),
            pl.BlockSpec((None, FORGET_HEADS, LANES), per_s),
            pl.BlockSpec(later.shape, lambda g, *_: (0, 0)),
            pl.BlockSpec(lpages.shape, lambda g, *_: (0, 0)),
            pl.BlockSpec(memory_space=pl.ANY),
            pl.BlockSpec(memory_space=pl.ANY),
            pl.BlockSpec(memory_space=pl.ANY),
            pl.BlockSpec((tq, LANES), lambda g, pt, pb, ph, pi, pj, pv: (pb[g] * nq + pi[g], ph[g])),
            pl.BlockSpec((None, LANES, tq), lambda g, pt, pb, ph, pi, pj, pv: (pb[g], ph[g], pj[g])),
            pl.BlockSpec((None, LANES, tq), lambda g, pt, pb, ph, pi, pj, pv: (pb[g], ph[g], pj[g])),
            pl.BlockSpec((tq, LANES), lambda g, pt, pb, ph, pi, pj, pv: (pb[g] * nq + pi[g], 0)),
            pl.BlockSpec((None, None, 2 * SUBLANES, tq), lambda g, pt, pb, ph, pi, pj, pv: (pb[g], pj[g], 0, 0)),
        ],
        out_specs=[
            pl.BlockSpec((None, t_new, fdim), per_s),
            pl.BlockSpec((tq, LANES), lambda g, pt, pb, ph, pi, pj, pv: (pb[g] * nq + pi[g], ph[g])),
        ],
        scratch_shapes=[
            pltpu.VMEM((KV_SLOTS, FORGET_HEADS, FOX_DIM, pc), F32),
            pltpu.VMEM((KV_SLOTS, FORGET_HEADS, FOX_DIM, pc), F32),
            pltpu.VMEM((2, n_pages * FORGET_HEADS, page), F32),
            pltpu.VMEM((n_pages * FORGET_HEADS, page), F32),
            pltpu.VMEM((rows, 1), F32),
            pltpu.VMEM((rows, 1), F32),
            pltpu.VMEM((rows, fdim), F32),
            pltpu.VMEM((FORGET_HEADS, LANES), F32),
            pltpu.VMEM((LANES, fdim), F32),
            pltpu.VMEM((LANES, fdim), F32),
            pltpu.VMEM((2, tq, LANES), F32),
            pltpu.VMEM((2, tq, LANES), F32),
            pltpu.VMEM((2, tq, LANES), BF16),
            pltpu.SemaphoreType.DMA((KV_SLOTS,)),
            pltpu.SemaphoreType.DMA((KV_SLOTS,)),
            pltpu.SemaphoreType.DMA((2,)),
        ],
    )
    return pl.pallas_call(
        functools.partial(_fox_kernel, n_chunks=n_chunks, t_new=t_new),
        grid_spec=grid_spec,
        out_shape=[jax.ShapeDtypeStruct((bs, t_new, fdim), F32),
                   jax.ShapeDtypeStruct((bp * length, fdim), F32)],
        compiler_params=_cparams(("arbitrary",)),
        name="fox",
    )(page_table, *sched, qblk, kn, vn, lft, later, lpages, kc, vc, lc, q, ktb, vtb, ccol, crow4)


def _ssd_kernel(xbc_ref, z_ref, sm_ref, smt_ref, st0_ref, tail0_ref, cw_ref, cb_ref, arow_ref, acol_ref,
                dsk_ref, ng_ref, y_ref, fin_ref, state, tail, *pads, t_real):
    q = CHUNK
    c = pl.program_id(1)
    hw = SSD_HEADS * SSD_DIM
    gs = SSD_STATE

    @pl.when(c == 0)
    def _():
        state[...] = st0_ref[...]
        tail[...] = tail0_ref[...]

    if t_real == q:
        x = xbc_ref[...]
        z = z_ref[...]
        sm = sm_ref[...]
        smt = smt_ref[...]
    else:
        xp, zp, smp, smtp = pads
        xp[...] = jnp.zeros_like(xp)
        zp[...] = jnp.zeros_like(zp)
        smp[...] = jnp.zeros_like(smp)
        smtp[...] = jnp.zeros_like(smtp)
        xp[0:t_real, :] = xbc_ref[...]
        zp[0:t_real, :] = z_ref[...]
        smp[0:t_real, :] = sm_ref[...]
        smtp[:, 0:t_real] = smt_ref[...]
        x, z, sm, smt = xp[...], zp[...], smp[...], smtp[...]

    prev = tail[...]
    row8 = _iota(prev.shape, 0)
    conv = cb_ref[...] + cw_ref[SSD_CONV - 1:SSD_CONV, :] * x
    for sh in range(1, SSD_CONV):
        rx = pltpu.roll(x, sh, 0)
        fix = jnp.where(row8 < sh, pltpu.roll(prev, sh, 0), rx[0:SUBLANES, :])
        xs_sh = jnp.concatenate([fix, rx[SUBLANES:, :]], axis=0)
        conv = conv + cw_ref[SSD_CONV - 1 - sh:SSD_CONV - sh, :] * xs_sh
    tail[...] = x[q - SUBLANES:q, :]
    xc = _silu(conv)
    xs = xc[:, 0:hw]

    r = _iota((q, q), 0)
    cidx = _iota((q, q), 1)
    tril = cidx <= r
    lower = tril.astype(BF16)
    upper = (r <= cidx).astype(BF16)
    acum_c = _dot3_right(lower, sm * arow_ref[...])
    acum_r = _dot3_left(smt * acol_ref[...], upper)
    lane = _iota((1, LANES), 1)
    first = lane < SSD_DIM
    rowp = _iota((LANES, 1), 0) < SSD_DIM

    ys = []
    for pr in range(SSD_HEADS // 2):
        g = pr // (SSD_HEADS // 4)
        bg = xc[:, hw + g * gs: hw + (g + 1) * gs].astype(BF16)
        cg = xc[:, hw + 2 * gs + g * gs: hw + 2 * gs + (g + 1) * gs].astype(BF16)
        gmat = _dot_nt(cg, bg)
        xpair = xs[:, pr * LANES:(pr + 1) * LANES]
        xpb = xpair.astype(BF16)
        yd, te, ea, cd = [], [], [], []
        for hh in range(2):
            h = FORGET_HEADS + 2 * pr + hh
            ac = acum_c[:, h:h + 1]
            ar = acum_r[h:h + 1, :]
            dtr = smt[h:h + 1, :]
            dtc = sm[:, h:h + 1]
            last = acum_c[q - 1:q, h:h + 1]
            decay = jnp.exp(jnp.where(tril, ac - ar, -jnp.inf))
            sc = gmat * decay * dtr
            yd.append(_dot(sc.astype(BF16), xpb))
            te.append(jnp.exp(last - ac) * dtc)
            ea.append(jnp.exp(ac))
            cd.append(jnp.exp(last))
        ydiag = jnp.where(first, yd[0], yd[1])
        xsc = (xpair * jnp.where(first, te[0], te[1])).astype(BF16)
        cstate = _dot_tn(xsc, bg)
        prev_st = state[pr * LANES:(pr + 1) * LANES, :]
        yoff = _dot_nt(cg, prev_st.astype(BF16)) * jnp.where(first, ea[0], ea[1])
        state[pr * LANES:(pr + 1) * LANES, :] = prev_st * jnp.where(rowp, cd[0], cd[1]) + cstate
        ys.append(ydiag + yoff + dsk_ref[:, pr * LANES:(pr + 1) * LANES] * xpair)
    y = jnp.concatenate(ys, axis=1) * _silu(z)
    y = _rms(y, ng_ref[...])
    y_ref[...] = y[0:t_real, :]

    @pl.when(c == pl.num_programs(1) - 1)
    def _():
        fin_ref[...] = state[...]


def _ssd(xbc, z, sm, smt4, st0, tail0, p):
    batch, length, cdim = xbc.shape
    hw = z.shape[2]
    t_real = min(length, CHUNK)
    nc = length // t_real
    blk = lambda b, c: (b, c, 0)
    per_b = lambda b, c: (b, 0, 0)
    pads = []
    if t_real != CHUNK:
        pads = [pltpu.VMEM((CHUNK, cdim), F32), pltpu.VMEM((CHUNK, hw), F32),
                pltpu.VMEM((CHUNK, LANES), F32), pltpu.VMEM((2 * SUBLANES, CHUNK), F32)]
    return pl.pallas_call(
        functools.partial(_ssd_kernel, t_real=t_real),
        grid=(batch, nc),
        in_specs=[
            pl.BlockSpec((None, t_real, cdim), blk),
            pl.BlockSpec((None, t_real, hw), blk),
            pl.BlockSpec((None, t_real, LANES), blk),
            pl.BlockSpec((None, None, 2 * SUBLANES, t_real), lambda b, c: (b, c, 0, 0)),
            pl.BlockSpec((None, hw, SSD_STATE), per_b),
            pl.BlockSpec((None, SUBLANES, cdim), per_b),
            _const_spec((SSD_CONV, cdim)),
            _const_spec((1, cdim)),
            _const_spec((1, LANES)),
            _const_spec((2 * SUBLANES, 1)),
            _const_spec((1, hw)),
            _const_spec((1, hw)),
        ],
        out_specs=[pl.BlockSpec((None, t_real, hw), blk),
                   pl.BlockSpec((None, hw, SSD_STATE), per_b)],
        out_shape=[jax.ShapeDtypeStruct((batch, length, hw), F32),
                   jax.ShapeDtypeStruct((batch, hw, SSD_STATE), F32)],
        scratch_shapes=[pltpu.VMEM((hw, SSD_STATE), F32), pltpu.VMEM((SUBLANES, cdim), F32)] + pads,
        compiler_params=_cparams(("parallel", "arbitrary")),
        name="ssd",
    )(xbc, z, sm, smt4, st0, tail0, p['ssd_conv_w'], p['ssd_conv_b'], p['a_row'], p['a_col'], p['d_row'],
      p['ssd_norm_g'])


def _mem_kv_kernel(m_ref, g_ref, w_ref, gk_ref, k_ref, v_ref):
    mw = MEM_HEADS * MEM_DIM
    xn = _rms(m_ref[...], g_ref[...]).astype(BF16)
    kv = _dot(xn, w_ref[...])
    for h in range(MEM_HEADS):
        sl = slice(h * MEM_DIM, (h + 1) * MEM_DIM)
        k_ref[:, sl] = _rms(kv[:, sl], gk_ref[...])
    v_ref[...] = kv[:, mw:]


def _mem_kv(mem, p, tm):
    n, d = mem.shape
    mw = MEM_HEADS * MEM_DIM
    row = lambda i: (i, 0)
    return pl.pallas_call(
        _mem_kv_kernel,
        grid=(n // tm,),
        in_specs=[pl.BlockSpec((tm, d), row), _const_spec((1, d)), _const_spec((d, 2 * mw)),
                  _const_spec((1, MEM_DIM))],
        out_specs=[pl.BlockSpec((tm, mw), row), pl.BlockSpec((tm, mw), row)],
        out_shape=[jax.ShapeDtypeStruct((n, mw), F32), jax.ShapeDtypeStruct((n, mw), F32)],
        compiler_params=_cparams(("parallel",)),
        name="mem_kv",
    )(mem, p['mem_in_norm_g'], p['w_mem_kv'], p['mem_k_norm_g'])


def _mix_out(x_ref, fo_ref, so_ref, wout_ref, g_ref, wq_ref, gq_ref):
    fw = fo_ref.shape[-1]
    y1 = x_ref[...] + _dot(fo_ref[...].astype(BF16), wout_ref[0:fw, :]) + _dot(so_ref[...].astype(BF16), wout_ref[fw:, :])
    xn = _rms(y1, g_ref[...]).astype(BF16)
    q = _dot(xn, wq_ref[...])
    qs = []
    for h in range(MEM_HEADS):
        qh = _rms(q[:, h * MEM_DIM:(h + 1) * MEM_DIM], gq_ref[...]) * MEM_DIM ** -0.5
        qs.append(qh)
    return y1, qs


def _memattn_prompt_kernel(x_ref, fo_ref, so_ref, wout_ref, g_ref, wq_ref, gq_ref, mk_ref, mv_ref, wo_ref, o_ref):
    y1, qs = _mix_out(x_ref, fo_ref, so_ref, wout_ref, g_ref, wq_ref, gq_ref)
    outs = []
    for h in range(MEM_HEADS):
        sl = slice(h * MEM_DIM, (h + 1) * MEM_DIM)
        s = _dot_nt(qs[h].astype(BF16), mk_ref[:, sl].astype(BF16))
        p = jnp.exp(s - jnp.max(s, axis=1, keepdims=True))
        o = _dot(p.astype(BF16), mv_ref[:, sl].astype(BF16)) / jnp.sum(p, axis=1, keepdims=True)
        outs.append(o.astype(BF16))
    o_ref[...] = y1 + _dot(jnp.concatenate(outs, axis=1), wo_ref[...])


def _memattn_prompt(x, fo, so, mk, mv, p, tm, length, n_mem):
    n, d = x.shape
    fw = fo.shape[1]
    mw = MEM_HEADS * MEM_DIM
    per = length // tm
    row = lambda i: (i, 0)
    return pl.pallas_call(
        _memattn_prompt_kernel,
        grid=(n // tm,),
        in_specs=[pl.BlockSpec((tm, d), row), pl.BlockSpec((tm, fw), row), pl.BlockSpec((tm, fw), row),
                  _const_spec((2 * fw, d)), _const_spec((1, d)), _const_spec((d, mw)), _const_spec((1, MEM_DIM)),
                  pl.BlockSpec((n_mem, mw), lambda i: (i // per, 0)),
                  pl.BlockSpec((n_mem, mw), lambda i: (i // per, 0)),
                  _const_spec((mw, d))],
        out_specs=pl.BlockSpec((tm, d), row),
        out_shape=jax.ShapeDtypeStruct((n, d), F32),
        compiler_params=_cparams(("parallel",)),
        name="memattn_prompt",
    )(x, fo, so, p['w_out'], p['norm_mem_g'], p['w_mem_q'], p['mem_q_norm_g'], mk, mv, p['w_mem_o'])


def _memattn_sample_kernel(x_ref, fo_ref, so_ref, wout_ref, g_ref, wq_ref, gq_ref, mk_ref, mv_ref, wo_ref, o_ref,
                           *, t_new):
    y1, qs = _mix_out(x_ref, fo_ref, so_ref, wout_ref, g_ref, wq_ref, gq_ref)
    tm = y1.shape[0]
    per = SUBLANES // t_new
    nrow = MEM_HEADS * SUBLANES
    ncol = mk_ref.shape[1]
    row = _iota((nrow, 1), 0)
    mine = [_div_pow2(lax.bitwise_and(row, jnp.int32(SUBLANES - 1)), t_new) == u for u in range(per)]
    same_head = (lax.bitwise_and(_iota((nrow, ncol), 1), jnp.int32(MEM_HEADS - 1))
                 == _div_pow2(_iota((nrow, ncol), 0), SUBLANES))
    tiles = []
    for r in range(tm // SUBLANES):
        qst = jnp.concatenate([qs[h][r * SUBLANES:(r + 1) * SUBLANES, :] for h in range(MEM_HEADS)], axis=0)
        qst = qst.astype(BF16)
        s = None
        for u in range(per):
            su = _dot_nt(qst, mk_ref[r * per + u].astype(BF16))
            s = su if s is None else jnp.where(mine[u], su, s)
        s = jnp.where(same_head, s, -jnp.inf)
        p = jnp.exp(s - jnp.max(s, axis=1, keepdims=True))
        pb = p.astype(BF16)
        o = None
        for u in range(per):
            ou = _dot(pb, mv_ref[r * per + u].astype(BF16))
            o = ou if o is None else jnp.where(mine[u], ou, o)
        o = o / jnp.sum(p, axis=1, keepdims=True)
        tiles.append(jnp.concatenate([o[h * SUBLANES:(h + 1) * SUBLANES, :] for h in range(MEM_HEADS)], axis=1))
    o_ref[...] = y1 + _dot(jnp.concatenate(tiles, axis=0).astype(BF16), wo_ref[...])


def _memattn_sample(x, fo, so, mk, mv, p, tm, t_new):
    n, d = x.shape
    fw = fo.shape[1]
    mw = MEM_HEADS * MEM_DIM
    n_flat = mk.shape[1]
    bb = tm // t_new
    row = lambda i: (i, 0)
    return pl.pallas_call(
        functools.partial(_memattn_sample_kernel, t_new=t_new),
        grid=(n // tm,),
        in_specs=[pl.BlockSpec((tm, d), row), pl.BlockSpec((tm, fw), row), pl.BlockSpec((tm, fw), row),
                  _const_spec((2 * fw, d)), _const_spec((1, d)), _const_spec((d, mw)), _const_spec((1, MEM_DIM)),
                  pl.BlockSpec((bb, n_flat, MEM_DIM), lambda i: (i, 0, 0)),
                  pl.BlockSpec((bb, n_flat, MEM_DIM), lambda i: (i, 0, 0)),
                  _const_spec((mw, d))],
        out_specs=pl.BlockSpec((tm, d), row),
        out_shape=jax.ShapeDtypeStruct((n, d), F32),
        compiler_params=_cparams(("parallel",)),
        name="memattn_sample",
    )(x, fo, so, p['w_out'], p['norm_mem_g'], p['w_mem_q'], p['mem_q_norm_g'], mk, mv, p['w_mem_o'])


FF_CHUNK = 1024


def _ff_chunks(dff):
    return [(s, min(s + FF_CHUNK, dff)) for s in range(0, dff, FF_CHUNK)]


def _ffn_prompt_kernel(x_ref, g_ref, wg_ref, wu_ref, cw_ref, cb_ref, wd_ref, o_ref, tail_ref, halo, *, per):
    i = pl.program_id(0)
    tm = x_ref.shape[0]
    dff = wg_ref.shape[1]

    @pl.when(i % per == 0)
    def _():
        halo[...] = jnp.zeros_like(halo)

    x = x_ref[...]
    xn = _rms(x, g_ref[...]).astype(BF16)
    row8 = _iota((SUBLANES, 1), 0)
    acc = x
    for lo, hi in _ff_chunks(dff):
        gate = _dot(xn, wg_ref[:, lo:hi])
        up = _dot(xn, wu_ref[:, lo:hi])
        prev = halo[:, lo:hi]
        conv = cb_ref[:, lo:hi] + cw_ref[FFN_CONV - 1:FFN_CONV, lo:hi] * gate
        for sh in range(1, FFN_CONV):
            rg = pltpu.roll(gate, sh, 0)
            fix = jnp.where(row8 < sh, pltpu.roll(prev, sh, 0), rg[0:SUBLANES, :])
            conv = conv + cw_ref[FFN_CONV - 1 - sh:FFN_CONV - sh, lo:hi] * jnp.concatenate([fix, rg[SUBLANES:, :]], axis=0)
        halo[:, lo:hi] = gate[tm - SUBLANES:tm, :]
        hmid = (_silu(conv) * up).astype(BF16)
        acc = acc + _dot(hmid, wd_ref[lo:hi, :])
    o_ref[...] = acc
    tail_ref[...] = halo[...]


def _ffn_prompt(x, p, tm, batch, length):
    n, d = x.shape
    dff = p['w_ffn_gate'].shape[1]
    per = length // tm
    row = lambda i: (i, 0)
    return pl.pallas_call(
        functools.partial(_ffn_prompt_kernel, per=per),
        grid=(n // tm,),
        in_specs=[pl.BlockSpec((tm, d), row), _const_spec((1, d)),
                  pl.BlockSpec((d, dff), lambda i: (0, 0), pipeline_mode=pl.Buffered(1)),
                  pl.BlockSpec((d, dff), lambda i: (0, 0), pipeline_mode=pl.Buffered(1)),
                  _const_spec((FFN_CONV, dff)), _const_spec((1, dff)),
                  pl.BlockSpec((dff, d), lambda i: (0, 0), pipeline_mode=pl.Buffered(1))],
        out_specs=[pl.BlockSpec((tm, d), row),
                   pl.BlockSpec((None, SUBLANES, dff), lambda i: (i // per, 0, 0))],
        out_shape=[jax.ShapeDtypeStruct((n, d), F32),
                   jax.ShapeDtypeStruct((batch, SUBLANES, dff), F32)],
        scratch_shapes=[pltpu.VMEM((SUBLANES, dff), F32)],
        compiler_params=_cparams(("arbitrary",)),
        name="ffn_prompt",
    )(x, p['norm_ffn_g'], p['w_ffn_gate'], p['w_ffn_up'], p['ffn_conv_w'], p['ffn_conv_b'], p['w_ffn_down'])


def _ffn_sample_kernel(x_ref, buf_ref, g_ref, wg_ref, wu_ref, cw_ref, cb_ref, wd_ref, o_ref, nbuf_ref, *, t_new):
    nb = x_ref.shape[0] // t_new
    dff = wg_ref.shape[1]
    x = x_ref[...]
    xn = _rms(x, g_ref[...]).astype(BF16)
    acc = x
    for lo, hi in _ff_chunks(dff):
        gate = _dot(xn, wg_ref[:, lo:hi])
        up = _dot(xn, wu_ref[:, lo:hi])
        ext = jnp.concatenate([buf_ref[:, lo:hi], gate], axis=0)
        conv = cb_ref[:, lo:hi]
        for j in range(FFN_CONV):
            conv = conv + cw_ref[j:j + 1, lo:hi] * ext[j * nb:(j + t_new) * nb, :]
        nbuf_ref[:, lo:hi] = ext[t_new * nb:, :]
        hmid = (_silu(conv) * up).astype(BF16)
        acc = acc + _dot(hmid, wd_ref[lo:hi, :])
    o_ref[...] = acc


def _ffn_sample(x_tm, buf_tm, p, t_new):
    n, d = x_tm.shape
    dff = p['w_ffn_gate'].shape[1]
    nbr = buf_tm.shape[0]
    return pl.pallas_call(
        functools.partial(_ffn_sample_kernel, t_new=t_new),
        grid=(1,),
        in_specs=[_const_spec((n, d)), _const_spec((nbr, dff)), _const_spec((1, d)),
                  pl.BlockSpec((d, dff), lambda i: (0, 0), pipeline_mode=pl.Buffered(1)),
                  pl.BlockSpec((d, dff), lambda i: (0, 0), pipeline_mode=pl.Buffered(1)),
                  _const_spec((FFN_CONV, dff)), _const_spec((1, dff)),
                  pl.BlockSpec((dff, d), lambda i: (0, 0), pipeline_mode=pl.Buffered(1))],
        out_specs=[_const_spec((n, d)), _const_spec((nbr, dff))],
        out_shape=[jax.ShapeDtypeStruct((n, d), F32), jax.ShapeDtypeStruct((nbr, dff), F32)],
        compiler_params=_cparams(("arbitrary",)),
        name="ffn_sample",
    )(x_tm, buf_tm, p['norm_ffn_g'], p['w_ffn_gate'], p['w_ffn_up'], p['ffn_conv_w'], p['ffn_conv_b'],
      p['w_ffn_down'])


def _prep_layer(l, norm_mix_g, w_in, b_forget, fox_q_norm_g, fox_k_norm_g, ssd_conv_w, ssd_conv_b, ssd_dt_bias,
                ssd_a_log, ssd_d, ssd_norm_g, w_out, norm_mem_g, mem_in_norm_g, w_mem_q, w_mem_kv, mem_q_norm_g,
                mem_k_norm_g, w_mem_o, norm_ffn_g, w_ffn_gate, w_ffn_up, ffn_conv_w, ffn_conv_b, w_ffn_down):
    fw = FORGET_HEADS * FOX_DIM
    hw = SSD_HEADS * SSD_DIM
    w = w_in[l]
    o_f = 3 * fw
    o_z = o_f + FORGET_HEADS
    o_x = o_z + hw
    o_dt = w.shape[1] - SSD_HEADS
    w_small = jnp.concatenate([w[:, o_f:o_z], w[:, o_dt:]], axis=1)
    b_small = jnp.concatenate([b_forget[l], ssd_dt_bias[l]]).astype(F32)
    a_neg = -jnp.exp(ssd_a_log[l].astype(F32))
    a16 = jnp.concatenate([jnp.zeros((FORGET_HEADS,), F32), a_neg])
    head_of = jnp.arange(fw) // FOX_DIM
    onehot = (head_of[:, None] == jnp.arange(LANES)[None, :]).astype(F32)
    row = lambda v: v.reshape(1, -1).astype(F32)
    return {
        'norm_mix_g': row(norm_mix_g[l]),
        'w_q': w[:, :fw].astype(BF16),
        'w_kvt': w[:, fw:o_f].T.astype(BF16),
        'w_zx': w[:, o_z:o_dt].astype(BF16),
        'w_sm': jnp.pad(w_small, ((0, 0), (0, LANES - w_small.shape[1]))).astype(BF16),
        'w_smt': w_small.T.astype(BF16),
        'b_sm': jnp.pad(b_small, (0, LANES - b_small.shape[0])).reshape(1, LANES),
        'b_smt': b_small.reshape(-1, 1),
        'gq': row(jnp.tile(fox_q_norm_g[l], FORGET_HEADS)),
        'gk_col': jnp.tile(fox_k_norm_g[l], FORGET_HEADS).reshape(-1, 1).astype(F32),
        'head_reduce': (onehot / FOX_DIM).astype(BF16),
        'head_expand': onehot.T.astype(BF16),
        'ssd_conv_w': ssd_conv_w[l].astype(F32),
        'ssd_conv_b': row(ssd_conv_b[l]),
        'a_row': jnp.pad(a16, (0, LANES - a16.shape[0])).reshape(1, LANES),
        'a_col': a16.reshape(-1, 1),
        'd_row': row(jnp.repeat(ssd_d[l], SSD_DIM)),
        'ssd_norm_g': row(ssd_norm_g[l]),
        'w_out': w_out[l].astype(BF16),
        'norm_mem_g': row(norm_mem_g[l]),
        'mem_in_norm_g': row(mem_in_norm_g[l]),
        'w_mem_q': w_mem_q[l].astype(BF16),
        'w_mem_kv': w_mem_kv[l].astype(BF16),
        'mem_q_norm_g': row(mem_q_norm_g[l]),
        'mem_k_norm_g': row(mem_k_norm_g[l]),
        'w_mem_o': w_mem_o[l].astype(BF16),
        'norm_ffn_g': row(norm_ffn_g[l]),
        'w_ffn_gate': w_ffn_gate[l].astype(BF16),
        'w_ffn_up': w_ffn_up[l].astype(BF16),
        'ffn_conv_w': ffn_conv_w[l].astype(F32),
        'ffn_conv_b': row(ffn_conv_b[l]),
        'w_ffn_down': w_ffn_down[l].astype(BF16),
    }


def _row_tile(n, want):
    t = min(n, want)
    while n % t:
        t //= 2
    return t


def _layer(xp, mem, xs, kc, vc, lc, st_ssm, st_conv, st_ffn, mem_k, mem_v, page_table, p):
    fw = FORGET_HEADS * FOX_DIM
    hw = SSD_HEADS * SSD_DIM
    heads = lambda a, b_, t_: a.reshape(b_, FORGET_HEADS, FOX_DIM, t_).transpose(0, 3, 1, 2)

    bp, length, d = xp.shape
    n_p = bp * length
    n_mem = mem.shape[1]
    tm = _row_tile(length, 512)
    xpf = xp.reshape(n_p, d)
    q_p, kt_p, vt_p, ktb_p, vtb_p, z_p, xbc_p, sm_p, smt_p = _inproj(xpf, p, tm, bp, length)
    bs, t_new, _ = xs.shape
    n_s = bs * t_new
    xsf = xs.reshape(n_s, d)
    q_s, kt_s, vt_s, _, _, z_s, xbc_s, sm_s, smt_s = _inproj(xsf, p, _row_tile(n_s, 512), 1, n_s)

    ccol, crow = _cumsum(sm_p, smt_p, bp, length)
    tq = _row_tile(length, 512)
    nq = length // tq
    crow4 = crow.reshape(2 * SUBLANES, bp, nq, tq).transpose(1, 2, 0, 3)
    k_s = kt_s[0].T.reshape(bs, t_new, fw)
    v_s = vt_s[0].T.reshape(bs, t_new, fw)
    own = (jnp.arange(fw) // FOX_DIM)[None, :] == jnp.arange(FORGET_HEADS)[:, None]
    qblk = jnp.where(own[None, None], q_s.reshape(bs, t_new, 1, fw), jnp.zeros((), q_s.dtype))
    qblk = qblk.reshape(bs, t_new * FORGET_HEADS, fw)
    lft = jnp.pad(smt_s[:FORGET_HEADS].reshape(FORGET_HEADS, bs, t_new).transpose(1, 0, 2),
                  ((0, 0), (0, 0), (0, LANES - t_new)))
    fox_s, fox_p = _fox(page_table, qblk, k_s, v_s, lft, kc, vc, lc, q_p, ktb_p, vtb_p, ccol, crow4, tq)

    cdim = xbc_p.shape[1]
    nc = length // CHUNK
    smt4_p = smt_p.reshape(2 * SUBLANES, bp, nc, CHUNK).transpose(1, 2, 0, 3)
    xbc3_p = xbc_p.reshape(bp, length, cdim)
    ssd_p, ssm_fin = _ssd(xbc3_p, z_p.reshape(bp, length, hw), sm_p.reshape(bp, length, LANES), smt4_p,
                          jnp.zeros((bp, hw, SSD_STATE), F32), jnp.zeros((bp, SUBLANES, cdim), F32), p)
    xbc3_s = xbc_s.reshape(bs, t_new, cdim)
    smt4_s = smt_s.reshape(2 * SUBLANES, bs, 1, t_new).transpose(1, 2, 0, 3)
    tail0 = jnp.pad(st_conv, ((0, 0), (SUBLANES - (SSD_CONV - 1), 0), (0, 0)))
    ssd_s, ssm_new = _ssd(xbc3_s, z_s.reshape(bs, t_new, hw), sm_s.reshape(bs, t_new, LANES), smt4_s,
                          st_ssm.reshape(bs, hw, SSD_STATE), tail0, p)

    mk, mv = _mem_kv(mem.reshape(bp * n_mem, d), p, _row_tile(n_mem, 512))
    y2_p = _memattn_prompt(xpf, fox_p, ssd_p.reshape(n_p, hw), mk, mv, p, tm, length, n_mem)
    flat = lambda a: a.reshape(a.shape[0], a.shape[1] * a.shape[2], a.shape[3])
    y2_s = _memattn_sample(xsf, fox_s.reshape(n_s, fw), ssd_s.reshape(n_s, hw), flat(mem_k), flat(mem_v), p,
                           _row_tile(n_s, 32), t_new)

    y3_p, ffn_tail = _ffn_prompt(y2_p, p, tm, bp, length)
    x_tm = y2_s.reshape(bs, t_new, d).transpose(1, 0, 2).reshape(n_s, d)
    dff = st_ffn.shape[-1]
    buf_tm = st_ffn.transpose(1, 0, 2).reshape((FFN_CONV - 1) * bs, dff)
    y3_tm, nbuf_tm = _ffn_sample(x_tm, buf_tm, p, t_new)

    prompt = (y3_p.reshape(bp, length, d),
              heads(kt_p, bp, length),
              heads(vt_p, bp, length),
              sm_p.reshape(bp, length, LANES)[:, :, :FORGET_HEADS],
              ssm_fin.reshape(bp, SSD_HEADS, SSD_DIM, SSD_STATE),
              xbc3_p[:, length - (SSD_CONV - 1):, :],
              ffn_tail[:, SUBLANES - (FFN_CONV - 1):, :],
              mk.reshape(bp, n_mem, MEM_HEADS, MEM_DIM),
              mv.reshape(bp, n_mem, MEM_HEADS, MEM_DIM))
    sample = (y3_tm.reshape(t_new, bs, d).transpose(1, 0, 2),
              k_s.reshape(bs, t_new, FORGET_HEADS, FOX_DIM),
              v_s.reshape(bs, t_new, FORGET_HEADS, FOX_DIM),
              sm_s.reshape(bs, t_new, LANES)[:, :, :FORGET_HEADS],
              ssm_new.reshape(bs, SSD_HEADS, SSD_DIM, SSD_STATE),
              jnp.concatenate([st_conv, xbc3_s], axis=1)[:, t_new:, :],
              nbuf_tm.reshape(FFN_CONV - 1, bs, dff).transpose(1, 0, 2))
    return prompt, sample


def kernel(x_prompt, mem_prompt, x_sample, cache_fox_k, cache_fox_v, cache_fox_logf, state_ssm, state_ssm_conv, state_ffn_conv, cache_mem_k, cache_mem_v, page_table, norm_mix_g, w_in, b_forget, fox_q_norm_g, fox_k_norm_g, ssd_conv_w, ssd_conv_b, ssd_dt_bias, ssd_a_log, ssd_d, ssd_norm_g, w_out, norm_mem_g, mem_in_norm_g, w_mem_q, w_mem_kv, mem_q_norm_g, mem_k_norm_g, w_mem_o, norm_ffn_g, w_ffn_gate, w_ffn_up, ffn_conv_w, ffn_conv_b, w_ffn_down):
    depth = w_in.shape[0]
    yp, ys = x_prompt, x_sample
    kc_all = jnp.transpose(cache_fox_k, (0, 1, 3, 4, 2))
    vc_all = jnp.transpose(cache_fox_v, (0, 1, 3, 4, 2))
    lc_all = jnp.transpose(cache_fox_logf, (0, 1, 3, 2))
    pouts, souts = [], []
    for l in range(depth):
        p = _prep_layer(l, norm_mix_g, w_in, b_forget, fox_q_norm_g, fox_k_norm_g, ssd_conv_w, ssd_conv_b,
                        ssd_dt_bias, ssd_a_log, ssd_d, ssd_norm_g, w_out, norm_mem_g, mem_in_norm_g, w_mem_q,
                        w_mem_kv, mem_q_norm_g, mem_k_norm_g, w_mem_o, norm_ffn_g, w_ffn_gate, w_ffn_up,
                        ffn_conv_w, ffn_conv_b, w_ffn_down)
        po, so = _layer(yp, mem_prompt, ys, kc_all[l], vc_all[l], lc_all[l], state_ssm[l], state_ssm_conv[l],
                        state_ffn_conv[l], cache_mem_k[l], cache_mem_v[l], page_table, p)
        yp, ys = po[0], so[0]
        pouts.append(po[1:])
        souts.append(so[1:])
    stack = lambda outs, i: jnp.stack([o[i] for o in outs])
    return (yp, ys) + tuple(stack(pouts, i) for i in range(8)) + tuple(stack(souts, i) for i in range(6))
```

```python
import functools

import numpy as np
import jax
import jax.numpy as jnp
from jax import lax
from jax.experimental import pallas as pl
from jax.experimental.pallas import tpu as pltpu

F32 = jnp.float32
BF16 = jnp.bfloat16
EPS = 1e-6
LOG2E = 1.4426950408889634
FORGET_HEADS = 8
FOX_DIM = 64
SSD_HEADS = 8
SSD_DIM = 64
SSD_STATE = 128
SSD_CONV = 4
CHUNK = 128
MEM_HEADS = 4
MEM_DIM = 128
FFN_CONV = 3
LANES = 128
SUBLANES = 8
VMEM_LIMIT = 56 * 1024 * 1024
NT = (((1,), (1,)), ((), ()))
TN = (((0,), (0,)), ((), ()))


def _dot(a, b):
    return jnp.dot(a, b, preferred_element_type=F32)


def _dot_nt(a, b):
    return lax.dot_general(a, b, NT, preferred_element_type=F32)


def _dot_tn(a, b):
    return lax.dot_general(a, b, TN, preferred_element_type=F32)


def _split3(x):
    hi = x.astype(BF16)
    r1 = x - hi.astype(F32)
    mid = r1.astype(BF16)
    lo = (r1 - mid.astype(F32)).astype(BF16)
    return hi, mid, lo


def _dot3_left(x, m):
    hi, mid, lo = _split3(x)
    return _dot(hi, m) + _dot(mid, m) + _dot(lo, m)


def _dot3_right(m, x):
    hi, mid, lo = _split3(x)
    return _dot(m, hi) + _dot(m, mid) + _dot(m, lo)


def _dot2_left(x, m):
    hi = x.astype(BF16)
    lo = (x - hi.astype(F32)).astype(BF16)
    return _dot(hi, m) + _dot(lo, m)


def _rms(x, g):
    return x * lax.rsqrt(jnp.mean(x * x, axis=-1, keepdims=True) + EPS) * g


def _softplus(x):
    return jnp.maximum(x, 0.0) + jnp.log1p(jnp.exp(-jnp.abs(x)))


def _log_sigmoid(x):
    return -_softplus(-x)


def _silu(x):
    return x * (1.0 / (1.0 + jnp.exp(-x)))


def _iota(shape, dim):
    return lax.broadcasted_iota(jnp.int32, shape, dim)


def _div_pow2(x, d):
    assert d & (d - 1) == 0
    return lax.shift_right_logical(x, jnp.int32(d.bit_length() - 1))


def _cparams(sem):
    return pltpu.CompilerParams(dimension_semantics=sem, vmem_limit_bytes=VMEM_LIMIT)


def _const_spec(shape):
    n = len(shape)
    return pl.BlockSpec(shape, lambda *_: (0,) * n)


def _inproj_kernel(x_ref, g_ref, wq_ref, wkvt_ref, wzx_ref, wsm_ref, wsmt_ref, gq_ref, gkc_ref, red_ref, exp_ref,
                   bsm_ref, bsmt_ref,
                   q_ref, kt_ref, vt_ref, ktb_ref, vtb_ref, z_ref, xbc_ref, sm_ref, smt_ref):
    fw = FORGET_HEADS * FOX_DIM
    tm = x_ref.shape[0]
    xn = _rms(x_ref[...], g_ref[...]).astype(BF16)

    q = _dot(xn, wq_ref[...])
    rs = lax.rsqrt(_dot2_left(q * q, red_ref[...]) + EPS)
    q_ref[...] = (q * _dot2_left(rs, exp_ref[...]) * gq_ref[...] * (FOX_DIM ** -0.5 * LOG2E)).astype(BF16)

    kv = _dot_nt(wkvt_ref[...], xn)
    k3 = kv[0:fw, :].reshape(FORGET_HEADS, FOX_DIM, tm)
    k3 = k3 * lax.rsqrt(jnp.mean(k3 * k3, axis=1, keepdims=True) + EPS)
    kn = k3.reshape(fw, tm) * gkc_ref[...]
    kt_ref[...] = kn
    ktb_ref[...] = kn.astype(BF16)
    vt = kv[fw:, :]
    vt_ref[...] = vt
    vtb_ref[...] = vt.astype(BF16)

    z_ref[...] = _dot(xn, wzx_ref[:, 0:fw])
    xbc_ref[...] = _dot(xn, wzx_ref[:, fw:])
    sm = _dot(xn, wsm_ref[...]) + bsm_ref[...]
    sm_ref[...] = jnp.where(_iota(sm.shape, 1) < FORGET_HEADS, _log_sigmoid(sm), _softplus(sm))
    smt = _dot_nt(wsmt_ref[...], xn) + bsmt_ref[...]
    smt_ref[...] = jnp.where(_iota(smt.shape, 0) < FORGET_HEADS, _log_sigmoid(smt), _softplus(smt))


def _inproj(x, p, tm, batch, length):
    n, d = x.shape
    fw = FORGET_HEADS * FOX_DIM
    nzx = p['w_zx'].shape[1]
    per = length // tm
    row = lambda i: (i, 0)
    tr = lambda i: (i // per, 0, i % per)
    return pl.pallas_call(
        _inproj_kernel,
        grid=(n // tm,),
        in_specs=[
            pl.BlockSpec((tm, d), row),
            _const_spec((1, d)),
            _const_spec((d, fw)),
            _const_spec((2 * fw, d)),
            _const_spec((d, nzx)),
            _const_spec((d, LANES)),
            _const_spec((2 * SUBLANES, d)),
            _const_spec((1, fw)),
            _const_spec((fw, 1)),
            _const_spec((fw, LANES)),
            _const_spec((LANES, fw)),
            _const_spec((1, LANES)),
            _const_spec((2 * SUBLANES, 1)),
        ],
        out_specs=[
            pl.BlockSpec((tm, fw), row),
            pl.BlockSpec((None, fw, tm), tr),
            pl.BlockSpec((None, fw, tm), tr),
            pl.BlockSpec((None, fw, tm), tr),
            pl.BlockSpec((None, fw, tm), tr),
            pl.BlockSpec((tm, fw), row),
            pl.BlockSpec((tm, nzx - fw), row),
            pl.BlockSpec((tm, LANES), row),
            pl.BlockSpec((2 * SUBLANES, tm), lambda i: (0, i)),
        ],
        out_shape=[
            jax.ShapeDtypeStruct((n, fw), BF16),
            jax.ShapeDtypeStruct((batch, fw, length), F32),
            jax.ShapeDtypeStruct((batch, fw, length), F32),
            jax.ShapeDtypeStruct((batch, fw, length), BF16),
            jax.ShapeDtypeStruct((batch, fw, length), BF16),
            jax.ShapeDtypeStruct((n, fw), F32),
            jax.ShapeDtypeStruct((n, nzx - fw), F32),
            jax.ShapeDtypeStruct((n, LANES), F32),
            jax.ShapeDtypeStruct((2 * SUBLANES, n), F32),
        ],
        compiler_params=_cparams(("parallel",)),
        name="inproj",
    )(x, p['norm_mix_g'], p['w_q'], p['w_kvt'], p['w_zx'], p['w_sm'], p['w_smt'], p['gq'], p['gk_col'],
      p['head_reduce'], p['head_expand'], p['b_sm'], p['b_smt'])


def _cumsum_kernel(sm_ref, smt_ref, col_ref, row_ref):
    length = sm_ref.shape[0]
    r = _iota((LANES, LANES), 0)
    c = _iota((LANES, LANES), 1)
    lower = (c <= r).astype(BF16)
    upper = (r <= c).astype(BF16)
    carry_c = jnp.zeros((1, LANES), F32)
    carry_r = jnp.zeros((smt_ref.shape[0], 1), F32)
    for b in range(length // LANES):
        sl = slice(b * LANES, (b + 1) * LANES)
        cs = _dot3_right(lower, sm_ref[sl, :]) + carry_c
        col_ref[sl, :] = cs * LOG2E
        carry_c = cs[LANES - 1:LANES, :]
        rs = _dot3_left(smt_ref[:, sl], upper) + carry_r
        row_ref[:, sl] = rs * LOG2E
        carry_r = rs[:, LANES - 1:LANES]


def _cumsum(sm, smt, batch, length):
    n = sm.shape[0]
    return pl.pallas_call(
        _cumsum_kernel,
        grid=(batch,),
        in_specs=[pl.BlockSpec((length, LANES), lambda b: (b, 0)),
                  pl.BlockSpec((2 * SUBLANES, length), lambda b: (0, b))],
        out_specs=[pl.BlockSpec((length, LANES), lambda b: (b, 0)),
                   pl.BlockSpec((2 * SUBLANES, length), lambda b: (0, b))],
        out_shape=[jax.ShapeDtypeStruct((n, LANES), F32),
                   jax.ShapeDtypeStruct((2 * SUBLANES, n), F32)],
        compiler_params=_cparams(("parallel",)),
        name="logf_cumsum",
    )(sm, smt)


PAGES_PER_CHUNK = 16
KV_SLOTS = 3
STREAMS = 2


def _fox_kernel(pt_ref, pb_ref, ph_ref, pi_ref, pj_ref, pv_ref,
                qb_ref, kn_ref, vn_ref, lft_ref, later_ref, lpages_ref, kc_ref, vc_ref, lc_ref,
                q_ref, kt_ref, vt_ref, cc_ref, cr_ref,
                sx_ref, sz_ref, ssm_ref, ssmt_ref, sst_ref, stail_ref,
                cw_ref, cb_ref, arow_ref, acol_ref, dsk_ref, ng_ref,
                os_ref, op_ref, sy_ref, sfin_ref,
                kbuf, vbuf, lbuf, suf, m_s, l_s, acc_s, cn_s, knp, vnp, pm, pacc, pqa, xp, zp, smp, smtp,
                ksem, vsem, lsem,
                *, n_chunks, t_new):
    g = pl.program_id(0)
    n_steps = pl.num_programs(0)
    b = g // n_chunks
    c = g % n_chunks
    slot = g % KV_SLOTS
    ppc = PAGES_PER_CHUNK
    n_pages = n_chunks * ppc
    page = kbuf.shape[-1] // ppc
    heads = FORGET_HEADS
    fdim = heads * FOX_DIM

    def kv_copies(bb, cc, sl):
        cps = []
        for j in range(ppc):
            pg = pt_ref[bb, cc * ppc + j]
            cps.append(pltpu.make_async_copy(kc_ref.at[pg], kbuf.at[sl, :, :, pl.ds(j * page, page)], ksem.at[sl]))
            cps.append(pltpu.make_async_copy(vc_ref.at[pg], vbuf.at[sl, :, :, pl.ds(j * page, page)], vsem.at[sl]))
        return cps

    def lf_copies(bb, sl):
        return [pltpu.make_async_copy(lc_ref.at[pt_ref[bb, j]], lbuf.at[sl, pl.ds(j * heads, heads), :], lsem.at[sl])
                for j in range(n_pages)]

    @pl.when(g == 0)
    def _():
        for cp in lf_copies(0, 0):
            cp.start()
        for ahead in range(KV_SLOTS - 1):
            for cp in kv_copies(ahead // n_chunks, ahead % n_chunks, ahead):
                cp.start()
        pm[...] = jnp.zeros_like(pm)
        pacc[...] = jnp.zeros_like(pacc)
        pqa[...] = jnp.zeros_like(pqa)

    last_c = c == n_chunks - 1
    g_ahead = g + (KV_SLOTS - 1)

    @pl.when(g_ahead < n_steps)
    def _():
        for cp in kv_copies(g_ahead // n_chunks, g_ahead % n_chunks, g_ahead % KV_SLOTS):
            cp.start()

    bslot = b % 2

    @pl.when(jnp.logical_and(c == 0, g + n_chunks < n_steps))
    def _():
        for cp in lf_copies(b + 1, 1 - bslot):
            cp.start()

    @pl.when(c == 0)
    def _():
        pltpu.make_async_copy(lbuf.at[1 - bslot], lbuf.at[bslot], lsem.at[bslot]).wait()
        lf = lbuf[bslot]
        within = _dot3_left(lf, later_ref[...])
        tot = within[:, 0:1] + lf[:, 0:1]
        suf[...] = (within + _dot3_right(lpages_ref[...], jnp.broadcast_to(tot, lf.shape))) * LOG2E
        rl = _iota((LANES, LANES), 0)
        cl = _iota((LANES, LANES), 1)
        cn_s[...] = _dot3_left(lft_ref[...], (rl <= cl).astype(BF16)) * LOG2E
        m_s[...] = jnp.full(m_s.shape, -jnp.inf, F32)
        l_s[...] = jnp.zeros(l_s.shape, F32)
        acc_s[...] = jnp.zeros(acc_s.shape, F32)

    cn = cn_s[...]
    cn_col = jnp.concatenate([cn[:, t:t + 1] for t in range(t_new)], axis=0)
    qb = qb_ref[...]

    other = (g + 1) % KV_SLOTS
    pltpu.make_async_copy(kbuf.at[other], kbuf.at[slot], ksem.at[slot]).wait()
    pltpu.make_async_copy(vbuf.at[other], vbuf.at[slot], vsem.at[slot]).wait()

    def online(st, s, pv):
        m = m_s[st]
        m_new = jnp.maximum(m, jnp.max(s, axis=1, keepdims=True))
        alpha = jnp.exp2(m - m_new)
        p = jnp.exp2(s - m_new)
        l_s[st] = alpha * l_s[st] + jnp.sum(p, axis=1, keepdims=True)
        acc_s[st] = alpha * acc_s[st] + pv(p.astype(BF16))
        m_s[st] = m_new

    def sample_chunk():
        hpc = ppc // STREAMS
        wdt = hpc * page
        scores = []
        for st in range(STREAMS):
            kt = kbuf[slot, :, :, pl.ds(st * wdt, wdt)].reshape(fdim, wdt).astype(BF16)
            sfx = jnp.concatenate([suf[pl.ds((c * ppc + st * hpc + j) * heads, heads), :] for j in range(hpc)],
                                  axis=1)
            scores.append(_dot(qb, kt) + jnp.concatenate([sfx] * t_new, axis=0) + cn_col)
        for st in range(STREAMS):
            vt = vbuf[slot, :, :, pl.ds(st * wdt, wdt)].reshape(fdim, wdt).astype(BF16)
            online(st, scores[st], lambda p: _dot_nt(p, vt))

    hp = ph_ref[g]
    pi = pi_ref[g]
    pj = pj_ref[g]
    valid = pv_ref[g] == 1
    diag = pj == pi
    tq = q_ref.shape[0]
    first = _iota((1, LANES), 1) < FOX_DIM

    def prompt_pair(masked):
        kt = kt_ref[...]
        vt = vt_ref[...]
        cr = cr_ref[...]
        tk = kt.shape[1]
        half = FOX_DIM
        fresh = pj == 0

        def terms(x):
            hi, mid, lo = _split3(x)
            return hi.astype(F32), mid.astype(F32), lo.astype(F32)

        @pl.when(fresh)
        def _():
            cc = cc_ref[...]
            qf = q_ref[...].astype(F32)
            lane = _iota((tq, LANES), 1)
            for hh in range(2):
                cq = jnp.sum(jnp.where(lane == 2 * hp + hh, cc, 0.0), axis=1, keepdims=True)
                hi, mid, lo = terms(cq)
                e0 = half * (1 - hh)
                ext = jnp.where(lane == e0, hi, jnp.where(lane == e0 + 1, mid, jnp.where(lane == e0 + 2, lo,
                      jnp.where(jnp.logical_and(lane >= e0 + 3, lane < e0 + 6), 1.0, 0.0))))
                own = (lane < half) if hh == 0 else (lane >= half)
                pqa[hh] = jnp.where(own, qf, ext).astype(BF16)

        sub = _iota(cr.shape, 0)
        r16 = _iota((2 * SUBLANES, tk), 0)
        pad = jnp.zeros((half - 2 * SUBLANES, tk), BF16)
        ones_row = jnp.where(r16 == 0, 1.0, 0.0).astype(BF16)
        wide = lambda a: jnp.concatenate([a] * (tk // LANES), axis=1)
        outs = []
        for hh in range(2):
            ck = jnp.sum(jnp.where(sub == 2 * hp + hh, cr, 0.0), axis=0, keepdims=True)
            hi, mid, lo = terms(ck)
            kext = jnp.where(r16 < 3, 1.0, jnp.where(r16 == 3, -hi, jnp.where(r16 == 4, -mid,
                   jnp.where(r16 == 5, -lo, 0.0)))).astype(BF16)
            if hh == 0:
                kta = jnp.concatenate([kt[0:half, :], kext, pad], axis=0)
                vta = jnp.concatenate([vt[0:half, :], ones_row, pad], axis=0)
            else:
                kta = jnp.concatenate([kext, pad, kt[half:, :]], axis=0)
                vta = jnp.concatenate([ones_row, pad, vt[half:, :]], axis=0)
            s = _dot(pqa[hh], kta)
            if masked:
                s = jnp.where(_iota(s.shape, 1) <= _iota(s.shape, 0), s, -jnp.inf)
            m = jnp.where(fresh, -jnp.inf, pm[hh])
            acc = jnp.where(fresh, 0.0, pacc[hh])
            m_new = jnp.maximum(m, jnp.max(s, axis=1, keepdims=True))
            p = jnp.exp2(s - wide(m_new)).astype(BF16)
            acc = jnp.exp2(m - m_new) * acc + _dot_nt(p, vta)
            if masked:
                e0 = half * (1 - hh)
                outs.append(acc / acc[:, e0:e0 + 1])
            else:
                pm[hh] = m_new
                pacc[hh] = acc
        if masked:
            op_ref[...] = jnp.where(first, outs[0], outs[1])

    sample_chunk()

    @pl.when(jnp.logical_and(valid, diag))
    def _():
        prompt_pair(True)

    @pl.when(jnp.logical_and(valid, jnp.logical_not(diag)))
    def _():
        prompt_pair(False)

    @pl.when(c == min(1, n_chunks - 1))
    def _():
        sy_ref[...] = _ssd_short_chunk(sx_ref, sz_ref, ssm_ref, ssmt_ref, (xp, zp, smp, smtp), stail_ref[...],
                                       sst_ref, sfin_ref, (cw_ref, cb_ref, arow_ref, acol_ref, dsk_ref, ng_ref))

    @pl.when(last_c)
    def _():
        knp[...] = jnp.zeros_like(knp)
        vnp[...] = jnp.zeros_like(vnp)
        knp[0:t_new, :] = kn_ref[...]
        vnp[0:t_new, :] = vn_ref[...]
        kn = knp[...].astype(BF16)
        vn = vnp[...].astype(BF16)
        s2 = _dot_nt(qb, kn) + cn_col - jnp.concatenate([cn] * t_new, axis=0)
        tok = _div_pow2(_iota(s2.shape, 0), heads)
        s2 = jnp.where(_iota(s2.shape, 1) <= tok, s2, -jnp.inf)
        online(0, s2, lambda p: _dot(p, vn))
        m_all = functools.reduce(jnp.maximum, [m_s[st] for st in range(STREAMS)])
        wts = [jnp.exp2(m_s[st] - m_all) for st in range(STREAMS)]
        o = (sum(wts[st] * acc_s[st] for st in range(STREAMS))
             / sum(wts[st] * l_s[st] for st in range(STREAMS)))
        keep = _div_pow2(_iota((heads, fdim), 1), FOX_DIM) == _iota((heads, fdim), 0)
        os_ref[...] = jnp.concatenate(
            [jnp.sum(jnp.where(keep, o[t * heads:(t + 1) * heads, :], 0.0), axis=0, keepdims=True)
             for t in range(t_new)], axis=0)


def _pair_schedule(batch, n_hp, nq, n_steps, n_chunks):
    pairs = [(b, h, i, j, 1) for b in range(batch) for h in range(n_hp) for i in range(nq) for j in range(i + 1)]
    n_idle = n_steps - len(pairs)
    assert n_idle >= 0, "more prompt attention pairs than sample chunks"
    steps = np.arange(n_steps)
    chunk = steps % n_chunks
    seq = steps // n_chunks
    cost = np.where(chunk == 0, 0, np.where(chunk == n_chunks - 1, 1 + seq % 2, 3))
    idle = np.zeros(n_steps, bool)
    idle[np.argsort(cost, kind="stable")[:n_idle]] = True
    rows, k = [], 0
    for g in range(n_steps):
        if idle[g]:
            rows.append(pairs[min(k, len(pairs) - 1)][:4] + (0,))
        else:
            rows.append(pairs[k])
            k += 1
    return [jnp.asarray(np.array(col, np.int32)) for col in zip(*rows)]


def _fox(page_table, qblk, kn, vn, lft, kc, vc, lc, q, ktb, vtb, ccol, crow4, tq,
         xbc_s, z_s, sm_s, smt4_s, st0, tail0, p):
    cdim = xbc_s.shape[2]
    hw = z_s.shape[2]
    bs, rows, fdim = qblk.shape
    t_new = rows // FORGET_HEADS
    n_pages = page_table.shape[1]
    page = kc.shape[-1]
    n_chunks = n_pages // PAGES_PER_CHUNK
    pc = PAGES_PER_CHUNK * page
    n_steps = bs * n_chunks
    bp, _, length = ktb.shape
    nq = length // tq
    n_hp = fdim // LANES
    sched = _pair_schedule(bp, n_hp, nq, n_steps, n_chunks)
    pos = np.arange(page)
    later = jnp.asarray(pos[:, None] > pos[None, :], BF16)
    rows_ph = np.arange(n_pages * FORGET_HEADS)
    lpages = jnp.asarray((rows_ph[None, :] % FORGET_HEADS == rows_ph[:, None] % FORGET_HEADS)
                         & (rows_ph[None, :] // FORGET_HEADS > rows_ph[:, None] // FORGET_HEADS), BF16)
    per_s = lambda g, *_: (g // n_chunks, 0, 0)
    grid_spec = pltpu.PrefetchScalarGridSpec(
        num_scalar_prefetch=6,
        grid=(n_steps,),
        in_specs=[
            pl.BlockSpec((None, rows, fdim), per_s),
            pl.BlockSpec((None, t_new, fdim), per_s),
            pl.BlockSpec((None, t_new, fdim), per_s),
            pl.BlockSpec((None, FORGET_HEADS, LANES), per_s),
            pl.BlockSpec(later.shape, lambda g, *_: (0, 0)),
            pl.BlockSpec(lpages.shape, lambda g, *_: (0, 0)),
            pl.BlockSpec(memory_space=pl.ANY),
            pl.BlockSpec(memory_space=pl.ANY),
            pl.BlockSpec(memory_space=pl.ANY),
            pl.BlockSpec((tq, LANES), lambda g, pt, pb, ph, pi, pj, pv: (pb[g] * nq + pi[g], ph[g])),
            pl.BlockSpec((None, LANES, tq), lambda g, pt, pb, ph, pi, pj, pv: (pb[g], ph[g], pj[g])),
            pl.BlockSpec((None, LANES, tq), lambda g, pt, pb, ph, pi, pj, pv: (pb[g], ph[g], pj[g])),
            pl.BlockSpec((tq, LANES), lambda g, pt, pb, ph, pi, pj, pv: (pb[g] * nq + pi[g], 0)),
            pl.BlockSpec((None, None, 2 * SUBLANES, tq), lambda g, pt, pb, ph, pi, pj, pv: (pb[g], pj[g], 0, 0)),
            pl.BlockSpec((None, t_new, cdim), per_s),
            pl.BlockSpec((None, t_new, hw), per_s),
            pl.BlockSpec((None, t_new, LANES), per_s),
            pl.BlockSpec((None, None, 2 * SUBLANES, t_new), lambda g, *_: (g // n_chunks, 0, 0, 0)),
            pl.BlockSpec((None, hw, SSD_STATE), per_s),
            pl.BlockSpec((None, SUBLANES, cdim), per_s),
        ] + _ssd_const_specs(cdim, hw),
        out_specs=[
            pl.BlockSpec((None, t_new, fdim), per_s),
            pl.BlockSpec((tq, LANES), lambda g, pt, pb, ph, pi, pj, pv: (pb[g] * nq + pi[g], ph[g])),
            pl.BlockSpec((None, t_new, hw), per_s),
            pl.BlockSpec((None, hw, SSD_STATE), per_s),
        ],
        scratch_shapes=[
            pltpu.VMEM((KV_SLOTS, FORGET_HEADS, FOX_DIM, pc), F32),
            pltpu.VMEM((KV_SLOTS, FORGET_HEADS, FOX_DIM, pc), F32),
            pltpu.VMEM((2, n_pages * FORGET_HEADS, page), F32),
            pltpu.VMEM((n_pages * FORGET_HEADS, page), F32),
            pltpu.VMEM((STREAMS, rows, 1), F32),
            pltpu.VMEM((STREAMS, rows, 1), F32),
            pltpu.VMEM((STREAMS, rows, fdim), F32),
            pltpu.VMEM((FORGET_HEADS, LANES), F32),
            pltpu.VMEM((LANES, fdim), F32),
            pltpu.VMEM((LANES, fdim), F32),
            pltpu.VMEM((2, tq, LANES), F32),
            pltpu.VMEM((2, tq, LANES), F32),
            pltpu.VMEM((2, tq, LANES), BF16),
        ] + _ssd_pad_scratch(cdim, hw) + [
            pltpu.SemaphoreType.DMA((KV_SLOTS,)),
            pltpu.SemaphoreType.DMA((KV_SLOTS,)),
            pltpu.SemaphoreType.DMA((2,)),
        ],
    )
    return pl.pallas_call(
        functools.partial(_fox_kernel, n_chunks=n_chunks, t_new=t_new),
        grid_spec=grid_spec,
        out_shape=[jax.ShapeDtypeStruct((bs, t_new, fdim), F32),
                   jax.ShapeDtypeStruct((bp * length, fdim), F32),
                   jax.ShapeDtypeStruct((bs, t_new, hw), F32),
                   jax.ShapeDtypeStruct((bs, hw, SSD_STATE), F32)],
        compiler_params=_cparams(("arbitrary",)),
        name="fox",
    )(page_table, *sched, qblk, kn, vn, lft, later, lpages, kc, vc, lc, q, ktb, vtb, ccol, crow4,
      xbc_s, z_s, sm_s, smt4_s, st0, tail0, *_ssd_consts(p))


def _ssd_chunk(x, z, sm, smt, prev, state_in, state_out, cw_ref, cb_ref, arow_ref, acol_ref, dsk_ref, ng_ref):
    q = CHUNK
    hw = SSD_HEADS * SSD_DIM
    gs = SSD_STATE

    row8 = _iota(prev.shape, 0)
    conv = cb_ref[...] + cw_ref[SSD_CONV - 1:SSD_CONV, :] * x
    for sh in range(1, SSD_CONV):
        rx = pltpu.roll(x, sh, 0)
        fix = jnp.where(row8 < sh, pltpu.roll(prev, sh, 0), rx[0:SUBLANES, :])
        xs_sh = jnp.concatenate([fix, rx[SUBLANES:, :]], axis=0)
        conv = conv + cw_ref[SSD_CONV - 1 - sh:SSD_CONV - sh, :] * xs_sh
    xc = _silu(conv)
    xs = xc[:, 0:hw]

    r = _iota((q, q), 0)
    cidx = _iota((q, q), 1)
    tril = cidx <= r
    lower = tril.astype(BF16)
    upper = (r <= cidx).astype(BF16)
    acum_c = _dot3_right(lower, sm * arow_ref[...])
    acum_r = _dot3_left(smt * acol_ref[...], upper)
    lane = _iota((1, LANES), 1)
    first = lane < SSD_DIM
    rowp = _iota((LANES, 1), 0) < SSD_DIM

    ys = []
    for pr in range(SSD_HEADS // 2):
        g = pr // (SSD_HEADS // 4)
        bg = xc[:, hw + g * gs: hw + (g + 1) * gs].astype(BF16)
        cg = xc[:, hw + 2 * gs + g * gs: hw + 2 * gs + (g + 1) * gs].astype(BF16)
        gmat = _dot_nt(cg, bg)
        xpair = xs[:, pr * LANES:(pr + 1) * LANES]
        xpb = xpair.astype(BF16)
        yd, te, ea, cd = [], [], [], []
        for hh in range(2):
            h = FORGET_HEADS + 2 * pr + hh
            ac = acum_c[:, h:h + 1]
            ar = acum_r[h:h + 1, :]
            dtr = smt[h:h + 1, :]
            dtc = sm[:, h:h + 1]
            last = acum_c[q - 1:q, h:h + 1]
            decay = jnp.exp(jnp.where(tril, ac - ar, -jnp.inf))
            sc = gmat * decay * dtr
            yd.append(_dot(sc.astype(BF16), xpb))
            te.append(jnp.exp(last - ac) * dtc)
            ea.append(jnp.exp(ac))
            cd.append(jnp.exp(last))
        ydiag = jnp.where(first, yd[0], yd[1])
        xsc = (xpair * jnp.where(first, te[0], te[1])).astype(BF16)
        cstate = _dot_tn(xsc, bg)
        prev_st = state_in[pr * LANES:(pr + 1) * LANES, :]
        yoff = _dot_nt(cg, prev_st.astype(BF16)) * jnp.where(first, ea[0], ea[1])
        state_out[pr * LANES:(pr + 1) * LANES, :] = prev_st * jnp.where(rowp, cd[0], cd[1]) + cstate
        ys.append(ydiag + yoff + dsk_ref[:, pr * LANES:(pr + 1) * LANES] * xpair)
    y = jnp.concatenate(ys, axis=1) * _silu(z)
    return _rms(y, ng_ref[...])


def _ssd_short_chunk(xbc_ref, z_ref, sm_ref, smt_ref, pads, prev, state_in, state_out, consts):
    t_real = xbc_ref.shape[0]
    xp, zp, smp, smtp = pads
    xp[...] = jnp.zeros_like(xp)
    zp[...] = jnp.zeros_like(zp)
    smp[...] = jnp.zeros_like(smp)
    smtp[...] = jnp.zeros_like(smtp)
    xp[0:t_real, :] = xbc_ref[...]
    zp[0:t_real, :] = z_ref[...]
    smp[0:t_real, :] = sm_ref[...]
    smtp[:, 0:t_real] = smt_ref[...]
    y = _ssd_chunk(xp[...], zp[...], smp[...], smtp[...], prev, state_in, state_out, *consts)
    return y[0:t_real, :]


def _ssd_pad_scratch(cdim, hw):
    return [pltpu.VMEM((CHUNK, cdim), F32), pltpu.VMEM((CHUNK, hw), F32),
            pltpu.VMEM((CHUNK, LANES), F32), pltpu.VMEM((2 * SUBLANES, CHUNK), F32)]


def _ssd_const_specs(cdim, hw):
    return [_const_spec((SSD_CONV, cdim)), _const_spec((1, cdim)), _const_spec((1, LANES)),
            _const_spec((2 * SUBLANES, 1)), _const_spec((1, hw)), _const_spec((1, hw))]


def _ssd_consts(p):
    return (p['ssd_conv_w'], p['ssd_conv_b'], p['a_row'], p['a_col'], p['d_row'], p['ssd_norm_g'])


def _ssd_kernel(xbc_ref, z_ref, sm_ref, smt_ref, cw_ref, cb_ref, arow_ref, acol_ref, dsk_ref, ng_ref,
                y_ref, fin_ref, state, tail):
    c = pl.program_id(1)

    @pl.when(c == 0)
    def _():
        state[...] = jnp.zeros_like(state)
        tail[...] = jnp.zeros_like(tail)

    x = xbc_ref[...]
    y_ref[...] = _ssd_chunk(x, z_ref[...], sm_ref[...], smt_ref[...], tail[...], state, state,
                            cw_ref, cb_ref, arow_ref, acol_ref, dsk_ref, ng_ref)
    tail[...] = x[CHUNK - SUBLANES:, :]

    @pl.when(c == pl.num_programs(1) - 1)
    def _():
        fin_ref[...] = state[...]


def _ssd(xbc, z, sm, smt4, p):
    batch, length, cdim = xbc.shape
    hw = z.shape[2]
    nc = length // CHUNK
    blk = lambda b, c: (b, c, 0)
    return pl.pallas_call(
        _ssd_kernel,
        grid=(batch, nc),
        in_specs=[
            pl.BlockSpec((None, CHUNK, cdim), blk),
            pl.BlockSpec((None, CHUNK, hw), blk),
            pl.BlockSpec((None, CHUNK, LANES), blk),
            pl.BlockSpec((None, None, 2 * SUBLANES, CHUNK), lambda b, c: (b, c, 0, 0)),
        ] + _ssd_const_specs(cdim, hw),
        out_specs=[pl.BlockSpec((None, CHUNK, hw), blk),
                   pl.BlockSpec((None, hw, SSD_STATE), lambda b, c: (b, 0, 0))],
        out_shape=[jax.ShapeDtypeStruct((batch, length, hw), F32),
                   jax.ShapeDtypeStruct((batch, hw, SSD_STATE), F32)],
        scratch_shapes=[pltpu.VMEM((hw, SSD_STATE), F32), pltpu.VMEM((SUBLANES, cdim), F32)],
        compiler_params=_cparams(("parallel", "arbitrary")),
        name="ssd",
    )(xbc, z, sm, smt4, *_ssd_consts(p))


def _mem_kv_kernel(m_ref, g_ref, w_ref, gk_ref, k_ref, v_ref):
    mw = MEM_HEADS * MEM_DIM
    xn = _rms(m_ref[...], g_ref[...]).astype(BF16)
    kv = _dot(xn, w_ref[...])
    for h in range(MEM_HEADS):
        sl = slice(h * MEM_DIM, (h + 1) * MEM_DIM)
        k_ref[:, sl] = _rms(kv[:, sl], gk_ref[...])
    v_ref[...] = kv[:, mw:]


def _mem_kv(mem, p, tm):
    n, d = mem.shape
    mw = MEM_HEADS * MEM_DIM
    row = lambda i: (i, 0)
    return pl.pallas_call(
        _mem_kv_kernel,
        grid=(n // tm,),
        in_specs=[pl.BlockSpec((tm, d), row), _const_spec((1, d)), _const_spec((d, 2 * mw)),
                  _const_spec((1, MEM_DIM))],
        out_specs=[pl.BlockSpec((tm, mw), row), pl.BlockSpec((tm, mw), row)],
        out_shape=[jax.ShapeDtypeStruct((n, mw), F32), jax.ShapeDtypeStruct((n, mw), F32)],
        compiler_params=_cparams(("parallel",)),
        name="mem_kv",
    )(mem, p['mem_in_norm_g'], p['w_mem_kv'], p['mem_k_norm_g'])


def _mix_out(x_ref, fo_ref, so_ref, wout_ref, g_ref, wq_ref, gq_ref):
    fw = fo_ref.shape[-1]
    y1 = x_ref[...] + _dot(fo_ref[...].astype(BF16), wout_ref[0:fw, :]) + _dot(so_ref[...].astype(BF16), wout_ref[fw:, :])
    xn = _rms(y1, g_ref[...]).astype(BF16)
    q = _dot(xn, wq_ref[...])
    qs = []
    for h in range(MEM_HEADS):
        qh = _rms(q[:, h * MEM_DIM:(h + 1) * MEM_DIM], gq_ref[...]) * MEM_DIM ** -0.5
        qs.append(qh)
    return y1, qs


def _memattn_prompt_kernel(x_ref, fo_ref, so_ref, wout_ref, g_ref, wq_ref, gq_ref, mk_ref, mv_ref, wo_ref, o_ref):
    y1, qs = _mix_out(x_ref, fo_ref, so_ref, wout_ref, g_ref, wq_ref, gq_ref)
    outs = []
    for h in range(MEM_HEADS):
        sl = slice(h * MEM_DIM, (h + 1) * MEM_DIM)
        s = _dot_nt(qs[h].astype(BF16), mk_ref[:, sl].astype(BF16))
        p = jnp.exp(s - jnp.max(s, axis=1, keepdims=True))
        o = _dot(p.astype(BF16), mv_ref[:, sl].astype(BF16)) / jnp.sum(p, axis=1, keepdims=True)
        outs.append(o.astype(BF16))
    o_ref[...] = y1 + _dot(jnp.concatenate(outs, axis=1), wo_ref[...])


def _memattn_prompt(x, fo, so, mk, mv, p, tm, length, n_mem):
    n, d = x.shape
    fw = fo.shape[1]
    mw = MEM_HEADS * MEM_DIM
    per = length // tm
    row = lambda i: (i, 0)
    return pl.pallas_call(
        _memattn_prompt_kernel,
        grid=(n // tm,),
        in_specs=[pl.BlockSpec((tm, d), row), pl.BlockSpec((tm, fw), row), pl.BlockSpec((tm, fw), row),
                  _const_spec((2 * fw, d)), _const_spec((1, d)), _const_spec((d, mw)), _const_spec((1, MEM_DIM)),
                  pl.BlockSpec((n_mem, mw), lambda i: (i // per, 0)),
                  pl.BlockSpec((n_mem, mw), lambda i: (i // per, 0)),
                  _const_spec((mw, d))],
        out_specs=pl.BlockSpec((tm, d), row),
        out_shape=jax.ShapeDtypeStruct((n, d), F32),
        compiler_params=_cparams(("parallel",)),
        name="memattn_prompt",
    )(x, fo, so, p['w_out'], p['norm_mem_g'], p['w_mem_q'], p['mem_q_norm_g'], mk, mv, p['w_mem_o'])


def _memattn_sample_kernel(x_ref, fo_ref, so_ref, wout_ref, g_ref, wq_ref, gq_ref, mk_ref, mv_ref, wo_ref, o_ref,
                           *, t_new):
    y1, qs = _mix_out(x_ref, fo_ref, so_ref, wout_ref, g_ref, wq_ref, gq_ref)
    tm = y1.shape[0]
    per = SUBLANES // t_new
    nrow = MEM_HEADS * SUBLANES
    ncol = mk_ref.shape[1]
    row = _iota((nrow, 1), 0)
    mine = [_div_pow2(lax.bitwise_and(row, jnp.int32(SUBLANES - 1)), t_new) == u for u in range(per)]
    same_head = (lax.bitwise_and(_iota((nrow, ncol), 1), jnp.int32(MEM_HEADS - 1))
                 == _div_pow2(_iota((nrow, ncol), 0), SUBLANES))
    tiles = []
    for r in range(tm // SUBLANES):
        qst = jnp.concatenate([qs[h][r * SUBLANES:(r + 1) * SUBLANES, :] for h in range(MEM_HEADS)], axis=0)
        qst = qst.astype(BF16)
        s = None
        for u in range(per):
            su = _dot_nt(qst, mk_ref[r * per + u].astype(BF16))
            s = su if s is None else jnp.where(mine[u], su, s)
        s = jnp.where(same_head, s, -jnp.inf)
        p = jnp.exp(s - jnp.max(s, axis=1, keepdims=True))
        pb = p.astype(BF16)
        o = None
        for u in range(per):
            ou = _dot(pb, mv_ref[r * per + u].astype(BF16))
            o = ou if o is None else jnp.where(mine[u], ou, o)
        o = o / jnp.sum(p, axis=1, keepdims=True)
        tiles.append(jnp.concatenate([o[h * SUBLANES:(h + 1) * SUBLANES, :] for h in range(MEM_HEADS)], axis=1))
    o_ref[...] = y1 + _dot(jnp.concatenate(tiles, axis=0).astype(BF16), wo_ref[...])


def _memattn_sample(x, fo, so, mk, mv, p, tm, t_new):
    n, d = x.shape
    fw = fo.shape[1]
    mw = MEM_HEADS * MEM_DIM
    n_flat = mk.shape[1]
    bb = tm // t_new
    row = lambda i: (i, 0)
    return pl.pallas_call(
        functools.partial(_memattn_sample_kernel, t_new=t_new),
        grid=(n // tm,),
        in_specs=[pl.BlockSpec((tm, d), row), pl.BlockSpec((tm, fw), row), pl.BlockSpec((tm, fw), row),
                  _const_spec((2 * fw, d)), _const_spec((1, d)), _const_spec((d, mw)), _const_spec((1, MEM_DIM)),
                  pl.BlockSpec((bb, n_flat, MEM_DIM), lambda i: (i, 0, 0)),
                  pl.BlockSpec((bb, n_flat, MEM_DIM), lambda i: (i, 0, 0)),
                  _const_spec((mw, d))],
        out_specs=pl.BlockSpec((tm, d), row),
        out_shape=jax.ShapeDtypeStruct((n, d), F32),
        compiler_params=_cparams(("parallel",)),
        name="memattn_sample",
    )(x, fo, so, p['w_out'], p['norm_mem_g'], p['w_mem_q'], p['mem_q_norm_g'], mk, mv, p['w_mem_o'])


FF_CHUNK = 1024


def _ff_chunks(dff):
    return [(s, min(s + FF_CHUNK, dff)) for s in range(0, dff, FF_CHUNK)]


def _ffn_prompt_kernel(x_ref, g_ref, wg_ref, wu_ref, cw_ref, cb_ref, wd_ref, o_ref, tail_ref, halo, *, per):
    i = pl.program_id(0)
    tm = x_ref.shape[0]
    dff = wg_ref.shape[1]

    @pl.when(i % per == 0)
    def _():
        halo[...] = jnp.zeros_like(halo)

    x = x_ref[...]
    xn = _rms(x, g_ref[...]).astype(BF16)
    row8 = _iota((SUBLANES, 1), 0)
    acc = x
    for lo, hi in _ff_chunks(dff):
        gate = _dot(xn, wg_ref[:, lo:hi])
        up = _dot(xn, wu_ref[:, lo:hi])
        prev = halo[:, lo:hi]
        conv = cb_ref[:, lo:hi] + cw_ref[FFN_CONV - 1:FFN_CONV, lo:hi] * gate
        for sh in range(1, FFN_CONV):
            rg = pltpu.roll(gate, sh, 0)
            fix = jnp.where(row8 < sh, pltpu.roll(prev, sh, 0), rg[0:SUBLANES, :])
            conv = conv + cw_ref[FFN_CONV - 1 - sh:FFN_CONV - sh, lo:hi] * jnp.concatenate([fix, rg[SUBLANES:, :]], axis=0)
        halo[:, lo:hi] = gate[tm - SUBLANES:tm, :]
        hmid = (_silu(conv) * up).astype(BF16)
        acc = acc + _dot(hmid, wd_ref[lo:hi, :])
    o_ref[...] = acc
    tail_ref[...] = halo[...]


def _ffn_prompt(x, p, tm, batch, length):
    n, d = x.shape
    dff = p['w_ffn_gate'].shape[1]
    per = length // tm
    row = lambda i: (i, 0)
    return pl.pallas_call(
        functools.partial(_ffn_prompt_kernel, per=per),
        grid=(n // tm,),
        in_specs=[pl.BlockSpec((tm, d), row), _const_spec((1, d)),
                  pl.BlockSpec((d, dff), lambda i: (0, 0), pipeline_mode=pl.Buffered(1)),
                  pl.BlockSpec((d, dff), lambda i: (0, 0), pipeline_mode=pl.Buffered(1)),
                  _const_spec((FFN_CONV, dff)), _const_spec((1, dff)),
                  pl.BlockSpec((dff, d), lambda i: (0, 0), pipeline_mode=pl.Buffered(1))],
        out_specs=[pl.BlockSpec((tm, d), row),
                   pl.BlockSpec((None, SUBLANES, dff), lambda i: (i // per, 0, 0))],
        out_shape=[jax.ShapeDtypeStruct((n, d), F32),
                   jax.ShapeDtypeStruct((batch, SUBLANES, dff), F32)],
        scratch_shapes=[pltpu.VMEM((SUBLANES, dff), F32)],
        compiler_params=_cparams(("arbitrary",)),
        name="ffn_prompt",
    )(x, p['norm_ffn_g'], p['w_ffn_gate'], p['w_ffn_up'], p['ffn_conv_w'], p['ffn_conv_b'], p['w_ffn_down'])


def _ffn_sample_kernel(x_ref, buf_ref, g_ref, wg_ref, wu_ref, cw_ref, cb_ref, wd_ref, o_ref, nbuf_ref, *, t_new):
    nb = x_ref.shape[0] // t_new
    dff = wg_ref.shape[1]
    x = x_ref[...]
    xn = _rms(x, g_ref[...]).astype(BF16)
    acc = x
    for lo, hi in _ff_chunks(dff):
        gate = _dot(xn, wg_ref[:, lo:hi])
        up = _dot(xn, wu_ref[:, lo:hi])
        ext = jnp.concatenate([buf_ref[:, lo:hi], gate], axis=0)
        conv = cb_ref[:, lo:hi]
        for j in range(FFN_CONV):
            conv = conv + cw_ref[j:j + 1, lo:hi] * ext[j * nb:(j + t_new) * nb, :]
        nbuf_ref[:, lo:hi] = ext[t_new * nb:, :]
        hmid = (_silu(conv) * up).astype(BF16)
        acc = acc + _dot(hmid, wd_ref[lo:hi, :])
    o_ref[...] = acc


def _ffn_sample(x_tm, buf_tm, p, t_new):
    n, d = x_tm.shape
    dff = p['w_ffn_gate'].shape[1]
    nbr = buf_tm.shape[0]
    return pl.pallas_call(
        functools.partial(_ffn_sample_kernel, t_new=t_new),
        grid=(1,),
        in_specs=[_const_spec((n, d)), _const_spec((nbr, dff)), _const_spec((1, d)),
                  pl.BlockSpec((d, dff), lambda i: (0, 0), pipeline_mode=pl.Buffered(1)),
                  pl.BlockSpec((d, dff), lambda i: (0, 0), pipeline_mode=pl.Buffered(1)),
                  _const_spec((FFN_CONV, dff)), _const_spec((1, dff)),
                  pl.BlockSpec((dff, d), lambda i: (0, 0), pipeline_mode=pl.Buffered(1))],
        out_specs=[_const_spec((n, d)), _const_spec((nbr, dff))],
        out_shape=[jax.ShapeDtypeStruct((n, d), F32), jax.ShapeDtypeStruct((nbr, dff), F32)],
        compiler_params=_cparams(("arbitrary",)),
        name="ffn_sample",
    )(x_tm, buf_tm, p['norm_ffn_g'], p['w_ffn_gate'], p['w_ffn_up'], p['ffn_conv_w'], p['ffn_conv_b'],
      p['w_ffn_down'])


def _prep_layer(l, norm_mix_g, w_in, b_forget, fox_q_norm_g, fox_k_norm_g, ssd_conv_w, ssd_conv_b, ssd_dt_bias,
                ssd_a_log, ssd_d, ssd_norm_g, w_out, norm_mem_g, mem_in_norm_g, w_mem_q, w_mem_kv, mem_q_norm_g,
                mem_k_norm_g, w_mem_o, norm_ffn_g, w_ffn_gate, w_ffn_up, ffn_conv_w, ffn_conv_b, w_ffn_down):
    fw = FORGET_HEADS * FOX_DIM
    hw = SSD_HEADS * SSD_DIM
    w = w_in[l]
    o_f = 3 * fw
    o_z = o_f + FORGET_HEADS
    o_x = o_z + hw
    o_dt = w.shape[1] - SSD_HEADS
    w_small = jnp.concatenate([w[:, o_f:o_z], w[:, o_dt:]], axis=1)
    b_small = jnp.concatenate([b_forget[l], ssd_dt_bias[l]]).astype(F32)
    a_neg = -jnp.exp(ssd_a_log[l].astype(F32))
    a16 = jnp.concatenate([jnp.zeros((FORGET_HEADS,), F32), a_neg])
    head_of = jnp.arange(fw) // FOX_DIM
    onehot = (head_of[:, None] == jnp.arange(LANES)[None, :]).astype(F32)
    row = lambda v: v.reshape(1, -1).astype(F32)
    return {
        'norm_mix_g': row(norm_mix_g[l]),
        'w_q': w[:, :fw].astype(BF16),
        'w_kvt': w[:, fw:o_f].T.astype(BF16),
        'w_zx': w[:, o_z:o_dt].astype(BF16),
        'w_sm': jnp.pad(w_small, ((0, 0), (0, LANES - w_small.shape[1]))).astype(BF16),
        'w_smt': w_small.T.astype(BF16),
        'b_sm': jnp.pad(b_small, (0, LANES - b_small.shape[0])).reshape(1, LANES),
        'b_smt': b_small.reshape(-1, 1),
        'gq': row(jnp.tile(fox_q_norm_g[l], FORGET_HEADS)),
        'gk_col': jnp.tile(fox_k_norm_g[l], FORGET_HEADS).reshape(-1, 1).astype(F32),
        'head_reduce': (onehot / FOX_DIM).astype(BF16),
        'head_expand': onehot.T.astype(BF16),
        'ssd_conv_w': ssd_conv_w[l].astype(F32),
        'ssd_conv_b': row(ssd_conv_b[l]),
        'a_row': jnp.pad(a16, (0, LANES - a16.shape[0])).reshape(1, LANES),
        'a_col': a16.reshape(-1, 1),
        'd_row': row(jnp.repeat(ssd_d[l], SSD_DIM)),
        'ssd_norm_g': row(ssd_norm_g[l]),
        'w_out': w_out[l].astype(BF16),
        'norm_mem_g': row(norm_mem_g[l]),
        'mem_in_norm_g': row(mem_in_norm_g[l]),
        'w_mem_q': w_mem_q[l].astype(BF16),
        'w_mem_kv': w_mem_kv[l].astype(BF16),
        'mem_q_norm_g': row(mem_q_norm_g[l]),
        'mem_k_norm_g': row(mem_k_norm_g[l]),
        'w_mem_o': w_mem_o[l].astype(BF16),
        'norm_ffn_g': row(norm_ffn_g[l]),
        'w_ffn_gate': w_ffn_gate[l].astype(BF16),
        'w_ffn_up': w_ffn_up[l].astype(BF16),
        'ffn_conv_w': ffn_conv_w[l].astype(F32),
        'ffn_conv_b': row(ffn_conv_b[l]),
        'w_ffn_down': w_ffn_down[l].astype(BF16),
    }


def _row_tile(n, want):
    t = min(n, want)
    while n % t:
        t //= 2
    return t


def _layer(xp, mem, xs, kc, vc, lc, st_ssm, st_conv, st_ffn, mem_k, mem_v, page_table, p):
    fw = FORGET_HEADS * FOX_DIM
    hw = SSD_HEADS * SSD_DIM
    heads = lambda a, b_, t_: a.reshape(b_, FORGET_HEADS, FOX_DIM, t_).transpose(0, 3, 1, 2)

    bp, length, d = xp.shape
    n_p = bp * length
    n_mem = mem.shape[1]
    tm = _row_tile(length, 512)
    xpf = xp.reshape(n_p, d)
    q_p, kt_p, vt_p, ktb_p, vtb_p, z_p, xbc_p, sm_p, smt_p = _inproj(xpf, p, tm, bp, length)
    bs, t_new, _ = xs.shape
    n_s = bs * t_new
    xsf = xs.reshape(n_s, d)
    q_s, kt_s, vt_s, _, _, z_s, xbc_s, sm_s, smt_s = _inproj(xsf, p, _row_tile(n_s, 512), 1, n_s)

    ccol, crow = _cumsum(sm_p, smt_p, bp, length)
    tq = _row_tile(length, 512)
    nq = length // tq
    crow4 = crow.reshape(2 * SUBLANES, bp, nq, tq).transpose(1, 2, 0, 3)
    k_s = kt_s[0].T.reshape(bs, t_new, fw)
    v_s = vt_s[0].T.reshape(bs, t_new, fw)
    own = (jnp.arange(fw) // FOX_DIM)[None, :] == jnp.arange(FORGET_HEADS)[:, None]
    qblk = jnp.where(own[None, None], q_s.reshape(bs, t_new, 1, fw), jnp.zeros((), q_s.dtype))
    qblk = qblk.reshape(bs, t_new * FORGET_HEADS, fw)
    lft = jnp.pad(smt_s[:FORGET_HEADS].reshape(FORGET_HEADS, bs, t_new).transpose(1, 0, 2),
                  ((0, 0), (0, 0), (0, LANES - t_new)))
    cdim = xbc_p.shape[1]
    xbc3_s = xbc_s.reshape(bs, t_new, cdim)
    smt4_s = smt_s.reshape(2 * SUBLANES, bs, 1, t_new).transpose(1, 2, 0, 3)
    tail0 = jnp.pad(st_conv, ((0, 0), (SUBLANES - (SSD_CONV - 1), 0), (0, 0)))
    fox_s, fox_p, ssd_s, ssm_new = _fox(
        page_table, qblk, k_s, v_s, lft, kc, vc, lc, q_p, ktb_p, vtb_p, ccol, crow4, tq,
        xbc3_s, z_s.reshape(bs, t_new, hw), sm_s.reshape(bs, t_new, LANES), smt4_s,
        st_ssm.reshape(bs, hw, SSD_STATE), tail0, p)

    nc = length // CHUNK
    smt4_p = smt_p.reshape(2 * SUBLANES, bp, nc, CHUNK).transpose(1, 2, 0, 3)
    xbc3_p = xbc_p.reshape(bp, length, cdim)
    ssd_p, ssm_fin = _ssd(xbc3_p, z_p.reshape(bp, length, hw), sm_p.reshape(bp, length, LANES), smt4_p, p)

    mk, mv = _mem_kv(mem.reshape(bp * n_mem, d), p, _row_tile(n_mem, 512))
    y2_p = _memattn_prompt(xpf, fox_p, ssd_p.reshape(n_p, hw), mk, mv, p, tm, length, n_mem)
    flat = lambda a: a.reshape(a.shape[0], a.shape[1] * a.shape[2], a.shape[3])
    y2_s = _memattn_sample(xsf, fox_s.reshape(n_s, fw), ssd_s.reshape(n_s, hw), flat(mem_k), flat(mem_v), p,
                           _row_tile(n_s, 32), t_new)

    y3_p, ffn_tail = _ffn_prompt(y2_p, p, tm, bp, length)
    x_tm = y2_s.reshape(bs, t_new, d).transpose(1, 0, 2).reshape(n_s, d)
    dff = st_ffn.shape[-1]
    buf_tm = st_ffn.transpose(1, 0, 2).reshape((FFN_CONV - 1) * bs, dff)
    y3_tm, nbuf_tm = _ffn_sample(x_tm, buf_tm, p, t_new)

    prompt = (y3_p.reshape(bp, length, d),
              heads(kt_p, bp, length),
              heads(vt_p, bp, length),
              sm_p.reshape(bp, length, LANES)[:, :, :FORGET_HEADS],
              ssm_fin.reshape(bp, SSD_HEADS, SSD_DIM, SSD_STATE),
              xbc3_p[:, length - (SSD_CONV - 1):, :],
              ffn_tail[:, SUBLANES - (FFN_CONV - 1):, :],
              mk.reshape(bp, n_mem, MEM_HEADS, MEM_DIM),
              mv.reshape(bp, n_mem, MEM_HEADS, MEM_DIM))
    sample = (y3_tm.reshape(t_new, bs, d).transpose(1, 0, 2),
              k_s.reshape(bs, t_new, FORGET_HEADS, FOX_DIM),
              v_s.reshape(bs, t_new, FORGET_HEADS, FOX_DIM),
              sm_s.reshape(bs, t_new, LANES)[:, :, :FORGET_HEADS],
              ssm_new.reshape(bs, SSD_HEADS, SSD_DIM, SSD_STATE),
              jnp.concatenate([st_conv, xbc3_s], axis=1)[:, t_new:, :],
              nbuf_tm.reshape(FFN_CONV - 1, bs, dff).transpose(1, 0, 2))
    return prompt, sample


def kernel(x_prompt, mem_prompt, x_sample, cache_fox_k, cache_fox_v, cache_fox_logf, state_ssm, state_ssm_conv, state_ffn_conv, cache_mem_k, cache_mem_v, page_table, norm_mix_g, w_in, b_forget, fox_q_norm_g, fox_k_norm_g, ssd_conv_w, ssd_conv_b, ssd_dt_bias, ssd_a_log, ssd_d, ssd_norm_g, w_out, norm_mem_g, mem_in_norm_g, w_mem_q, w_mem_kv, mem_q_norm_g, mem_k_norm_g, w_mem_o, norm_ffn_g, w_ffn_gate, w_ffn_up, ffn_conv_w, ffn_conv_b, w_ffn_down):
    depth = w_in.shape[0]
    yp, ys = x_prompt, x_sample
    kc_all = jnp.transpose(cache_fox_k, (0, 1, 3, 4, 2))
    vc_all = jnp.transpose(cache_fox_v, (0, 1, 3, 4, 2))
    lc_all = jnp.transpose(cache_fox_logf, (0, 1, 3, 2))
    pouts, souts = [], []
    for l in range(depth):
        p = _prep_layer(l, norm_mix_g, w_in, b_forget, fox_q_norm_g, fox_k_norm_g, ssd_conv_w, ssd_conv_b,
                        ssd_dt_bias, ssd_a_log, ssd_d, ssd_norm_g, w_out, norm_mem_g, mem_in_norm_g, w_mem_q,
                        w_mem_kv, mem_q_norm_g, mem_k_norm_g, w_mem_o, norm_ffn_g, w_ffn_gate, w_ffn_up,
                        ffn_conv_w, ffn_conv_b, w_ffn_down)
        po, so = _layer(yp, mem_prompt, ys, kc_all[l], vc_all[l], lc_all[l], state_ssm[l], state_ssm_conv[l],
                        state_ffn_conv[l], cache_mem_k[l], cache_mem_v[l], page_table, p)
        yp, ys = po[0], so[0]
        pouts.append(po[1:])
        souts.append(so[1:])
    stack = lambda outs, i: jnp.stack([o[i] for o in outs])
    return (yp, ys) + tuple(stack(pouts, i) for i in range(8)) + tuple(stack(souts, i) for i in range(6))
```

```python
import functools

import numpy as np
import jax
import jax.numpy as jnp
from jax import lax
from jax.experimental import pallas as pl
from jax.experimental.pallas import tpu as pltpu

F32 = jnp.float32
BF16 = jnp.bfloat16
EPS = 1e-6
LOG2E = 1.4426950408889634
FORGET_HEADS = 8
FOX_DIM = 64
SSD_HEADS = 8
SSD_DIM = 64
SSD_STATE = 128
SSD_CONV = 4
CHUNK = 128
MEM_HEADS = 4
MEM_DIM = 128
FFN_CONV = 3
LANES = 128
SUBLANES = 8
VMEM_LIMIT = 56 * 1024 * 1024
NT = (((1,), (1,)), ((), ()))
TN = (((0,), (0,)), ((), ()))


def _dot(a, b):
    return jnp.dot(a, b, preferred_element_type=F32)


def _dot_nt(a, b):
    return lax.dot_general(a, b, NT, preferred_element_type=F32)


def _dot_tn(a, b):
    return lax.dot_general(a, b, TN, preferred_element_type=F32)


def _split3(x):
    hi = x.astype(BF16)
    r1 = x - hi.astype(F32)
    mid = r1.astype(BF16)
    lo = (r1 - mid.astype(F32)).astype(BF16)
    return hi, mid, lo


def _dot3_left(x, m):
    hi, mid, lo = _split3(x)
    return _dot(hi, m) + _dot(mid, m) + _dot(lo, m)


def _dot3_right(m, x):
    hi, mid, lo = _split3(x)
    return _dot(m, hi) + _dot(m, mid) + _dot(m, lo)


def _dot2_left(x, m):
    hi = x.astype(BF16)
    lo = (x - hi.astype(F32)).astype(BF16)
    return _dot(hi, m) + _dot(lo, m)


def _rms(x, g):
    return x * lax.rsqrt(jnp.mean(x * x, axis=-1, keepdims=True) + EPS) * g


def _softplus(x):
    return jnp.maximum(x, 0.0) + jnp.log1p(jnp.exp(-jnp.abs(x)))


def _log_sigmoid(x):
    return -_softplus(-x)


def _silu(x):
    return x * (1.0 / (1.0 + jnp.exp(-x)))


def _iota(shape, dim):
    return lax.broadcasted_iota(jnp.int32, shape, dim)


def _div_pow2(x, d):
    assert d & (d - 1) == 0
    return lax.shift_right_logical(x, jnp.int32(d.bit_length() - 1))


def _cparams(sem):
    return pltpu.CompilerParams(dimension_semantics=sem, vmem_limit_bytes=VMEM_LIMIT)


def _const_spec(shape):
    n = len(shape)
    return pl.BlockSpec(shape, lambda *_: (0,) * n)


def _inproj_kernel(x_ref, g_ref, wq_ref, wkvt_ref, wzx_ref, wsm_ref, wsmt_ref, gq_ref, gkc_ref, red_ref, exp_ref,
                   bsm_ref, bsmt_ref,
                   q_ref, kt_ref, vt_ref, ktb_ref, vtb_ref, z_ref, xbc_ref, sm_ref, smt_ref):
    fw = FORGET_HEADS * FOX_DIM
    tm = x_ref.shape[0]
    xn = _rms(x_ref[...], g_ref[...]).astype(BF16)

    q = _dot(xn, wq_ref[...])
    rs = lax.rsqrt(_dot2_left(q * q, red_ref[...]) + EPS)
    q_ref[...] = (q * _dot2_left(rs, exp_ref[...]) * gq_ref[...] * (FOX_DIM ** -0.5 * LOG2E)).astype(BF16)

    kv = _dot_nt(wkvt_ref[...], xn)
    k3 = kv[0:fw, :].reshape(FORGET_HEADS, FOX_DIM, tm)
    k3 = k3 * lax.rsqrt(jnp.mean(k3 * k3, axis=1, keepdims=True) + EPS)
    kn = k3.reshape(fw, tm) * gkc_ref[...]
    kt_ref[...] = kn
    ktb_ref[...] = kn.astype(BF16)
    vt = kv[fw:, :]
    vt_ref[...] = vt
    vtb_ref[...] = vt.astype(BF16)

    z_ref[...] = _dot(xn, wzx_ref[:, 0:fw])
    xbc_ref[...] = _dot(xn, wzx_ref[:, fw:])
    sm = _dot(xn, wsm_ref[...]) + bsm_ref[...]
    sm_ref[...] = jnp.where(_iota(sm.shape, 1) < FORGET_HEADS, _log_sigmoid(sm), _softplus(sm))
    smt = _dot_nt(wsmt_ref[...], xn) + bsmt_ref[...]
    smt_ref[...] = jnp.where(_iota(smt.shape, 0) < FORGET_HEADS, _log_sigmoid(smt), _softplus(smt))


def _inproj(x, p, tm, batch, length):
    n, d = x.shape
    fw = FORGET_HEADS * FOX_DIM
    nzx = p['w_zx'].shape[1]
    per = length // tm
    row = lambda i: (i, 0)
    tr = lambda i: (i // per, 0, i % per)
    return pl.pallas_call(
        _inproj_kernel,
        grid=(n // tm,),
        in_specs=[
            pl.BlockSpec((tm, d), row),
            _const_spec((1, d)),
            _const_spec((d, fw)),
            _const_spec((2 * fw, d)),
            _const_spec((d, nzx)),
            _const_spec((d, LANES)),
            _const_spec((2 * SUBLANES, d)),
            _const_spec((1, fw)),
            _const_spec((fw, 1)),
            _const_spec((fw, LANES)),
            _const_spec((LANES, fw)),
            _const_spec((1, LANES)),
            _const_spec((2 * SUBLANES, 1)),
        ],
        out_specs=[
            pl.BlockSpec((tm, fw), row),
            pl.BlockSpec((None, fw, tm), tr),
            pl.BlockSpec((None, fw, tm), tr),
            pl.BlockSpec((None, fw, tm), tr),
            pl.BlockSpec((None, fw, tm), tr),
            pl.BlockSpec((tm, fw), row),
            pl.BlockSpec((tm, nzx - fw), row),
            pl.BlockSpec((tm, LANES), row),
            pl.BlockSpec((2 * SUBLANES, tm), lambda i: (0, i)),
        ],
        out_shape=[
            jax.ShapeDtypeStruct((n, fw), BF16),
            jax.ShapeDtypeStruct((batch, fw, length), F32),
            jax.ShapeDtypeStruct((batch, fw, length), F32),
            jax.ShapeDtypeStruct((batch, fw, length), BF16),
            jax.ShapeDtypeStruct((batch, fw, length), BF16),
            jax.ShapeDtypeStruct((n, fw), F32),
            jax.ShapeDtypeStruct((n, nzx - fw), F32),
            jax.ShapeDtypeStruct((n, LANES), F32),
            jax.ShapeDtypeStruct((2 * SUBLANES, n), F32),
        ],
        compiler_params=_cparams(("parallel",)),
        name="inproj",
    )(x, p['norm_mix_g'], p['w_q'], p['w_kvt'], p['w_zx'], p['w_sm'], p['w_smt'], p['gq'], p['gk_col'],
      p['head_reduce'], p['head_expand'], p['b_sm'], p['b_smt'])


def _cumsum_kernel(sm_ref, smt_ref, col_ref, row_ref):
    length = sm_ref.shape[0]
    r = _iota((LANES, LANES), 0)
    c = _iota((LANES, LANES), 1)
    lower = (c <= r).astype(BF16)
    upper = (r <= c).astype(BF16)
    carry_c = jnp.zeros((1, LANES), F32)
    carry_r = jnp.zeros((smt_ref.shape[0], 1), F32)
    for b in range(length // LANES):
        sl = slice(b * LANES, (b + 1) * LANES)
        cs = _dot3_right(lower, sm_ref[sl, :]) + carry_c
        col_ref[sl, :] = cs * LOG2E
        carry_c = cs[LANES - 1:LANES, :]
        rs = _dot3_left(smt_ref[:, sl], upper) + carry_r
        row_ref[:, sl] = rs * LOG2E
        carry_r = rs[:, LANES - 1:LANES]


def _cumsum(sm, smt, batch, length):
    n = sm.shape[0]
    return pl.pallas_call(
        _cumsum_kernel,
        grid=(batch,),
        in_specs=[pl.BlockSpec((length, LANES), lambda b: (b, 0)),
                  pl.BlockSpec((2 * SUBLANES, length), lambda b: (0, b))],
        out_specs=[pl.BlockSpec((length, LANES), lambda b: (b, 0)),
                   pl.BlockSpec((2 * SUBLANES, length), lambda b: (0, b))],
        out_shape=[jax.ShapeDtypeStruct((n, LANES), F32),
                   jax.ShapeDtypeStruct((2 * SUBLANES, n), F32)],
        compiler_params=_cparams(("parallel",)),
        name="logf_cumsum",
    )(sm, smt)


PAGES_PER_CHUNK = 32
KV_SLOTS = 2
STREAMS = 2


def _fox_kernel(pt_ref, pb_ref, ph_ref, pi_ref, pj_ref, pv_ref,
                qb_ref, kn_ref, vn_ref, lft_ref, later_ref, lpages_ref, kc_ref, vc_ref, lc_ref,
                q_ref, kt_ref, vt_ref, cc_ref, cr_ref,
                sx_ref, sz_ref, ssm_ref, ssmt_ref, sst_ref, stail_ref,
                cw_ref, cb_ref, arow_ref, acol_ref, dsk_ref, ng_ref,
                os_ref, op_ref, sy_ref, sfin_ref,
                kbuf, vbuf, lbuf, suf, m_s, l_s, acc_s, cn_s, knp, vnp, pm, pacc, pqa, xp, zp, smp, smtp,
                ksem, vsem, lsem,
                *, n_chunks, t_new):
    g = pl.program_id(0)
    n_steps = pl.num_programs(0)
    b = g // n_chunks
    c = g % n_chunks
    slot = g % KV_SLOTS
    ppc = PAGES_PER_CHUNK
    n_pages = n_chunks * ppc
    page = kbuf.shape[-1] // ppc
    heads = FORGET_HEADS
    fdim = heads * FOX_DIM

    def kv_copies(bb, cc, sl):
        cps = []
        for j in range(ppc):
            pg = pt_ref[bb, cc * ppc + j]
            cps.append(pltpu.make_async_copy(kc_ref.at[pg], kbuf.at[sl, :, :, pl.ds(j * page, page)], ksem.at[sl]))
            cps.append(pltpu.make_async_copy(vc_ref.at[pg], vbuf.at[sl, :, :, pl.ds(j * page, page)], vsem.at[sl]))
        return cps

    def lf_copies(bb, sl):
        return [pltpu.make_async_copy(lc_ref.at[pt_ref[bb, j]], lbuf.at[sl, pl.ds(j * heads, heads), :], lsem.at[sl])
                for j in range(n_pages)]

    @pl.when(g == 0)
    def _():
        for cp in lf_copies(0, 0):
            cp.start()
        for ahead in range(KV_SLOTS - 1):
            for cp in kv_copies(ahead // n_chunks, ahead % n_chunks, ahead):
                cp.start()
        pm[...] = jnp.zeros_like(pm)
        pacc[...] = jnp.zeros_like(pacc)
        pqa[...] = jnp.zeros_like(pqa)

    last_c = c == n_chunks - 1
    g_ahead = g + (KV_SLOTS - 1)

    @pl.when(g_ahead < n_steps)
    def _():
        for cp in kv_copies(g_ahead // n_chunks, g_ahead % n_chunks, g_ahead % KV_SLOTS):
            cp.start()

    bslot = b % 2

    @pl.when(jnp.logical_and(c == 0, g + n_chunks < n_steps))
    def _():
        for cp in lf_copies(b + 1, 1 - bslot):
            cp.start()

    @pl.when(c == 0)
    def _():
        pltpu.make_async_copy(lbuf.at[1 - bslot], lbuf.at[bslot], lsem.at[bslot]).wait()
        lf = lbuf[bslot]
        within = _dot3_left(lf, later_ref[...])
        tot = within[:, 0:1] + lf[:, 0:1]
        suf[...] = (within + _dot3_right(lpages_ref[...], jnp.broadcast_to(tot, lf.shape))) * LOG2E
        rl = _iota((LANES, LANES), 0)
        cl = _iota((LANES, LANES), 1)
        cn_s[...] = _dot3_left(lft_ref[...], (rl <= cl).astype(BF16)) * LOG2E
        m_s[...] = jnp.full(m_s.shape, -jnp.inf, F32)
        l_s[...] = jnp.zeros(l_s.shape, F32)
        acc_s[...] = jnp.zeros(acc_s.shape, F32)

    cn = cn_s[...]
    cn_col = jnp.concatenate([cn[:, t:t + 1] for t in range(t_new)], axis=0)
    qb = qb_ref[...]

    other = (g + 1) % KV_SLOTS
    pltpu.make_async_copy(kbuf.at[other], kbuf.at[slot], ksem.at[slot]).wait()
    pltpu.make_async_copy(vbuf.at[other], vbuf.at[slot], vsem.at[slot]).wait()

    def online(st, s, pv):
        m = m_s[st]
        m_new = jnp.maximum(m, jnp.max(s, axis=1, keepdims=True))
        alpha = jnp.exp2(m - m_new)
        p = jnp.exp2(s - m_new)
        l_s[st] = alpha * l_s[st] + jnp.sum(p, axis=1, keepdims=True)
        acc_s[st] = alpha * acc_s[st] + pv(p.astype(BF16))
        m_s[st] = m_new

    def sample_chunk():
        hpc = ppc // STREAMS
        wdt = hpc * page
        scores = []
        for st in range(STREAMS):
            kt = kbuf[slot, :, :, pl.ds(st * wdt, wdt)].reshape(fdim, wdt).astype(BF16)
            sfx = jnp.concatenate([suf[pl.ds((c * ppc + st * hpc + j) * heads, heads), :] for j in range(hpc)],
                                  axis=1)
            scores.append(_dot(qb, kt) + jnp.concatenate([sfx] * t_new, axis=0) + cn_col)
        for st in range(STREAMS):
            vt = vbuf[slot, :, :, pl.ds(st * wdt, wdt)].reshape(fdim, wdt).astype(BF16)
            online(st, scores[st], lambda p: _dot_nt(p, vt))

    hp = ph_ref[g]
    pi = pi_ref[g]
    pj = pj_ref[g]
    kind = pv_ref[g]
    tq = q_ref.shape[0]
    first = _iota((1, LANES), 1) < FOX_DIM

    def prompt_pair(masked):
        kt = kt_ref[...]
        vt = vt_ref[...]
        cr = cr_ref[...]
        tk = kt.shape[1]
        half = FOX_DIM
        fresh = pj == 0

        def terms(x):
            hi, mid, lo = _split3(x)
            return hi.astype(F32), mid.astype(F32), lo.astype(F32)

        @pl.when(fresh)
        def _():
            cc = cc_ref[...]
            qf = q_ref[...].astype(F32)
            lane = _iota((tq, LANES), 1)
            for hh in range(2):
                cq = jnp.sum(jnp.where(lane == 2 * hp + hh, cc, 0.0), axis=1, keepdims=True)
                hi, mid, lo = terms(cq)
                e0 = half * (1 - hh)
                ext = jnp.where(lane == e0, hi, jnp.where(lane == e0 + 1, mid, jnp.where(lane == e0 + 2, lo,
                      jnp.where(jnp.logical_and(lane >= e0 + 3, lane < e0 + 6), 1.0, 0.0))))
                own = (lane < half) if hh == 0 else (lane >= half)
                pqa[hh] = jnp.where(own, qf, ext).astype(BF16)

        sub = _iota(cr.shape, 0)
        r16 = _iota((2 * SUBLANES, tk), 0)
        pad = jnp.zeros((half - 2 * SUBLANES, tk), BF16)
        ones_row = jnp.where(r16 == 0, 1.0, 0.0).astype(BF16)
        wide = lambda a: jnp.concatenate([a] * (tk // LANES), axis=1)
        outs = []
        for hh in range(2):
            ck = jnp.sum(jnp.where(sub == 2 * hp + hh, cr, 0.0), axis=0, keepdims=True)
            hi, mid, lo = terms(ck)
            kext = jnp.where(r16 < 3, 1.0, jnp.where(r16 == 3, -hi, jnp.where(r16 == 4, -mid,
                   jnp.where(r16 == 5, -lo, 0.0)))).astype(BF16)
            if hh == 0:
                kta = jnp.concatenate([kt[0:half, :], kext, pad], axis=0)
                vta = jnp.concatenate([vt[0:half, :], ones_row, pad], axis=0)
            else:
                kta = jnp.concatenate([kext, pad, kt[half:, :]], axis=0)
                vta = jnp.concatenate([ones_row, pad, vt[half:, :]], axis=0)
            s = _dot(pqa[hh], kta)
            if masked:
                s = jnp.where(_iota(s.shape, 1) + (pj * tk - pi * tq) <= _iota(s.shape, 0), s, -jnp.inf)
            m = jnp.where(fresh, -jnp.inf, pm[hh])
            acc = jnp.where(fresh, 0.0, pacc[hh])
            m_new = jnp.maximum(m, jnp.max(s, axis=1, keepdims=True))
            p = jnp.exp2(s - wide(m_new)).astype(BF16)
            acc = jnp.exp2(m - m_new) * acc + _dot_nt(p, vta)
            if masked:
                e0 = half * (1 - hh)
                outs.append(acc / acc[:, e0:e0 + 1])
            else:
                pm[hh] = m_new
                pacc[hh] = acc
        if masked:
            op_ref[...] = jnp.where(first, outs[0], outs[1])

    sample_chunk()

    @pl.when(kind == 2)
    def _():
        prompt_pair(True)

    @pl.when(kind == 1)
    def _():
        prompt_pair(False)

    @pl.when(c == min(1, n_chunks - 1))
    def _():
        sy_ref[...] = _ssd_short_chunk(sx_ref, sz_ref, ssm_ref, ssmt_ref, (xp, zp, smp, smtp), stail_ref[...],
                                       sst_ref, sfin_ref, (cw_ref, cb_ref, arow_ref, acol_ref, dsk_ref, ng_ref))

    @pl.when(last_c)
    def _():
        knp[...] = jnp.zeros_like(knp)
        vnp[...] = jnp.zeros_like(vnp)
        knp[0:t_new, :] = kn_ref[...]
        vnp[0:t_new, :] = vn_ref[...]
        kn = knp[...].astype(BF16)
        vn = vnp[...].astype(BF16)
        s2 = _dot_nt(qb, kn) + cn_col - jnp.concatenate([cn] * t_new, axis=0)
        tok = _div_pow2(_iota(s2.shape, 0), heads)
        s2 = jnp.where(_iota(s2.shape, 1) <= tok, s2, -jnp.inf)
        online(0, s2, lambda p: _dot(p, vn))
        m_all = functools.reduce(jnp.maximum, [m_s[st] for st in range(STREAMS)])
        wts = [jnp.exp2(m_s[st] - m_all) for st in range(STREAMS)]
        o = (sum(wts[st] * acc_s[st] for st in range(STREAMS))
             / sum(wts[st] * l_s[st] for st in range(STREAMS)))
        keep = _div_pow2(_iota((heads, fdim), 1), FOX_DIM) == _iota((heads, fdim), 0)
        os_ref[...] = jnp.concatenate(
            [jnp.sum(jnp.where(keep, o[t * heads:(t + 1) * heads, :], 0.0), axis=0, keepdims=True)
             for t in range(t_new)], axis=0)


def _pair_schedule(batch, n_hp, nq, tq, tk, n_steps, n_chunks):
    pairs = []
    for b in range(batch):
        for h in range(n_hp):
            for i in range(nq):
                j_last = ((i + 1) * tq - 1) // tk
                pairs += [(b, h, i, j, 2 if j == j_last else 1) for j in range(j_last + 1)]
    n_idle = n_steps - len(pairs)
    assert n_idle >= 0, "more prompt attention pairs than sample chunks"
    steps = np.arange(n_steps)
    chunk = steps % n_chunks
    seq = steps // n_chunks
    first_chunk = chunk == 0
    n_seq = n_steps // n_chunks
    if n_idle <= n_seq:
        pick = (seq * n_idle) // n_seq != ((seq + 1) * n_idle) // n_seq
        cost = np.where(np.logical_and(first_chunk, pick), 0, 1)
    else:
        cost = np.where(first_chunk, 0, np.where(chunk == n_chunks - 1, 1 + seq % 2, 3))
    idle = np.zeros(n_steps, bool)
    idle[np.argsort(cost, kind="stable")[:n_idle]] = True
    rows, k = [], 0
    for g in range(n_steps):
        if idle[g]:
            rows.append(pairs[min(k, len(pairs) - 1)][:4] + (0,))
        else:
            rows.append(pairs[k])
            k += 1
    return [jnp.asarray(np.array(col, np.int32)) for col in zip(*rows)]


def _fox(page_table, qblk, kn, vn, lft, kc, vc, lc, q, ktb, vtb, ccol, crow4, tq, tk,
         xbc_s, z_s, sm_s, smt4_s, st0, tail0, p):
    cdim = xbc_s.shape[2]
    hw = z_s.shape[2]
    bs, rows, fdim = qblk.shape
    t_new = rows // FORGET_HEADS
    n_pages = page_table.shape[1]
    page = kc.shape[-1]
    n_chunks = n_pages // PAGES_PER_CHUNK
    pc = PAGES_PER_CHUNK * page
    n_steps = bs * n_chunks
    bp, _, length = ktb.shape
    nq = length // tq
    n_hp = fdim // LANES
    sched = _pair_schedule(bp, n_hp, nq, tq, tk, n_steps, n_chunks)
    pos = np.arange(page)
    later = jnp.asarray(pos[:, None] > pos[None, :], BF16)
    rows_ph = np.arange(n_pages * FORGET_HEADS)
    lpages = jnp.asarray((rows_ph[None, :] % FORGET_HEADS == rows_ph[:, None] % FORGET_HEADS)
                         & (rows_ph[None, :] // FORGET_HEADS > rows_ph[:, None] // FORGET_HEADS), BF16)
    per_s = lambda g, *_: (g // n_chunks, 0, 0)
    grid_spec = pltpu.PrefetchScalarGridSpec(
        num_scalar_prefetch=6,
        grid=(n_steps,),
        in_specs=[
            pl.BlockSpec((None, rows, fdim), per_s),
            pl.BlockSpec((None, t_new, fdim), per_s),
            pl.BlockSpec((None, t_new, fdim), per_s),
            pl.BlockSpec((None, FORGET_HEADS, LANES), per_s),
            pl.BlockSpec(later.shape, lambda g, *_: (0, 0)),
            pl.BlockSpec(lpages.shape, lambda g, *_: (0, 0)),
            pl.BlockSpec(memory_space=pl.ANY),
            pl.BlockSpec(memory_space=pl.ANY),
            pl.BlockSpec(memory_space=pl.ANY),
            pl.BlockSpec((tq, LANES), lambda g, pt, pb, ph, pi, pj, pv: (pb[g] * nq + pi[g], ph[g])),
            pl.BlockSpec((None, LANES, tk), lambda g, pt, pb, ph, pi, pj, pv: (pb[g], ph[g], pj[g])),
            pl.BlockSpec((None, LANES, tk), lambda g, pt, pb, ph, pi, pj, pv: (pb[g], ph[g], pj[g])),
            pl.BlockSpec((tq, LANES), lambda g, pt, pb, ph, pi, pj, pv: (pb[g] * nq + pi[g], 0)),
            pl.BlockSpec((None, None, 2 * SUBLANES, tk), lambda g, pt, pb, ph, pi, pj, pv: (pb[g], pj[g], 0, 0)),
            pl.BlockSpec((None, t_new, cdim), per_s),
            pl.BlockSpec((None, t_new, hw), per_s),
            pl.BlockSpec((None, t_new, LANES), per_s),
            pl.BlockSpec((None, None, 2 * SUBLANES, t_new), lambda g, *_: (g // n_chunks, 0, 0, 0)),
            pl.BlockSpec((None, hw, SSD_STATE), per_s),
            pl.BlockSpec((None, SUBLANES, cdim), per_s),
        ] + _ssd_const_specs(cdim, hw),
        out_specs=[
            pl.BlockSpec((None, t_new, fdim), per_s),
            pl.BlockSpec((tq, LANES), lambda g, pt, pb, ph, pi, pj, pv: (pb[g] * nq + pi[g], ph[g])),
            pl.BlockSpec((None, t_new, hw), per_s),
            pl.BlockSpec((None, hw, SSD_STATE), per_s),
        ],
        scratch_shapes=[
            pltpu.VMEM((KV_SLOTS, FORGET_HEADS, FOX_DIM, pc), F32),
            pltpu.VMEM((KV_SLOTS, FORGET_HEADS, FOX_DIM, pc), F32),
            pltpu.VMEM((2, n_pages * FORGET_HEADS, page), F32),
            pltpu.VMEM((n_pages * FORGET_HEADS, page), F32),
            pltpu.VMEM((STREAMS, rows, 1), F32),
            pltpu.VMEM((STREAMS, rows, 1), F32),
            pltpu.VMEM((STREAMS, rows, fdim), F32),
            pltpu.VMEM((FORGET_HEADS, LANES), F32),
            pltpu.VMEM((LANES, fdim), F32),
            pltpu.VMEM((LANES, fdim), F32),
            pltpu.VMEM((2, tq, LANES), F32),
            pltpu.VMEM((2, tq, LANES), F32),
            pltpu.VMEM((2, tq, LANES), BF16),
        ] + _ssd_pad_scratch(cdim, hw) + [
            pltpu.SemaphoreType.DMA((KV_SLOTS,)),
            pltpu.SemaphoreType.DMA((KV_SLOTS,)),
            pltpu.SemaphoreType.DMA((2,)),
        ],
    )
    return pl.pallas_call(
        functools.partial(_fox_kernel, n_chunks=n_chunks, t_new=t_new),
        grid_spec=grid_spec,
        out_shape=[jax.ShapeDtypeStruct((bs, t_new, fdim), F32),
                   jax.ShapeDtypeStruct((bp * length, fdim), F32),
                   jax.ShapeDtypeStruct((bs, t_new, hw), F32),
                   jax.ShapeDtypeStruct((bs, hw, SSD_STATE), F32)],
        compiler_params=_cparams(("arbitrary",)),
        name="fox",
    )(page_table, *sched, qblk, kn, vn, lft, later, lpages, kc, vc, lc, q, ktb, vtb, ccol, crow4,
      xbc_s, z_s, sm_s, smt4_s, st0, tail0, *_ssd_consts(p))


def _ssd_chunk(x, z, sm, smt, prev, state_in, state_out, cw_ref, cb_ref, arow_ref, acol_ref, dsk_ref, ng_ref):
    q = CHUNK
    hw = SSD_HEADS * SSD_DIM
    gs = SSD_STATE

    row8 = _iota(prev.shape, 0)
    conv = cb_ref[...] + cw_ref[SSD_CONV - 1:SSD_CONV, :] * x
    for sh in range(1, SSD_CONV):
        rx = pltpu.roll(x, sh, 0)
        fix = jnp.where(row8 < sh, pltpu.roll(prev, sh, 0), rx[0:SUBLANES, :])
        xs_sh = jnp.concatenate([fix, rx[SUBLANES:, :]], axis=0)
        conv = conv + cw_ref[SSD_CONV - 1 - sh:SSD_CONV - sh, :] * xs_sh
    xc = _silu(conv)
    xs = xc[:, 0:hw]

    r = _iota((q, q), 0)
    cidx = _iota((q, q), 1)
    tril = cidx <= r
    lower = tril.astype(BF16)
    upper = (r <= cidx).astype(BF16)
    acum_c = _dot3_right(lower, sm * arow_ref[...])
    acum_r = _dot3_left(smt * acol_ref[...], upper)
    lane = _iota((1, LANES), 1)
    first = lane < SSD_DIM
    rowp = _iota((LANES, 1), 0) < SSD_DIM

    ys = []
    for pr in range(SSD_HEADS // 2):
        g = pr // (SSD_HEADS // 4)
        bg = xc[:, hw + g * gs: hw + (g + 1) * gs].astype(BF16)
        cg = xc[:, hw + 2 * gs + g * gs: hw + 2 * gs + (g + 1) * gs].astype(BF16)
        gmat = _dot_nt(cg, bg)
        xpair = xs[:, pr * LANES:(pr + 1) * LANES]
        xpb = xpair.astype(BF16)
        yd, te, ea, cd = [], [], [], []
        for hh in range(2):
            h = FORGET_HEADS + 2 * pr + hh
            ac = acum_c[:, h:h + 1]
            ar = acum_r[h:h + 1, :]
            dtr = smt[h:h + 1, :]
            dtc = sm[:, h:h + 1]
            last = acum_c[q - 1:q, h:h + 1]
            decay = jnp.exp(jnp.where(tril, ac - ar, -jnp.inf))
            sc = gmat * decay * dtr
            yd.append(_dot(sc.astype(BF16), xpb))
            te.append(jnp.exp(last - ac) * dtc)
            ea.append(jnp.exp(ac))
            cd.append(jnp.exp(last))
        ydiag = jnp.where(first, yd[0], yd[1])
        xsc = (xpair * jnp.where(first, te[0], te[1])).astype(BF16)
        cstate = _dot_tn(xsc, bg)
        prev_st = state_in[pr * LANES:(pr + 1) * LANES, :]
        yoff = _dot_nt(cg, prev_st.astype(BF16)) * jnp.where(first, ea[0], ea[1])
        state_out[pr * LANES:(pr + 1) * LANES, :] = prev_st * jnp.where(rowp, cd[0], cd[1]) + cstate
        ys.append(ydiag + yoff + dsk_ref[:, pr * LANES:(pr + 1) * LANES] * xpair)
    y = jnp.concatenate(ys, axis=1) * _silu(z)
    return _rms(y, ng_ref[...])


def _ssd_short_chunk(xbc_ref, z_ref, sm_ref, smt_ref, pads, prev, state_in, state_out, consts):
    t_real = xbc_ref.shape[0]
    xp, zp, smp, smtp = pads
    xp[...] = jnp.zeros_like(xp)
    zp[...] = jnp.zeros_like(zp)
    smp[...] = jnp.zeros_like(smp)
    smtp[...] = jnp.zeros_like(smtp)
    xp[0:t_real, :] = xbc_ref[...]
    zp[0:t_real, :] = z_ref[...]
    smp[0:t_real, :] = sm_ref[...]
    smtp[:, 0:t_real] = smt_ref[...]
    y = _ssd_chunk(xp[...], zp[...], smp[...], smtp[...], prev, state_in, state_out, *consts)
    return y[0:t_real, :]


def _ssd_pad_scratch(cdim, hw):
    return [pltpu.VMEM((CHUNK, cdim), F32), pltpu.VMEM((CHUNK, hw), F32),
            pltpu.VMEM((CHUNK, LANES), F32), pltpu.VMEM((2 * SUBLANES, CHUNK), F32)]


def _ssd_const_specs(cdim, hw):
    return [_const_spec((SSD_CONV, cdim)), _const_spec((1, cdim)), _const_spec((1, LANES)),
            _const_spec((2 * SUBLANES, 1)), _const_spec((1, hw)), _const_spec((1, hw))]


def _ssd_consts(p):
    return (p['ssd_conv_w'], p['ssd_conv_b'], p['a_row'], p['a_col'], p['d_row'], p['ssd_norm_g'])


def _ssd_kernel(xbc_ref, z_ref, sm_ref, smt_ref, cw_ref, cb_ref, arow_ref, acol_ref, dsk_ref, ng_ref,
                y_ref, fin_ref, state, tail):
    c = pl.program_id(1)

    @pl.when(c == 0)
    def _():
        state[...] = jnp.zeros_like(state)
        tail[...] = jnp.zeros_like(tail)

    x = xbc_ref[...]
    y_ref[...] = _ssd_chunk(x, z_ref[...], sm_ref[...], smt_ref[...], tail[...], state, state,
                            cw_ref, cb_ref, arow_ref, acol_ref, dsk_ref, ng_ref)
    tail[...] = x[CHUNK - SUBLANES:, :]

    @pl.when(c == pl.num_programs(1) - 1)
    def _():
        fin_ref[...] = state[...]


def _ssd(xbc, z, sm, smt4, p):
    batch, length, cdim = xbc.shape
    hw = z.shape[2]
    nc = length // CHUNK
    blk = lambda b, c: (b, c, 0)
    return pl.pallas_call(
        _ssd_kernel,
        grid=(batch, nc),
        in_specs=[
            pl.BlockSpec((None, CHUNK, cdim), blk),
            pl.BlockSpec((None, CHUNK, hw), blk),
            pl.BlockSpec((None, CHUNK, LANES), blk),
            pl.BlockSpec((None, None, 2 * SUBLANES, CHUNK), lambda b, c: (b, c, 0, 0)),
        ] + _ssd_const_specs(cdim, hw),
        out_specs=[pl.BlockSpec((None, CHUNK, hw), blk),
                   pl.BlockSpec((None, hw, SSD_STATE), lambda b, c: (b, 0, 0))],
        out_shape=[jax.ShapeDtypeStruct((batch, length, hw), F32),
                   jax.ShapeDtypeStruct((batch, hw, SSD_STATE), F32)],
        scratch_shapes=[pltpu.VMEM((hw, SSD_STATE), F32), pltpu.VMEM((SUBLANES, cdim), F32)],
        compiler_params=_cparams(("parallel", "arbitrary")),
        name="ssd",
    )(xbc, z, sm, smt4, *_ssd_consts(p))


def _mem_kv_kernel(m_ref, g_ref, w_ref, gk_ref, k_ref, v_ref):
    mw = MEM_HEADS * MEM_DIM
    xn = _rms(m_ref[...], g_ref[...]).astype(BF16)
    kv = _dot(xn, w_ref[...])
    for h in range(MEM_HEADS):
        sl = slice(h * MEM_DIM, (h + 1) * MEM_DIM)
        k_ref[:, sl] = _rms(kv[:, sl], gk_ref[...])
    v_ref[...] = kv[:, mw:]


def _mem_kv(mem, p, tm):
    n, d = mem.shape
    mw = MEM_HEADS * MEM_DIM
    row = lambda i: (i, 0)
    return pl.pallas_call(
        _mem_kv_kernel,
        grid=(n // tm,),
        in_specs=[pl.BlockSpec((tm, d), row), _const_spec((1, d)), _const_spec((d, 2 * mw)),
                  _const_spec((1, MEM_DIM))],
        out_specs=[pl.BlockSpec((tm, mw), row), pl.BlockSpec((tm, mw), row)],
        out_shape=[jax.ShapeDtypeStruct((n, mw), F32), jax.ShapeDtypeStruct((n, mw), F32)],
        compiler_params=_cparams(("parallel",)),
        name="mem_kv",
    )(mem, p['mem_in_norm_g'], p['w_mem_kv'], p['mem_k_norm_g'])


def _mix_out(x_ref, fo_ref, so_ref, wout_ref, g_ref, wq_ref, gq_ref):
    fw = fo_ref.shape[-1]
    y1 = x_ref[...] + _dot(fo_ref[...].astype(BF16), wout_ref[0:fw, :]) + _dot(so_ref[...].astype(BF16), wout_ref[fw:, :])
    xn = _rms(y1, g_ref[...]).astype(BF16)
    q = _dot(xn, wq_ref[...])
    qs = []
    for h in range(MEM_HEADS):
        qh = _rms(q[:, h * MEM_DIM:(h + 1) * MEM_DIM], gq_ref[...]) * MEM_DIM ** -0.5
        qs.append(qh)
    return y1, qs


def _memattn_prompt_kernel(x_ref, fo_ref, so_ref, wout_ref, g_ref, wq_ref, gq_ref, mk_ref, mv_ref, wo_ref, o_ref):
    y1, qs = _mix_out(x_ref, fo_ref, so_ref, wout_ref, g_ref, wq_ref, gq_ref)
    outs = []
    for h in range(MEM_HEADS):
        sl = slice(h * MEM_DIM, (h + 1) * MEM_DIM)
        s = _dot_nt(qs[h].astype(BF16), mk_ref[:, sl].astype(BF16))
        p = jnp.exp(s - jnp.max(s, axis=1, keepdims=True))
        o = _dot(p.astype(BF16), mv_ref[:, sl].astype(BF16)) / jnp.sum(p, axis=1, keepdims=True)
        outs.append(o.astype(BF16))
    o_ref[...] = y1 + _dot(jnp.concatenate(outs, axis=1), wo_ref[...])


def _memattn_prompt(x, fo, so, mk, mv, p, tm, length, n_mem):
    n, d = x.shape
    fw = fo.shape[1]
    mw = MEM_HEADS * MEM_DIM
    per = length // tm
    row = lambda i: (i, 0)
    return pl.pallas_call(
        _memattn_prompt_kernel,
        grid=(n // tm,),
        in_specs=[pl.BlockSpec((tm, d), row), pl.BlockSpec((tm, fw), row), pl.BlockSpec((tm, fw), row),
                  _const_spec((2 * fw, d)), _const_spec((1, d)), _const_spec((d, mw)), _const_spec((1, MEM_DIM)),
                  pl.BlockSpec((n_mem, mw), lambda i: (i // per, 0)),
                  pl.BlockSpec((n_mem, mw), lambda i: (i // per, 0)),
                  _const_spec((mw, d))],
        out_specs=pl.BlockSpec((tm, d), row),
        out_shape=jax.ShapeDtypeStruct((n, d), F32),
        compiler_params=_cparams(("parallel",)),
        name="memattn_prompt",
    )(x, fo, so, p['w_out'], p['norm_mem_g'], p['w_mem_q'], p['mem_q_norm_g'], mk, mv, p['w_mem_o'])


def _memattn_sample_kernel(x_ref, fo_ref, so_ref, wout_ref, g_ref, wq_ref, gq_ref, mk_ref, mv_ref, wo_ref, o_ref,
                           *, t_new):
    y1, qs = _mix_out(x_ref, fo_ref, so_ref, wout_ref, g_ref, wq_ref, gq_ref)
    tm = y1.shape[0]
    per = SUBLANES // t_new
    nrow = MEM_HEADS * SUBLANES
    ncol = mk_ref.shape[1]
    row = _iota((nrow, 1), 0)
    mine = [_div_pow2(lax.bitwise_and(row, jnp.int32(SUBLANES - 1)), t_new) == u for u in range(per)]
    same_head = (lax.bitwise_and(_iota((nrow, ncol), 1), jnp.int32(MEM_HEADS - 1))
                 == _div_pow2(_iota((nrow, ncol), 0), SUBLANES))
    tiles = []
    for r in range(tm // SUBLANES):
        qst = jnp.concatenate([qs[h][r * SUBLANES:(r + 1) * SUBLANES, :] for h in range(MEM_HEADS)], axis=0)
        qst = qst.astype(BF16)
        s = None
        for u in range(per):
            su = _dot_nt(qst, mk_ref[r * per + u].astype(BF16))
            s = su if s is None else jnp.where(mine[u], su, s)
        s = jnp.where(same_head, s, -jnp.inf)
        p = jnp.exp(s - jnp.max(s, axis=1, keepdims=True))
        pb = p.astype(BF16)
        o = None
        for u in range(per):
            ou = _dot(pb, mv_ref[r * per + u].astype(BF16))
            o = ou if o is None else jnp.where(mine[u], ou, o)
        o = o / jnp.sum(p, axis=1, keepdims=True)
        tiles.append(jnp.concatenate([o[h * SUBLANES:(h + 1) * SUBLANES, :] for h in range(MEM_HEADS)], axis=1))
    o_ref[...] = y1 + _dot(jnp.concatenate(tiles, axis=0).astype(BF16), wo_ref[...])


def _memattn_sample(x, fo, so, mk, mv, p, tm, t_new):
    n, d = x.shape
    fw = fo.shape[1]
    mw = MEM_HEADS * MEM_DIM
    n_flat = mk.shape[1]
    bb = tm // t_new
    row = lambda i: (i, 0)
    return pl.pallas_call(
        functools.partial(_memattn_sample_kernel, t_new=t_new),
        grid=(n // tm,),
        in_specs=[pl.BlockSpec((tm, d), row), pl.BlockSpec((tm, fw), row), pl.BlockSpec((tm, fw), row),
                  _const_spec((2 * fw, d)), _const_spec((1, d)), _const_spec((d, mw)), _const_spec((1, MEM_DIM)),
                  pl.BlockSpec((bb, n_flat, MEM_DIM), lambda i: (i, 0, 0)),
                  pl.BlockSpec((bb, n_flat, MEM_DIM), lambda i: (i, 0, 0)),
                  _const_spec((mw, d))],
        out_specs=pl.BlockSpec((tm, d), row),
        out_shape=jax.ShapeDtypeStruct((n, d), F32),
        compiler_params=_cparams(("parallel",)),
        name="memattn_sample",
    )(x, fo, so, p['w_out'], p['norm_mem_g'], p['w_mem_q'], p['mem_q_norm_g'], mk, mv, p['w_mem_o'])


FF_CHUNK = 1024


def _ff_chunks(dff):
    return [(s, min(s + FF_CHUNK, dff)) for s in range(0, dff, FF_CHUNK)]


def _ffn_prompt_kernel(x_ref, g_ref, wg_ref, wu_ref, cw_ref, cb_ref, wd_ref, o_ref, tail_ref, halo, *, per):
    i = pl.program_id(0)
    tm = x_ref.shape[0]
    dff = wg_ref.shape[1]

    @pl.when(i % per == 0)
    def _():
        halo[...] = jnp.zeros_like(halo)

    x = x_ref[...]
    xn = _rms(x, g_ref[...]).astype(BF16)
    row8 = _iota((SUBLANES, 1), 0)
    acc = x
    for lo, hi in _ff_chunks(dff):
        gate = _dot(xn, wg_ref[:, lo:hi])
        up = _dot(xn, wu_ref[:, lo:hi])
        prev = halo[:, lo:hi]
        conv = cb_ref[:, lo:hi] + cw_ref[FFN_CONV - 1:FFN_CONV, lo:hi] * gate
        for sh in range(1, FFN_CONV):
            rg = pltpu.roll(gate, sh, 0)
            fix = jnp.where(row8 < sh, pltpu.roll(prev, sh, 0), rg[0:SUBLANES, :])
            conv = conv + cw_ref[FFN_CONV - 1 - sh:FFN_CONV - sh, lo:hi] * jnp.concatenate([fix, rg[SUBLANES:, :]], axis=0)
        halo[:, lo:hi] = gate[tm - SUBLANES:tm, :]
        hmid = (_silu(conv) * up).astype(BF16)
        acc = acc + _dot(hmid, wd_ref[lo:hi, :])
    o_ref[...] = acc
    tail_ref[...] = halo[...]


def _ffn_prompt(x, p, tm, batch, length):
    n, d = x.shape
    dff = p['w_ffn_gate'].shape[1]
    per = length // tm
    row = lambda i: (i, 0)
    return pl.pallas_call(
        functools.partial(_ffn_prompt_kernel, per=per),
        grid=(n // tm,),
        in_specs=[pl.BlockSpec((tm, d), row), _const_spec((1, d)),
                  pl.BlockSpec((d, dff), lambda i: (0, 0), pipeline_mode=pl.Buffered(1)),
                  pl.BlockSpec((d, dff), lambda i: (0, 0), pipeline_mode=pl.Buffered(1)),
                  _const_spec((FFN_CONV, dff)), _const_spec((1, dff)),
                  pl.BlockSpec((dff, d), lambda i: (0, 0), pipeline_mode=pl.Buffered(1))],
        out_specs=[pl.BlockSpec((tm, d), row),
                   pl.BlockSpec((None, SUBLANES, dff), lambda i: (i // per, 0, 0))],
        out_shape=[jax.ShapeDtypeStruct((n, d), F32),
                   jax.ShapeDtypeStruct((batch, SUBLANES, dff), F32)],
        scratch_shapes=[pltpu.VMEM((SUBLANES, dff), F32)],
        compiler_params=_cparams(("arbitrary",)),
        name="ffn_prompt",
    )(x, p['norm_ffn_g'], p['w_ffn_gate'], p['w_ffn_up'], p['ffn_conv_w'], p['ffn_conv_b'], p['w_ffn_down'])


def _ffn_sample_kernel(x_ref, buf_ref, g_ref, wg_ref, wu_ref, cw_ref, cb_ref, wd_ref, o_ref, nbuf_ref, *, t_new):
    nb = x_ref.shape[0] // t_new
    dff = wg_ref.shape[1]
    x = x_ref[...]
    xn = _rms(x, g_ref[...]).astype(BF16)
    acc = x
    for lo, hi in _ff_chunks(dff):
        gate = _dot(xn, wg_ref[:, lo:hi])
        up = _dot(xn, wu_ref[:, lo:hi])
        ext = jnp.concatenate([buf_ref[:, lo:hi], gate], axis=0)
        conv = cb_ref[:, lo:hi]
        for j in range(FFN_CONV):
            conv = conv + cw_ref[j:j + 1, lo:hi] * ext[j * nb:(j + t_new) * nb, :]
        nbuf_ref[:, lo:hi] = ext[t_new * nb:, :]
        hmid = (_silu(conv) * up).astype(BF16)
        acc = acc + _dot(hmid, wd_ref[lo:hi, :])
    o_ref[...] = acc


def _ffn_sample(x_tm, buf_tm, p, t_new):
    n, d = x_tm.shape
    dff = p['w_ffn_gate'].shape[1]
    nbr = buf_tm.shape[0]
    return pl.pallas_call(
        functools.partial(_ffn_sample_kernel, t_new=t_new),
        grid=(1,),
        in_specs=[_const_spec((n, d)), _const_spec((nbr, dff)), _const_spec((1, d)),
                  pl.BlockSpec((d, dff), lambda i: (0, 0), pipeline_mode=pl.Buffered(1)),
                  pl.BlockSpec((d, dff), lambda i: (0, 0), pipeline_mode=pl.Buffered(1)),
                  _const_spec((FFN_CONV, dff)), _const_spec((1, dff)),
                  pl.BlockSpec((dff, d), lambda i: (0, 0), pipeline_mode=pl.Buffered(1))],
        out_specs=[_const_spec((n, d)), _const_spec((nbr, dff))],
        out_shape=[jax.ShapeDtypeStruct((n, d), F32), jax.ShapeDtypeStruct((nbr, dff), F32)],
        compiler_params=_cparams(("arbitrary",)),
        name="ffn_sample",
    )(x_tm, buf_tm, p['norm_ffn_g'], p['w_ffn_gate'], p['w_ffn_up'], p['ffn_conv_w'], p['ffn_conv_b'],
      p['w_ffn_down'])


def _prep_layer(l, norm_mix_g, w_in, b_forget, fox_q_norm_g, fox_k_norm_g, ssd_conv_w, ssd_conv_b, ssd_dt_bias,
                ssd_a_log, ssd_d, ssd_norm_g, w_out, norm_mem_g, mem_in_norm_g, w_mem_q, w_mem_kv, mem_q_norm_g,
                mem_k_norm_g, w_mem_o, norm_ffn_g, w_ffn_gate, w_ffn_up, ffn_conv_w, ffn_conv_b, w_ffn_down):
    fw = FORGET_HEADS * FOX_DIM
    hw = SSD_HEADS * SSD_DIM
    w = w_in[l]
    o_f = 3 * fw
    o_z = o_f + FORGET_HEADS
    o_x = o_z + hw
    o_dt = w.shape[1] - SSD_HEADS
    w_small = jnp.concatenate([w[:, o_f:o_z], w[:, o_dt:]], axis=1)
    b_small = jnp.concatenate([b_forget[l], ssd_dt_bias[l]]).astype(F32)
    a_neg = -jnp.exp(ssd_a_log[l].astype(F32))
    a16 = jnp.concatenate([jnp.zeros((FORGET_HEADS,), F32), a_neg])
    head_of = jnp.arange(fw) // FOX_DIM
    onehot = (head_of[:, None] == jnp.arange(LANES)[None, :]).astype(F32)
    row = lambda v: v.reshape(1, -1).astype(F32)
    return {
        'norm_mix_g': row(norm_mix_g[l]),
        'w_q': w[:, :fw].astype(BF16),
        'w_kvt': w[:, fw:o_f].T.astype(BF16),
        'w_zx': w[:, o_z:o_dt].astype(BF16),
        'w_sm': jnp.pad(w_small, ((0, 0), (0, LANES - w_small.shape[1]))).astype(BF16),
        'w_smt': w_small.T.astype(BF16),
        'b_sm': jnp.pad(b_small, (0, LANES - b_small.shape[0])).reshape(1, LANES),
        'b_smt': b_small.reshape(-1, 1),
        'gq': row(jnp.tile(fox_q_norm_g[l], FORGET_HEADS)),
        'gk_col': jnp.tile(fox_k_norm_g[l], FORGET_HEADS).reshape(-1, 1).astype(F32),
        'head_reduce': (onehot / FOX_DIM).astype(BF16),
        'head_expand': onehot.T.astype(BF16),
        'ssd_conv_w': ssd_conv_w[l].astype(F32),
        'ssd_conv_b': row(ssd_conv_b[l]),
        'a_row': jnp.pad(a16, (0, LANES - a16.shape[0])).reshape(1, LANES),
        'a_col': a16.reshape(-1, 1),
        'd_row': row(jnp.repeat(ssd_d[l], SSD_DIM)),
        'ssd_norm_g': row(ssd_norm_g[l]),
        'w_out': w_out[l].astype(BF16),
        'norm_mem_g': row(norm_mem_g[l]),
        'mem_in_norm_g': row(mem_in_norm_g[l]),
        'w_mem_q': w_mem_q[l].astype(BF16),
        'w_mem_kv': w_mem_kv[l].astype(BF16),
        'mem_q_norm_g': row(mem_q_norm_g[l]),
        'mem_k_norm_g': row(mem_k_norm_g[l]),
        'w_mem_o': w_mem_o[l].astype(BF16),
        'norm_ffn_g': row(norm_ffn_g[l]),
        'w_ffn_gate': w_ffn_gate[l].astype(BF16),
        'w_ffn_up': w_ffn_up[l].astype(BF16),
        'ffn_conv_w': ffn_conv_w[l].astype(F32),
        'ffn_conv_b': row(ffn_conv_b[l]),
        'w_ffn_down': w_ffn_down[l].astype(BF16),
    }


def _row_tile(n, want):
    t = min(n, want)
    while n % t:
        t //= 2
    return t


def _layer(xp, mem, xs, kc, vc, lc, st_ssm, st_conv, st_ffn, mem_k, mem_v, page_table, p):
    fw = FORGET_HEADS * FOX_DIM
    hw = SSD_HEADS * SSD_DIM
    heads = lambda a, b_, t_: a.reshape(b_, FORGET_HEADS, FOX_DIM, t_).transpose(0, 3, 1, 2)

    bp, length, d = xp.shape
    n_p = bp * length
    n_mem = mem.shape[1]
    tm = _row_tile(length, 512)
    xpf = xp.reshape(n_p, d)
    q_p, kt_p, vt_p, ktb_p, vtb_p, z_p, xbc_p, sm_p, smt_p = _inproj(xpf, p, tm, bp, length)
    bs, t_new, _ = xs.shape
    n_s = bs * t_new
    xsf = xs.reshape(n_s, d)
    q_s, kt_s, vt_s, _, _, z_s, xbc_s, sm_s, smt_s = _inproj(xsf, p, _row_tile(n_s, 512), 1, n_s)

    ccol, crow = _cumsum(sm_p, smt_p, bp, length)
    tq = _row_tile(length, 512)
    nq = length // tq
    tk = _row_tile(length, 1024)
    crow4 = crow.reshape(2 * SUBLANES, bp, length // tk, tk).transpose(1, 2, 0, 3)
    k_s = kt_s[0].T.reshape(bs, t_new, fw)
    v_s = vt_s[0].T.reshape(bs, t_new, fw)
    own = (jnp.arange(fw) // FOX_DIM)[None, :] == jnp.arange(FORGET_HEADS)[:, None]
    qblk = jnp.where(own[None, None], q_s.reshape(bs, t_new, 1, fw), jnp.zeros((), q_s.dtype))
    qblk = qblk.reshape(bs, t_new * FORGET_HEADS, fw)
    lft = jnp.pad(smt_s[:FORGET_HEADS].reshape(FORGET_HEADS, bs, t_new).transpose(1, 0, 2),
                  ((0, 0), (0, 0), (0, LANES - t_new)))
    cdim = xbc_p.shape[1]
    xbc3_s = xbc_s.reshape(bs, t_new, cdim)
    smt4_s = smt_s.reshape(2 * SUBLANES, bs, 1, t_new).transpose(1, 2, 0, 3)
    tail0 = jnp.pad(st_conv, ((0, 0), (SUBLANES - (SSD_CONV - 1), 0), (0, 0)))
    fox_s, fox_p, ssd_s, ssm_new = _fox(
        page_table, qblk, k_s, v_s, lft, kc, vc, lc, q_p, ktb_p, vtb_p, ccol, crow4, tq, tk,
        xbc3_s, z_s.reshape(bs, t_new, hw), sm_s.reshape(bs, t_new, LANES), smt4_s,
        st_ssm.reshape(bs, hw, SSD_STATE), tail0, p)

    nc = length // CHUNK
    smt4_p = smt_p.reshape(2 * SUBLANES, bp, nc, CHUNK).transpose(1, 2, 0, 3)
    xbc3_p = xbc_p.reshape(bp, length, cdim)
    ssd_p, ssm_fin = _ssd(xbc3_p, z_p.reshape(bp, length, hw), sm_p.reshape(bp, length, LANES), smt4_p, p)

    mk, mv = _mem_kv(mem.reshape(bp * n_mem, d), p, _row_tile(n_mem, 512))
    y2_p = _memattn_prompt(xpf, fox_p, ssd_p.reshape(n_p, hw), mk, mv, p, tm, length, n_mem)
    flat = lambda a: a.reshape(a.shape[0], a.shape[1] * a.shape[2], a.shape[3])
    y2_s = _memattn_sample(xsf, fox_s.reshape(n_s, fw), ssd_s.reshape(n_s, hw), flat(mem_k), flat(mem_v), p,
                           _row_tile(n_s, 32), t_new)

    y3_p, ffn_tail = _ffn_prompt(y2_p, p, tm, bp, length)
    x_tm = y2_s.reshape(bs, t_new, d).transpose(1, 0, 2).reshape(n_s, d)
    dff = st_ffn.shape[-1]
    buf_tm = st_ffn.transpose(1, 0, 2).reshape((FFN_CONV - 1) * bs, dff)
    y3_tm, nbuf_tm = _ffn_sample(x_tm, buf_tm, p, t_new)

    prompt = (y3_p.reshape(bp, length, d),
              heads(kt_p, bp, length),
              heads(vt_p, bp, length),
              sm_p.reshape(bp, length, LANES)[:, :, :FORGET_HEADS],
              ssm_fin.reshape(bp, SSD_HEADS, SSD_DIM, SSD_STATE),
              xbc3_p[:, length - (SSD_CONV - 1):, :],
              ffn_tail[:, SUBLANES - (FFN_CONV - 1):, :],
              mk.reshape(bp, n_mem, MEM_HEADS, MEM_DIM),
              mv.reshape(bp, n_mem, MEM_HEADS, MEM_DIM))
    sample = (y3_tm.reshape(t_new, bs, d).transpose(1, 0, 2),
              k_s.reshape(bs, t_new, FORGET_HEADS, FOX_DIM),
              v_s.reshape(bs, t_new, FORGET_HEADS, FOX_DIM),
              sm_s.reshape(bs, t_new, LANES)[:, :, :FORGET_HEADS],
              ssm_new.reshape(bs, SSD_HEADS, SSD_DIM, SSD_STATE),
              jnp.concatenate([st_conv, xbc3_s], axis=1)[:, t_new:, :],
              nbuf_tm.reshape(FFN_CONV - 1, bs, dff).transpose(1, 0, 2))
    return prompt, sample


def kernel(x_prompt, mem_prompt, x_sample, cache_fox_k, cache_fox_v, cache_fox_logf, state_ssm, state_ssm_conv, state_ffn_conv, cache_mem_k, cache_mem_v, page_table, norm_mix_g, w_in, b_forget, fox_q_norm_g, fox_k_norm_g, ssd_conv_w, ssd_conv_b, ssd_dt_bias, ssd_a_log, ssd_d, ssd_norm_g, w_out, norm_mem_g, mem_in_norm_g, w_mem_q, w_mem_kv, mem_q_norm_g, mem_k_norm_g, w_mem_o, norm_ffn_g, w_ffn_gate, w_ffn_up, ffn_conv_w, ffn_conv_b, w_ffn_down):
    depth = w_in.shape[0]
    yp, ys = x_prompt, x_sample
    kc_all = jnp.transpose(cache_fox_k, (0, 1, 3, 4, 2))
    vc_all = jnp.transpose(cache_fox_v, (0, 1, 3, 4, 2))
    lc_all = jnp.transpose(cache_fox_logf, (0, 1, 3, 2))
    pouts, souts = [], []
    for l in range(depth):
        p = _prep_layer(l, norm_mix_g, w_in, b_forget, fox_q_norm_g, fox_k_norm_g, ssd_conv_w, ssd_conv_b,
                        ssd_dt_bias, ssd_a_log, ssd_d, ssd_norm_g, w_out, norm_mem_g, mem_in_norm_g, w_mem_q,
                        w_mem_kv, mem_q_norm_g, mem_k_norm_g, w_mem_o, norm_ffn_g, w_ffn_gate, w_ffn_up,
                        ffn_conv_w, ffn_conv_b, w_ffn_down)
        po, so = _layer(yp, mem_prompt, ys, kc_all[l], vc_all[l], lc_all[l], state_ssm[l], state_ssm_conv[l],
                        state_ffn_conv[l], cache_mem_k[l], cache_mem_v[l], page_table, p)
        yp, ys = po[0], so[0]
        pouts.append(po[1:])
        souts.append(so[1:])
    stack = lambda outs, i: jnp.stack([o[i] for o in outs])
    return (yp, ys) + tuple(stack(pouts, i) for i in range(8)) + tuple(stack(souts, i) for i in range(6))
```

```python
import functools

import numpy as np
import jax
import jax.numpy as jnp
from jax import lax
from jax.experimental import pallas as pl
from jax.experimental.pallas import tpu as pltpu

F32 = jnp.float32
BF16 = jnp.bfloat16
EPS = 1e-6
LOG2E = 1.4426950408889634
FORGET_HEADS = 8
FOX_DIM = 64
SSD_HEADS = 8
SSD_DIM = 64
SSD_STATE = 128
SSD_CONV = 4
CHUNK = 128
MEM_HEADS = 4
MEM_DIM = 128
FFN_CONV = 3
LANES = 128
SUBLANES = 8
VMEM_LIMIT = 56 * 1024 * 1024
NT = (((1,), (1,)), ((), ()))
TN = (((0,), (0,)), ((), ()))


def _dot(a, b):
    return jnp.dot(a, b, preferred_element_type=F32)


def _dot_nt(a, b):
    return lax.dot_general(a, b, NT, preferred_element_type=F32)


def _dot_tn(a, b):
    return lax.dot_general(a, b, TN, preferred_element_type=F32)


def _split3(x):
    hi = x.astype(BF16)
    r1 = x - hi.astype(F32)
    mid = r1.astype(BF16)
    lo = (r1 - mid.astype(F32)).astype(BF16)
    return hi, mid, lo


def _dot3_left(x, m):
    hi, mid, lo = _split3(x)
    return _dot(hi, m) + _dot(mid, m) + _dot(lo, m)


def _dot3_right(m, x):
    hi, mid, lo = _split3(x)
    return _dot(m, hi) + _dot(m, mid) + _dot(m, lo)


def _dot2_left(x, m):
    hi = x.astype(BF16)
    lo = (x - hi.astype(F32)).astype(BF16)
    return _dot(hi, m) + _dot(lo, m)


def _dot2_right(m, x):
    hi = x.astype(BF16)
    lo = (x - hi.astype(F32)).astype(BF16)
    return _dot(m, hi) + _dot(m, lo)


def _rms(x, g):
    return x * lax.rsqrt(jnp.mean(x * x, axis=-1, keepdims=True) + EPS) * g


def _softplus(x):
    return jnp.maximum(x, 0.0) + jnp.log1p(jnp.exp(-jnp.abs(x)))


def _log_sigmoid(x):
    return -_softplus(-x)


def _silu(x):
    return x * (1.0 / (1.0 + jnp.exp(-x)))


def _iota(shape, dim):
    return lax.broadcasted_iota(jnp.int32, shape, dim)


def _div_pow2(x, d):
    assert d & (d - 1) == 0
    return lax.shift_right_logical(x, jnp.int32(d.bit_length() - 1))


def _cparams(sem):
    return pltpu.CompilerParams(dimension_semantics=sem, vmem_limit_bytes=VMEM_LIMIT)


def _const_spec(shape):
    n = len(shape)
    return pl.BlockSpec(shape, lambda *_: (0,) * n)


def _inproj_kernel(x_ref, g_ref, wq_ref, wkvt_ref, wzx_ref, wsm_ref, wsmt_ref, gq_ref, gkc_ref, red_ref, exp_ref,
                   bsm_ref, bsmt_ref,
                   q_ref, kt_ref, vt_ref, ktb_ref, vtb_ref, z_ref, xbc_ref, sm_ref, smt_ref):
    fw = FORGET_HEADS * FOX_DIM
    tm = x_ref.shape[0]
    xn = _rms(x_ref[...], g_ref[...]).astype(BF16)

    q = _dot(xn, wq_ref[...])
    rs = lax.rsqrt(_dot2_left(q * q, red_ref[...]) + EPS)
    q_ref[...] = (q * _dot2_left(rs, exp_ref[...]) * gq_ref[...] * (FOX_DIM ** -0.5 * LOG2E)).astype(BF16)

    kv = _dot_nt(wkvt_ref[...], xn)
    k3 = kv[0:fw, :].reshape(FORGET_HEADS, FOX_DIM, tm)
    k3 = k3 * lax.rsqrt(jnp.mean(k3 * k3, axis=1, keepdims=True) + EPS)
    kn = k3.reshape(fw, tm) * gkc_ref[...]
    kt_ref[...] = kn
    ktb_ref[...] = kn.astype(BF16)
    vt = kv[fw:, :]
    vt_ref[...] = vt
    vtb_ref[...] = vt.astype(BF16)

    z_ref[...] = _dot(xn, wzx_ref[:, 0:fw])
    xbc_ref[...] = _dot(xn, wzx_ref[:, fw:])
    sm = _dot(xn, wsm_ref[...]) + bsm_ref[...]
    sm_ref[...] = jnp.where(_iota(sm.shape, 1) < FORGET_HEADS, _log_sigmoid(sm), _softplus(sm))
    smt = _dot_nt(wsmt_ref[...], xn) + bsmt_ref[...]
    smt_ref[...] = jnp.where(_iota(smt.shape, 0) < FORGET_HEADS, _log_sigmoid(smt), _softplus(smt))


def _inproj(x, p, tm, batch, length):
    n, d = x.shape
    fw = FORGET_HEADS * FOX_DIM
    nzx = p['w_zx'].shape[1]
    per = length // tm
    row = lambda i: (i, 0)
    tr = lambda i: (i // per, 0, i % per)
    return pl.pallas_call(
        _inproj_kernel,
        grid=(n // tm,),
        in_specs=[
            pl.BlockSpec((tm, d), row),
            _const_spec((1, d)),
            _const_spec((d, fw)),
            _const_spec((2 * fw, d)),
            _const_spec((d, nzx)),
            _const_spec((d, LANES)),
            _const_spec((2 * SUBLANES, d)),
            _const_spec((1, fw)),
            _const_spec((fw, 1)),
            _const_spec((fw, LANES)),
            _const_spec((LANES, fw)),
            _const_spec((1, LANES)),
            _const_spec((2 * SUBLANES, 1)),
        ],
        out_specs=[
            pl.BlockSpec((tm, fw), row),
            pl.BlockSpec((None, fw, tm), tr),
            pl.BlockSpec((None, fw, tm), tr),
            pl.BlockSpec((None, fw, tm), tr),
            pl.BlockSpec((None, fw, tm), tr),
            pl.BlockSpec((tm, fw), row),
            pl.BlockSpec((tm, nzx - fw), row),
            pl.BlockSpec((tm, LANES), row),
            pl.BlockSpec((2 * SUBLANES, tm), lambda i: (0, i)),
        ],
        out_shape=[
            jax.ShapeDtypeStruct((n, fw), BF16),
            jax.ShapeDtypeStruct((batch, fw, length), F32),
            jax.ShapeDtypeStruct((batch, fw, length), F32),
            jax.ShapeDtypeStruct((batch, fw, length), BF16),
            jax.ShapeDtypeStruct((batch, fw, length), BF16),
            jax.ShapeDtypeStruct((n, fw), F32),
            jax.ShapeDtypeStruct((n, nzx - fw), F32),
            jax.ShapeDtypeStruct((n, LANES), F32),
            jax.ShapeDtypeStruct((2 * SUBLANES, n), F32),
        ],
        compiler_params=_cparams(("parallel",)),
        name="inproj",
    )(x, p['norm_mix_g'], p['w_q'], p['w_kvt'], p['w_zx'], p['w_sm'], p['w_smt'], p['gq'], p['gk_col'],
      p['head_reduce'], p['head_expand'], p['b_sm'], p['b_smt'])


def _cumsum_kernel(sm_ref, smt_ref, col_ref, row_ref):
    length = sm_ref.shape[0]
    r = _iota((LANES, LANES), 0)
    c = _iota((LANES, LANES), 1)
    lower = (c <= r).astype(BF16)
    upper = (r <= c).astype(BF16)
    carry_c = jnp.zeros((1, LANES), F32)
    carry_r = jnp.zeros((smt_ref.shape[0], 1), F32)
    for b in range(length // LANES):
        sl = slice(b * LANES, (b + 1) * LANES)
        cs = _dot3_right(lower, sm_ref[sl, :]) + carry_c
        col_ref[sl, :] = cs * LOG2E
        carry_c = cs[LANES - 1:LANES, :]
        rs = _dot3_left(smt_ref[:, sl], upper) + carry_r
        row_ref[:, sl] = rs * LOG2E
        carry_r = rs[:, LANES - 1:LANES]


def _cumsum(sm, smt, batch, length):
    n = sm.shape[0]
    return pl.pallas_call(
        _cumsum_kernel,
        grid=(batch,),
        in_specs=[pl.BlockSpec((length, LANES), lambda b: (b, 0)),
                  pl.BlockSpec((2 * SUBLANES, length), lambda b: (0, b))],
        out_specs=[pl.BlockSpec((length, LANES), lambda b: (b, 0)),
                   pl.BlockSpec((2 * SUBLANES, length), lambda b: (0, b))],
        out_shape=[jax.ShapeDtypeStruct((n, LANES), F32),
                   jax.ShapeDtypeStruct((2 * SUBLANES, n), F32)],
        compiler_params=_cparams(("parallel",)),
        name="logf_cumsum",
    )(sm, smt)


PAGES_PER_CHUNK = 32
KV_SLOTS = 2
STREAMS = 2


def _fox_kernel(pt_ref, pb_ref, ph_ref, pi_ref, pj_ref, pv_ref,
                qb_ref, knvn_ref, lft_ref, masks_ref, kc_ref, vc_ref, lc_ref,
                q_ref, kt_ref, vt_ref, cc_ref, cr_ref,
                sxzm_ref, ssmt_ref, sst_ref, stail_ref, rows_ref, acol_ref,
                os_ref, op_ref, sfin_ref,
                kbuf, vbuf, lbuf, suf, m_s, l_s, acc_s, cn_s, knp, vnp, pm, pacc, pqa, xp, zp, smp, smtp,
                ksem, vsem, lsem,
                *, n_chunks, t_new):
    g = pl.program_id(0)
    n_steps = pl.num_programs(0)
    b = g // n_chunks
    c = g % n_chunks
    slot = g % KV_SLOTS
    ppc = PAGES_PER_CHUNK
    n_pages = n_chunks * ppc
    page = kbuf.shape[-1] // ppc
    heads = FORGET_HEADS
    fdim = heads * FOX_DIM

    def kv_copies(bb, cc, sl):
        cps = []
        for j in range(ppc):
            pg = pt_ref[bb, cc * ppc + j]
            cps.append(pltpu.make_async_copy(kc_ref.at[pg], kbuf.at[sl, :, :, pl.ds(j * page, page)], ksem.at[sl]))
            cps.append(pltpu.make_async_copy(vc_ref.at[pg], vbuf.at[sl, :, :, pl.ds(j * page, page)], vsem.at[sl]))
        return cps

    def lf_copies(bb, sl):
        return [pltpu.make_async_copy(lc_ref.at[pt_ref[bb, j]], lbuf.at[sl, pl.ds(j * heads, heads), :], lsem.at[sl])
                for j in range(n_pages)]

    @pl.when(g == 0)
    def _():
        for cp in lf_copies(0, 0):
            cp.start()
        for ahead in range(KV_SLOTS - 1):
            for cp in kv_copies(ahead // n_chunks, ahead % n_chunks, ahead):
                cp.start()
        pm[...] = jnp.zeros_like(pm)
        pacc[...] = jnp.zeros_like(pacc)
        pqa[...] = jnp.zeros_like(pqa)

    last_c = c == n_chunks - 1
    g_ahead = g + (KV_SLOTS - 1)

    @pl.when(g_ahead < n_steps)
    def _():
        for cp in kv_copies(g_ahead // n_chunks, g_ahead % n_chunks, g_ahead % KV_SLOTS):
            cp.start()

    bslot = b % 2

    @pl.when(jnp.logical_and(c == 0, g + n_chunks < n_steps))
    def _():
        for cp in lf_copies(b + 1, 1 - bslot):
            cp.start()

    @pl.when(c == 0)
    def _():
        pltpu.make_async_copy(lbuf.at[1 - bslot], lbuf.at[bslot], lsem.at[bslot]).wait()
        lf = lbuf[bslot]
        nr = lf.shape[0]
        later = masks_ref[nr:nr + page, 0:page]
        lpages = masks_ref[0:nr, 0:nr]
        within = _dot2_left(lf, later)
        tot = within[:, 0:1] + lf[:, 0:1]
        suf[...] = (within + _dot2_right(lpages, jnp.broadcast_to(tot, lf.shape))) * LOG2E
        rl = _iota((LANES, LANES), 0)
        cl = _iota((LANES, LANES), 1)
        cn_s[...] = _dot3_left(lft_ref[...], (rl <= cl).astype(BF16)) * LOG2E
        m_s[...] = jnp.full(m_s.shape, -jnp.inf, F32)
        l_s[...] = jnp.zeros(l_s.shape, F32)
        acc_s[...] = jnp.zeros(acc_s.shape, F32)

    cn = cn_s[...]
    cn_col = jnp.concatenate([cn[:, t:t + 1] for t in range(t_new)], axis=0)
    qb = qb_ref[...]

    other = (g + 1) % KV_SLOTS
    pltpu.make_async_copy(kbuf.at[other], kbuf.at[slot], ksem.at[slot]).wait()
    pltpu.make_async_copy(vbuf.at[other], vbuf.at[slot], vsem.at[slot]).wait()

    def online(st, s, pv):
        m = m_s[st]
        m_new = jnp.maximum(m, jnp.max(s, axis=1, keepdims=True))
        alpha = jnp.exp2(m - m_new)
        p = jnp.exp2(s - m_new)
        l_s[st] = alpha * l_s[st] + jnp.sum(p, axis=1, keepdims=True)
        acc_s[st] = alpha * acc_s[st] + pv(p.astype(BF16))
        m_s[st] = m_new

    def sample_chunk():
        hpc = ppc // STREAMS
        wdt = hpc * page
        scores = []
        for st in range(STREAMS):
            kt = kbuf[slot, :, :, pl.ds(st * wdt, wdt)].reshape(fdim, wdt).astype(BF16)
            sfx = jnp.concatenate([suf[pl.ds((c * ppc + st * hpc + j) * heads, heads), :] for j in range(hpc)],
                                  axis=1)
            scores.append(_dot(qb, kt) + jnp.concatenate([sfx] * t_new, axis=0) + cn_col)
        for st in range(STREAMS):
            vt = vbuf[slot, :, :, pl.ds(st * wdt, wdt)].reshape(fdim, wdt).astype(BF16)
            online(st, scores[st], lambda p: _dot_nt(p, vt))

    hp = ph_ref[g]
    pi = pi_ref[g]
    pj = pj_ref[g]
    kind = pv_ref[g]
    tq = q_ref.shape[0]
    first = _iota((1, LANES), 1) < FOX_DIM

    def prompt_pair(masked):
        kt = kt_ref[...]
        vt = vt_ref[...]
        cr = cr_ref[...]
        tk = kt.shape[1]
        half = FOX_DIM
        fresh = pj == 0

        def terms(x):
            hi, mid, lo = _split3(x)
            return hi.astype(F32), mid.astype(F32), lo.astype(F32)

        @pl.when(fresh)
        def _():
            cc = cc_ref[...]
            qf = q_ref[...].astype(F32)
            lane = _iota((tq, LANES), 1)
            for hh in range(2):
                cq = jnp.sum(jnp.where(lane == 2 * hp + hh, cc, 0.0), axis=1, keepdims=True)
                hi, mid, lo = terms(cq)
                e0 = half * (1 - hh)
                ext = jnp.where(lane == e0, hi, jnp.where(lane == e0 + 1, mid, jnp.where(lane == e0 + 2, lo,
                      jnp.where(jnp.logical_and(lane >= e0 + 3, lane < e0 + 6), 1.0, 0.0))))
                own = (lane < half) if hh == 0 else (lane >= half)
                pqa[hh] = jnp.where(own, qf, ext).astype(BF16)

        sub = _iota(cr.shape, 0)
        r16 = _iota((2 * SUBLANES, tk), 0)
        pad = jnp.zeros((half - 2 * SUBLANES, tk), BF16)
        ones_row = jnp.where(r16 == 0, 1.0, 0.0).astype(BF16)
        wide = lambda a: jnp.concatenate([a] * (tk // LANES), axis=1)
        outs = []
        for hh in range(2):
            ck = jnp.sum(jnp.where(sub == 2 * hp + hh, cr, 0.0), axis=0, keepdims=True)
            hi, mid, lo = terms(ck)
            kext = jnp.where(r16 < 3, 1.0, jnp.where(r16 == 3, -hi, jnp.where(r16 == 4, -mid,
                   jnp.where(r16 == 5, -lo, 0.0)))).astype(BF16)
            if hh == 0:
                kta = jnp.concatenate([kt[0:half, :], kext, pad], axis=0)
                vta = jnp.concatenate([vt[0:half, :], ones_row, pad], axis=0)
            else:
                kta = jnp.concatenate([kext, pad, kt[half:, :]], axis=0)
                vta = jnp.concatenate([ones_row, pad, vt[half:, :]], axis=0)
            s = _dot(pqa[hh], kta)
            if masked:
                s = jnp.where(_iota(s.shape, 1) + (pj * tk - pi * tq) <= _iota(s.shape, 0), s, -jnp.inf)
            m = jnp.where(fresh, -jnp.inf, pm[hh])
            acc = jnp.where(fresh, 0.0, pacc[hh])
            m_new = jnp.maximum(m, jnp.max(s, axis=1, keepdims=True))
            p = jnp.exp2(s - wide(m_new)).astype(BF16)
            acc = jnp.exp2(m - m_new) * acc + _dot_nt(p, vta)
            if masked:
                e0 = half * (1 - hh)
                outs.append(acc / acc[:, e0:e0 + 1])
            else:
                pm[hh] = m_new
                pacc[hh] = acc
        if masked:
            op_ref[...] = jnp.where(first, outs[0], outs[1])

    sample_chunk()

    @pl.when(kind == 2)
    def _():
        prompt_pair(True)

    @pl.when(kind == 1)
    def _():
        prompt_pair(False)

    @pl.when(c == min(1, n_chunks - 1))
    def _():
        cdim, hw = xp.shape[1], zp.shape[1]
        os_ref[:, fdim:] = _ssd_short_chunk(
            sxzm_ref[:, 0:cdim], sxzm_ref[:, cdim:cdim + hw], sxzm_ref[:, cdim + hw:], ssmt_ref[...],
            (xp, zp, smp, smtp), stail_ref[...], sst_ref, sfin_ref, (rows_ref, acol_ref))

    @pl.when(last_c)
    def _():
        knp[...] = jnp.zeros_like(knp)
        vnp[...] = jnp.zeros_like(vnp)
        knp[0:t_new, :] = knvn_ref[:, 0:fdim]
        vnp[0:t_new, :] = knvn_ref[:, fdim:]
        kn = knp[...].astype(BF16)
        vn = vnp[...].astype(BF16)
        s2 = _dot_nt(qb, kn) + cn_col - jnp.concatenate([cn] * t_new, axis=0)
        tok = _div_pow2(_iota(s2.shape, 0), heads)
        s2 = jnp.where(_iota(s2.shape, 1) <= tok, s2, -jnp.inf)
        online(0, s2, lambda p: _dot(p, vn))
        m_all = functools.reduce(jnp.maximum, [m_s[st] for st in range(STREAMS)])
        wts = [jnp.exp2(m_s[st] - m_all) for st in range(STREAMS)]
        o = (sum(wts[st] * acc_s[st] for st in range(STREAMS))
             / sum(wts[st] * l_s[st] for st in range(STREAMS)))
        keep = _div_pow2(_iota((heads, fdim), 1), FOX_DIM) == _iota((heads, fdim), 0)
        os_ref[:, 0:fdim] = jnp.concatenate(
            [jnp.sum(jnp.where(keep, o[t * heads:(t + 1) * heads, :], 0.0), axis=0, keepdims=True)
             for t in range(t_new)], axis=0)


def _pair_schedule(batch, n_hp, nq, tq, tk, n_steps, n_chunks):
    pairs = []
    for b in range(batch):
        for h in range(n_hp):
            for i in range(nq):
                j_last = ((i + 1) * tq - 1) // tk
                pairs += [(b, h, i, j, 2 if j == j_last else 1) for j in range(j_last + 1)]
    n_idle = n_steps - len(pairs)
    assert n_idle >= 0, "more prompt attention pairs than sample chunks"
    steps = np.arange(n_steps)
    chunk = steps % n_chunks
    seq = steps // n_chunks
    first_chunk = chunk == 0
    n_seq = n_steps // n_chunks
    if n_idle <= n_seq:
        pick = (seq * n_idle) // n_seq != ((seq + 1) * n_idle) // n_seq
        cost = np.where(np.logical_and(first_chunk, pick), 0, 1)
    else:
        cost = np.where(first_chunk, 0, np.where(chunk == n_chunks - 1, 1 + seq % 2, 3))
    idle = np.zeros(n_steps, bool)
    idle[np.argsort(cost, kind="stable")[:n_idle]] = True
    rows, k = [], 0
    for g in range(n_steps):
        if idle[g]:
            rows.append(pairs[min(k, len(pairs) - 1)][:4] + (0,))
        else:
            rows.append(pairs[k])
            k += 1
    return [jnp.asarray(np.array(col, np.int32)) for col in zip(*rows)]


def _fox(page_table, qblk, knvn, lft, kc, vc, lc, q, ktb, vtb, ccol, crow4, tq, tk,
         sxzm, smt4_s, st0, tail0, p):
    cdim = tail0.shape[2]
    hw = st0.shape[1]
    bs, rows, fdim = qblk.shape
    t_new = rows // FORGET_HEADS
    n_pages = page_table.shape[1]
    page = kc.shape[-1]
    n_chunks = n_pages // PAGES_PER_CHUNK
    pc = PAGES_PER_CHUNK * page
    n_steps = bs * n_chunks
    bp, _, length = ktb.shape
    nq = length // tq
    n_hp = fdim // LANES
    sched = _pair_schedule(bp, n_hp, nq, tq, tk, n_steps, n_chunks)
    pos = np.arange(page)
    rows_ph = np.arange(n_pages * FORGET_HEADS)
    nr = rows_ph.size
    masks_np = np.zeros((nr + page, max(nr, page)), np.float32)
    masks_np[:nr, :nr] = ((rows_ph[None, :] % FORGET_HEADS == rows_ph[:, None] % FORGET_HEADS)
                          & (rows_ph[None, :] // FORGET_HEADS > rows_ph[:, None] // FORGET_HEADS))
    masks_np[nr:, :page] = pos[:, None] > pos[None, :]
    masks = jnp.asarray(masks_np, BF16)
    per_s = lambda g, *_: (g // n_chunks, 0, 0)
    grid_spec = pltpu.PrefetchScalarGridSpec(
        num_scalar_prefetch=6,
        grid=(n_steps,),
        in_specs=[
            pl.BlockSpec((None, rows, fdim), per_s),
            pl.BlockSpec((None, t_new, 2 * fdim), per_s),
            pl.BlockSpec((None, FORGET_HEADS, LANES), per_s),
            pl.BlockSpec(masks.shape, lambda g, *_: (0, 0)),
            pl.BlockSpec(memory_space=pl.ANY),
            pl.BlockSpec(memory_space=pl.ANY),
            pl.BlockSpec(memory_space=pl.ANY),
            pl.BlockSpec((tq, LANES), lambda g, pt, pb, ph, pi, pj, pv: (pb[g] * nq + pi[g], ph[g])),
            pl.BlockSpec((None, LANES, tk), lambda g, pt, pb, ph, pi, pj, pv: (pb[g], ph[g], pj[g])),
            pl.BlockSpec((None, LANES, tk), lambda g, pt, pb, ph, pi, pj, pv: (pb[g], ph[g], pj[g])),
            pl.BlockSpec((tq, LANES), lambda g, pt, pb, ph, pi, pj, pv: (pb[g] * nq + pi[g], 0)),
            pl.BlockSpec((None, None, 2 * SUBLANES, tk), lambda g, pt, pb, ph, pi, pj, pv: (pb[g], pj[g], 0, 0)),
            pl.BlockSpec((None, t_new, cdim + hw + LANES), per_s),
            pl.BlockSpec((None, None, 2 * SUBLANES, t_new), lambda g, *_: (g // n_chunks, 0, 0, 0)),
            pl.BlockSpec((None, hw, SSD_STATE), per_s),
            pl.BlockSpec((None, SUBLANES, cdim), per_s),
        ] + _ssd_const_specs(cdim),
        out_specs=[
            pl.BlockSpec((None, t_new, fdim + hw), per_s),
            pl.BlockSpec((tq, LANES), lambda g, pt, pb, ph, pi, pj, pv: (pb[g] * nq + pi[g], ph[g])),
            pl.BlockSpec((None, hw, SSD_STATE), per_s),
        ],
        scratch_shapes=[
            pltpu.VMEM((KV_SLOTS, FORGET_HEADS, FOX_DIM, pc), F32),
            pltpu.VMEM((KV_SLOTS, FORGET_HEADS, FOX_DIM, pc), F32),
            pltpu.VMEM((2, n_pages * FORGET_HEADS, page), F32),
            pltpu.VMEM((n_pages * FORGET_HEADS, page), F32),
            pltpu.VMEM((STREAMS, rows, 1), F32),
            pltpu.VMEM((STREAMS, rows, 1), F32),
            pltpu.VMEM((STREAMS, rows, fdim), F32),
            pltpu.VMEM((FORGET_HEADS, LANES), F32),
            pltpu.VMEM((LANES, fdim), F32),
            pltpu.VMEM((LANES, fdim), F32),
            pltpu.VMEM((2, tq, LANES), F32),
            pltpu.VMEM((2, tq, LANES), F32),
            pltpu.VMEM((2, tq, LANES), BF16),
        ] + _ssd_pad_scratch(cdim, hw) + [
            pltpu.SemaphoreType.DMA((KV_SLOTS,)),
            pltpu.SemaphoreType.DMA((KV_SLOTS,)),
            pltpu.SemaphoreType.DMA((2,)),
        ],
    )
    return pl.pallas_call(
        functools.partial(_fox_kernel, n_chunks=n_chunks, t_new=t_new),
        grid_spec=grid_spec,
        out_shape=[jax.ShapeDtypeStruct((bs, t_new, fdim + hw), F32),
                   jax.ShapeDtypeStruct((bp * length, fdim), F32),
                   jax.ShapeDtypeStruct((bs, hw, SSD_STATE), F32)],
        compiler_params=_cparams(("arbitrary",)),
        name="fox",
    )(page_table, *sched, qblk, knvn, lft, masks, kc, vc, lc, q, ktb, vtb, ccol, crow4,
      sxzm, smt4_s, st0, tail0, *_ssd_consts(p))


def _ssd_chunk(x, z, sm, smt, prev, state_in, state_out, rows_ref, acol_ref):
    q = CHUNK
    hw = SSD_HEADS * SSD_DIM
    gs = SSD_STATE
    cw_ref = rows_ref.at[0:SSD_CONV]
    cb_ref = rows_ref.at[SSD_CONV:SSD_CONV + 1]
    dsk_ref = rows_ref.at[SSD_CONV + 1:SSD_CONV + 2, 0:hw]
    ng_ref = rows_ref.at[SSD_CONV + 1:SSD_CONV + 2, hw:2 * hw]
    arow_ref = rows_ref.at[SSD_CONV + 2:SSD_CONV + 3, 0:LANES]

    row8 = _iota(prev.shape, 0)
    conv = cb_ref[...] + cw_ref[SSD_CONV - 1:SSD_CONV, :] * x
    for sh in range(1, SSD_CONV):
        rx = pltpu.roll(x, sh, 0)
        fix = jnp.where(row8 < sh, pltpu.roll(prev, sh, 0), rx[0:SUBLANES, :])
        xs_sh = jnp.concatenate([fix, rx[SUBLANES:, :]], axis=0)
        conv = conv + cw_ref[SSD_CONV - 1 - sh:SSD_CONV - sh, :] * xs_sh
    xc = _silu(conv)
    xs = xc[:, 0:hw]

    r = _iota((q, q), 0)
    cidx = _iota((q, q), 1)
    tril = cidx <= r
    lower = tril.astype(BF16)
    upper = (r <= cidx).astype(BF16)
    acum_c = _dot3_right(lower, sm * arow_ref[...])
    acum_r = _dot3_left(smt * acol_ref[...], upper)
    lane = _iota((1, LANES), 1)
    first = lane < SSD_DIM
    rowp = _iota((LANES, 1), 0) < SSD_DIM

    ys = []
    for pr in range(SSD_HEADS // 2):
        g = pr // (SSD_HEADS // 4)
        bg = xc[:, hw + g * gs: hw + (g + 1) * gs].astype(BF16)
        cg = xc[:, hw + 2 * gs + g * gs: hw + 2 * gs + (g + 1) * gs].astype(BF16)
        gmat = _dot_nt(cg, bg)
        xpair = xs[:, pr * LANES:(pr + 1) * LANES]
        xpb = xpair.astype(BF16)
        yd, te, ea, cd = [], [], [], []
        for hh in range(2):
            h = FORGET_HEADS + 2 * pr + hh
            ac = acum_c[:, h:h + 1]
            ar = acum_r[h:h + 1, :]
            dtr = smt[h:h + 1, :]
            dtc = sm[:, h:h + 1]
            last = acum_c[q - 1:q, h:h + 1]
            decay = jnp.exp(jnp.where(tril, ac - ar, -jnp.inf))
            sc = gmat * decay * dtr
            yd.append(_dot(sc.astype(BF16), xpb))
            te.append(jnp.exp(last - ac) * dtc)
            ea.append(jnp.exp(ac))
            cd.append(jnp.exp(last))
        ydiag = jnp.where(first, yd[0], yd[1])
        xsc = (xpair * jnp.where(first, te[0], te[1])).astype(BF16)
        cstate = _dot_tn(xsc, bg)
        prev_st = state_in[pr * LANES:(pr + 1) * LANES, :]
        yoff = _dot_nt(cg, prev_st.astype(BF16)) * jnp.where(first, ea[0], ea[1])
        state_out[pr * LANES:(pr + 1) * LANES, :] = prev_st * jnp.where(rowp, cd[0], cd[1]) + cstate
        ys.append(ydiag + yoff + dsk_ref[:, pr * LANES:(pr + 1) * LANES] * xpair)
    y = jnp.concatenate(ys, axis=1) * _silu(z)
    return _rms(y, ng_ref[...])


def _ssd_short_chunk(x, z, sm, smt, pads, prev, state_in, state_out, consts):
    t_real = x.shape[0]
    xp, zp, smp, smtp = pads
    xp[...] = jnp.zeros_like(xp)
    zp[...] = jnp.zeros_like(zp)
    smp[...] = jnp.zeros_like(smp)
    smtp[...] = jnp.zeros_like(smtp)
    xp[0:t_real, :] = x
    zp[0:t_real, :] = z
    smp[0:t_real, :] = sm
    smtp[:, 0:t_real] = smt
    y = _ssd_chunk(xp[...], zp[...], smp[...], smtp[...], prev, state_in, state_out, *consts)
    return y[0:t_real, :]


def _ssd_pad_scratch(cdim, hw):
    return [pltpu.VMEM((CHUNK, cdim), F32), pltpu.VMEM((CHUNK, hw), F32),
            pltpu.VMEM((CHUNK, LANES), F32), pltpu.VMEM((2 * SUBLANES, CHUNK), F32)]


def _ssd_param_rows(conv_w, conv_b, d_row, norm_g, a_row):
    cdim = conv_w.shape[1]
    padl = lambda v: jnp.pad(v, ((0, 0), (0, cdim - v.shape[1])))
    rows = [conv_w, conv_b, jnp.concatenate([d_row, norm_g], axis=1), padl(a_row)]
    used = sum(r.shape[0] for r in rows)
    return jnp.concatenate(rows + [jnp.zeros((SUBLANES - used, cdim), F32)], axis=0)


def _ssd_const_specs(cdim):
    return [_const_spec((SUBLANES, cdim)), _const_spec((2 * SUBLANES, 1))]


def _ssd_consts(p):
    return (p['ssd_rows'], p['a_col'])


def _ssd_kernel(xbc_ref, z_ref, sm_ref, smt_ref, rows_ref, acol_ref, y_ref, fin_ref, state, tail):
    c = pl.program_id(1)

    @pl.when(c == 0)
    def _():
        state[...] = jnp.zeros_like(state)
        tail[...] = jnp.zeros_like(tail)

    x = xbc_ref[...]
    y_ref[...] = _ssd_chunk(x, z_ref[...], sm_ref[...], smt_ref[...], tail[...], state, state, rows_ref, acol_ref)
    tail[...] = x[CHUNK - SUBLANES:, :]

    @pl.when(c == pl.num_programs(1) - 1)
    def _():
        fin_ref[...] = state[...]


def _ssd(xbc, z, sm, smt4, p):
    batch, length, cdim = xbc.shape
    hw = z.shape[2]
    nc = length // CHUNK
    blk = lambda b, c: (b, c, 0)
    return pl.pallas_call(
        _ssd_kernel,
        grid=(batch, nc),
        in_specs=[
            pl.BlockSpec((None, CHUNK, cdim), blk),
            pl.BlockSpec((None, CHUNK, hw), blk),
            pl.BlockSpec((None, CHUNK, LANES), blk),
            pl.BlockSpec((None, None, 2 * SUBLANES, CHUNK), lambda b, c: (b, c, 0, 0)),
        ] + _ssd_const_specs(cdim),
        out_specs=[pl.BlockSpec((None, CHUNK, hw), blk),
                   pl.BlockSpec((None, hw, SSD_STATE), lambda b, c: (b, 0, 0))],
        out_shape=[jax.ShapeDtypeStruct((batch, length, hw), F32),
                   jax.ShapeDtypeStruct((batch, hw, SSD_STATE), F32)],
        scratch_shapes=[pltpu.VMEM((hw, SSD_STATE), F32), pltpu.VMEM((SUBLANES, cdim), F32)],
        compiler_params=_cparams(("parallel", "arbitrary")),
        name="ssd",
    )(xbc, z, sm, smt4, *_ssd_consts(p))


def _mem_kv_kernel(m_ref, g_ref, w_ref, gk_ref, k_ref, v_ref):
    mw = MEM_HEADS * MEM_DIM
    xn = _rms(m_ref[...], g_ref[...]).astype(BF16)
    kv = _dot(xn, w_ref[...])
    for h in range(MEM_HEADS):
        sl = slice(h * MEM_DIM, (h + 1) * MEM_DIM)
        k_ref[:, sl] = _rms(kv[:, sl], gk_ref[...])
    v_ref[...] = kv[:, mw:]


def _mem_kv(mem, p, tm):
    n, d = mem.shape
    mw = MEM_HEADS * MEM_DIM
    row = lambda i: (i, 0)
    return pl.pallas_call(
        _mem_kv_kernel,
        grid=(n // tm,),
        in_specs=[pl.BlockSpec((tm, d), row), _const_spec((1, d)), _const_spec((d, 2 * mw)),
                  _const_spec((1, MEM_DIM))],
        out_specs=[pl.BlockSpec((tm, mw), row), pl.BlockSpec((tm, mw), row)],
        out_shape=[jax.ShapeDtypeStruct((n, mw), F32), jax.ShapeDtypeStruct((n, mw), F32)],
        compiler_params=_cparams(("parallel",)),
        name="mem_kv",
    )(mem, p['mem_in_norm_g'], p['w_mem_kv'], p['mem_k_norm_g'])


def _mix_out(x_ref, mix_refs, wout_ref, g_ref, wq_ref, gq_ref):
    y1 = x_ref[...]
    lo = 0
    for m_ref in mix_refs:
        hi = lo + m_ref.shape[-1]
        y1 = y1 + _dot(m_ref[...].astype(BF16), wout_ref[lo:hi, :])
        lo = hi
    xn = _rms(y1, g_ref[...]).astype(BF16)
    q = _dot(xn, wq_ref[...])
    qs = []
    for h in range(MEM_HEADS):
        qh = _rms(q[:, h * MEM_DIM:(h + 1) * MEM_DIM], gq_ref[...]) * MEM_DIM ** -0.5
        qs.append(qh)
    return y1, qs


def _memattn_prompt_kernel(x_ref, fo_ref, so_ref, wout_ref, g_ref, wq_ref, gq_ref, mk_ref, mv_ref, wo_ref, o_ref):
    y1, qs = _mix_out(x_ref, (fo_ref, so_ref), wout_ref, g_ref, wq_ref, gq_ref)
    outs = []
    for h in range(MEM_HEADS):
        sl = slice(h * MEM_DIM, (h + 1) * MEM_DIM)
        s = _dot_nt(qs[h].astype(BF16), mk_ref[:, sl].astype(BF16))
        p = jnp.exp(s - jnp.max(s, axis=1, keepdims=True))
        o = _dot(p.astype(BF16), mv_ref[:, sl].astype(BF16)) / jnp.sum(p, axis=1, keepdims=True)
        outs.append(o.astype(BF16))
    o_ref[...] = y1 + _dot(jnp.concatenate(outs, axis=1), wo_ref[...])


def _memattn_prompt(x, fo, so, mk, mv, p, tm, length, n_mem):
    n, d = x.shape
    fw = fo.shape[1]
    mw = MEM_HEADS * MEM_DIM
    per = length // tm
    row = lambda i: (i, 0)
    return pl.pallas_call(
        _memattn_prompt_kernel,
        grid=(n // tm,),
        in_specs=[pl.BlockSpec((tm, d), row), pl.BlockSpec((tm, fw), row), pl.BlockSpec((tm, fw), row),
                  _const_spec((2 * fw, d)), _const_spec((1, d)), _const_spec((d, mw)), _const_spec((1, MEM_DIM)),
                  pl.BlockSpec((n_mem, mw), lambda i: (i // per, 0)),
                  pl.BlockSpec((n_mem, mw), lambda i: (i // per, 0)),
                  _const_spec((mw, d))],
        out_specs=pl.BlockSpec((tm, d), row),
        out_shape=jax.ShapeDtypeStruct((n, d), F32),
        compiler_params=_cparams(("parallel",)),
        name="memattn_prompt",
    )(x, fo, so, p['w_out'], p['norm_mem_g'], p['w_mem_q'], p['mem_q_norm_g'], mk, mv, p['w_mem_o'])


def _memattn_sample_kernel(x_ref, fs_ref, wout_ref, g_ref, wq_ref, gq_ref, mk_ref, mv_ref, wo_ref, o_ref,
                           *, t_new):
    y1, qs = _mix_out(x_ref, (fs_ref,), wout_ref, g_ref, wq_ref, gq_ref)
    tm = y1.shape[0]
    per = SUBLANES // t_new
    nrow = MEM_HEADS * SUBLANES
    ncol = mk_ref.shape[1]
    row = _iota((nrow, 1), 0)
    mine = [_div_pow2(lax.bitwise_and(row, jnp.int32(SUBLANES - 1)), t_new) == u for u in range(per)]
    same_head = (lax.bitwise_and(_iota((nrow, ncol), 1), jnp.int32(MEM_HEADS - 1))
                 == _div_pow2(_iota((nrow, ncol), 0), SUBLANES))
    tiles = []
    for r in range(tm // SUBLANES):
        qst = jnp.concatenate([qs[h][r * SUBLANES:(r + 1) * SUBLANES, :] for h in range(MEM_HEADS)], axis=0)
        qst = qst.astype(BF16)
        s = None
        for u in range(per):
            su = _dot_nt(qst, mk_ref[r * per + u].astype(BF16))
            s = su if s is None else jnp.where(mine[u], su, s)
        s = jnp.where(same_head, s, -jnp.inf)
        p = jnp.exp(s - jnp.max(s, axis=1, keepdims=True))
        pb = p.astype(BF16)
        o = None
        for u in range(per):
            ou = _dot(pb, mv_ref[r * per + u].astype(BF16))
            o = ou if o is None else jnp.where(mine[u], ou, o)
        o = o / jnp.sum(p, axis=1, keepdims=True)
        tiles.append(jnp.concatenate([o[h * SUBLANES:(h + 1) * SUBLANES, :] for h in range(MEM_HEADS)], axis=1))
    o_ref[...] = y1 + _dot(jnp.concatenate(tiles, axis=0).astype(BF16), wo_ref[...])


def _memattn_sample(x, fs, mk, mv, p, tm, t_new):
    n, d = x.shape
    mixw = fs.shape[1]
    mw = MEM_HEADS * MEM_DIM
    n_flat = mk.shape[1]
    bb = tm // t_new
    row = lambda i: (i, 0)
    return pl.pallas_call(
        functools.partial(_memattn_sample_kernel, t_new=t_new),
        grid=(n // tm,),
        in_specs=[pl.BlockSpec((tm, d), row), pl.BlockSpec((tm, mixw), row),
                  _const_spec((mixw, d)), _const_spec((1, d)), _const_spec((d, mw)), _const_spec((1, MEM_DIM)),
                  pl.BlockSpec((bb, n_flat, MEM_DIM), lambda i: (i, 0, 0)),
                  pl.BlockSpec((bb, n_flat, MEM_DIM), lambda i: (i, 0, 0)),
                  _const_spec((mw, d))],
        out_specs=pl.BlockSpec((tm, d), row),
        out_shape=jax.ShapeDtypeStruct((n, d), F32),
        compiler_params=_cparams(("parallel",)),
        name="memattn_sample",
    )(x, fs, p['w_out'], p['norm_mem_g'], p['w_mem_q'], p['mem_q_norm_g'], mk, mv, p['w_mem_o'])


FF_CHUNK = 1024


def _ff_chunks(dff):
    return [(s, min(s + FF_CHUNK, dff)) for s in range(0, dff, FF_CHUNK)]


def _ffn_prompt_kernel(x_ref, g_ref, wg_ref, wu_ref, cw_ref, cb_ref, wd_ref, o_ref, tail_ref, halo, *, per):
    i = pl.program_id(0)
    tm = x_ref.shape[0]
    dff = wg_ref.shape[1]

    @pl.when(i % per == 0)
    def _():
        halo[...] = jnp.zeros_like(halo)

    x = x_ref[...]
    xn = _rms(x, g_ref[...]).astype(BF16)
    row8 = _iota((SUBLANES, 1), 0)
    acc = x
    for lo, hi in _ff_chunks(dff):
        gate = _dot(xn, wg_ref[:, lo:hi])
        up = _dot(xn, wu_ref[:, lo:hi])
        prev = halo[:, lo:hi]
        conv = cb_ref[:, lo:hi] + cw_ref[FFN_CONV - 1:FFN_CONV, lo:hi] * gate
        for sh in range(1, FFN_CONV):
            rg = pltpu.roll(gate, sh, 0)
            fix = jnp.where(row8 < sh, pltpu.roll(prev, sh, 0), rg[0:SUBLANES, :])
            conv = conv + cw_ref[FFN_CONV - 1 - sh:FFN_CONV - sh, lo:hi] * jnp.concatenate([fix, rg[SUBLANES:, :]], axis=0)
        halo[:, lo:hi] = gate[tm - SUBLANES:tm, :]
        hmid = (_silu(conv) * up).astype(BF16)
        acc = acc + _dot(hmid, wd_ref[lo:hi, :])
    o_ref[...] = acc
    tail_ref[...] = halo[...]


def _ffn_prompt(x, p, tm, batch, length):
    n, d = x.shape
    dff = p['w_ffn_gate'].shape[1]
    per = length // tm
    row = lambda i: (i, 0)
    return pl.pallas_call(
        functools.partial(_ffn_prompt_kernel, per=per),
        grid=(n // tm,),
        in_specs=[pl.BlockSpec((tm, d), row), _const_spec((1, d)),
                  pl.BlockSpec((d, dff), lambda i: (0, 0), pipeline_mode=pl.Buffered(1)),
                  pl.BlockSpec((d, dff), lambda i: (0, 0), pipeline_mode=pl.Buffered(1)),
                  _const_spec((FFN_CONV, dff)), _const_spec((1, dff)),
                  pl.BlockSpec((dff, d), lambda i: (0, 0), pipeline_mode=pl.Buffered(1))],
        out_specs=[pl.BlockSpec((tm, d), row),
                   pl.BlockSpec((None, SUBLANES, dff), lambda i: (i // per, 0, 0))],
        out_shape=[jax.ShapeDtypeStruct((n, d), F32),
                   jax.ShapeDtypeStruct((batch, SUBLANES, dff), F32)],
        scratch_shapes=[pltpu.VMEM((SUBLANES, dff), F32)],
        compiler_params=_cparams(("arbitrary",)),
        name="ffn_prompt",
    )(x, p['norm_ffn_g'], p['w_ffn_gate'], p['w_ffn_up'], p['ffn_conv_w'], p['ffn_conv_b'], p['w_ffn_down'])


def _ffn_sample_kernel(x_ref, buf_ref, g_ref, wg_ref, wu_ref, cw_ref, cb_ref, wd_ref, o_ref, nbuf_ref, *, t_new):
    nb = x_ref.shape[0] // t_new
    dff = wg_ref.shape[1]
    x = x_ref[...]
    xn = _rms(x, g_ref[...]).astype(BF16)
    acc = x
    for lo, hi in _ff_chunks(dff):
        gate = _dot(xn, wg_ref[:, lo:hi])
        up = _dot(xn, wu_ref[:, lo:hi])
        ext = jnp.concatenate([buf_ref[:, lo:hi], gate], axis=0)
        conv = cb_ref[:, lo:hi]
        for j in range(FFN_CONV):
            conv = conv + cw_ref[j:j + 1, lo:hi] * ext[j * nb:(j + t_new) * nb, :]
        nbuf_ref[:, lo:hi] = ext[t_new * nb:, :]
        hmid = (_silu(conv) * up).astype(BF16)
        acc = acc + _dot(hmid, wd_ref[lo:hi, :])
    o_ref[...] = acc


def _ffn_sample(x_tm, buf_tm, p, t_new):
    n, d = x_tm.shape
    dff = p['w_ffn_gate'].shape[1]
    nbr = buf_tm.shape[0]
    return pl.pallas_call(
        functools.partial(_ffn_sample_kernel, t_new=t_new),
        grid=(1,),
        in_specs=[_const_spec((n, d)), _const_spec((nbr, dff)), _const_spec((1, d)),
                  pl.BlockSpec((d, dff), lambda i: (0, 0), pipeline_mode=pl.Buffered(1)),
                  pl.BlockSpec((d, dff), lambda i: (0, 0), pipeline_mode=pl.Buffered(1)),
                  _const_spec((FFN_CONV, dff)), _const_spec((1, dff)),
                  pl.BlockSpec((dff, d), lambda i: (0, 0), pipeline_mode=pl.Buffered(1))],
        out_specs=[_const_spec((n, d)), _const_spec((nbr, dff))],
        out_shape=[jax.ShapeDtypeStruct((n, d), F32), jax.ShapeDtypeStruct((nbr, dff), F32)],
        compiler_params=_cparams(("arbitrary",)),
        name="ffn_sample",
    )(x_tm, buf_tm, p['norm_ffn_g'], p['w_ffn_gate'], p['w_ffn_up'], p['ffn_conv_w'], p['ffn_conv_b'],
      p['w_ffn_down'])


def _prep_layer(l, norm_mix_g, w_in, b_forget, fox_q_norm_g, fox_k_norm_g, ssd_conv_w, ssd_conv_b, ssd_dt_bias,
                ssd_a_log, ssd_d, ssd_norm_g, w_out, norm_mem_g, mem_in_norm_g, w_mem_q, w_mem_kv, mem_q_norm_g,
                mem_k_norm_g, w_mem_o, norm_ffn_g, w_ffn_gate, w_ffn_up, ffn_conv_w, ffn_conv_b, w_ffn_down):
    fw = FORGET_HEADS * FOX_DIM
    hw = SSD_HEADS * SSD_DIM
    w = w_in[l]
    o_f = 3 * fw
    o_z = o_f + FORGET_HEADS
    o_x = o_z + hw
    o_dt = w.shape[1] - SSD_HEADS
    w_small = jnp.concatenate([w[:, o_f:o_z], w[:, o_dt:]], axis=1)
    b_small = jnp.concatenate([b_forget[l], ssd_dt_bias[l]]).astype(F32)
    a_neg = -jnp.exp(ssd_a_log[l].astype(F32))
    a16 = jnp.concatenate([jnp.zeros((FORGET_HEADS,), F32), a_neg])
    head_of = jnp.arange(fw) // FOX_DIM
    onehot = (head_of[:, None] == jnp.arange(LANES)[None, :]).astype(F32)
    row = lambda v: v.reshape(1, -1).astype(F32)
    return {
        'norm_mix_g': row(norm_mix_g[l]),
        'w_q': w[:, :fw].astype(BF16),
        'w_kvt': w[:, fw:o_f].T.astype(BF16),
        'w_zx': w[:, o_z:o_dt].astype(BF16),
        'w_sm': jnp.pad(w_small, ((0, 0), (0, LANES - w_small.shape[1]))).astype(BF16),
        'w_smt': w_small.T.astype(BF16),
        'b_sm': jnp.pad(b_small, (0, LANES - b_small.shape[0])).reshape(1, LANES),
        'b_smt': b_small.reshape(-1, 1),
        'gq': row(jnp.tile(fox_q_norm_g[l], FORGET_HEADS)),
        'gk_col': jnp.tile(fox_k_norm_g[l], FORGET_HEADS).reshape(-1, 1).astype(F32),
        'head_reduce': (onehot / FOX_DIM).astype(BF16),
        'head_expand': onehot.T.astype(BF16),
        'ssd_rows': _ssd_param_rows(ssd_conv_w[l].astype(F32), row(ssd_conv_b[l]), row(jnp.repeat(ssd_d[l], SSD_DIM)),
                                    row(ssd_norm_g[l]), row(a16)),
        'a_col': a16.reshape(-1, 1),
        'w_out': w_out[l].astype(BF16),
        'norm_mem_g': row(norm_mem_g[l]),
        'mem_in_norm_g': row(mem_in_norm_g[l]),
        'w_mem_q': w_mem_q[l].astype(BF16),
        'w_mem_kv': w_mem_kv[l].astype(BF16),
        'mem_q_norm_g': row(mem_q_norm_g[l]),
        'mem_k_norm_g': row(mem_k_norm_g[l]),
        'w_mem_o': w_mem_o[l].astype(BF16),
        'norm_ffn_g': row(norm_ffn_g[l]),
        'w_ffn_gate': w_ffn_gate[l].astype(BF16),
        'w_ffn_up': w_ffn_up[l].astype(BF16),
        'ffn_conv_w': ffn_conv_w[l].astype(F32),
        'ffn_conv_b': row(ffn_conv_b[l]),
        'w_ffn_down': w_ffn_down[l].astype(BF16),
    }


def _row_tile(n, want):
    t = min(n, want)
    while n % t:
        t //= 2
    return t


def _layer(xp, mem, xs, kc, vc, lc, st_ssm, st_conv, st_ffn, mem_k, mem_v, page_table, p):
    fw = FORGET_HEADS * FOX_DIM
    hw = SSD_HEADS * SSD_DIM
    heads = lambda a, b_, t_: a.reshape(b_, FORGET_HEADS, FOX_DIM, t_).transpose(0, 3, 1, 2)

    bp, length, d = xp.shape
    n_p = bp * length
    n_mem = mem.shape[1]
    tm = _row_tile(length, 512)
    xpf = xp.reshape(n_p, d)
    q_p, kt_p, vt_p, ktb_p, vtb_p, z_p, xbc_p, sm_p, smt_p = _inproj(xpf, p, tm, bp, length)
    bs, t_new, _ = xs.shape
    n_s = bs * t_new
    xsf = xs.reshape(n_s, d)
    q_s, kt_s, vt_s, _, _, z_s, xbc_s, sm_s, smt_s = _inproj(xsf, p, _row_tile(n_s, 512), 1, n_s)

    ccol, crow = _cumsum(sm_p, smt_p, bp, length)
    tq = _row_tile(length, 512)
    nq = length // tq
    tk = _row_tile(length, 1024)
    crow4 = crow.reshape(2 * SUBLANES, bp, length // tk, tk).transpose(1, 2, 0, 3)
    k_s = kt_s[0].T.reshape(bs, t_new, fw)
    v_s = vt_s[0].T.reshape(bs, t_new, fw)
    own = (jnp.arange(fw) // FOX_DIM)[None, :] == jnp.arange(FORGET_HEADS)[:, None]
    qblk = jnp.where(own[None, None], q_s.reshape(bs, t_new, 1, fw), jnp.zeros((), q_s.dtype))
    qblk = qblk.reshape(bs, t_new * FORGET_HEADS, fw)
    lft = jnp.pad(smt_s[:FORGET_HEADS].reshape(FORGET_HEADS, bs, t_new).transpose(1, 0, 2),
                  ((0, 0), (0, 0), (0, LANES - t_new)))
    cdim = xbc_p.shape[1]
    xbc3_s = xbc_s.reshape(bs, t_new, cdim)
    smt4_s = smt_s.reshape(2 * SUBLANES, bs, 1, t_new).transpose(1, 2, 0, 3)
    tail0 = jnp.pad(st_conv, ((0, 0), (SUBLANES - (SSD_CONV - 1), 0), (0, 0)))
    sxzm = jnp.concatenate([xbc_s, z_s, sm_s], axis=1).reshape(bs, t_new, cdim + hw + LANES)
    mix_s, fox_p, ssm_new = _fox(
        page_table, qblk, jnp.concatenate([k_s, v_s], axis=2), lft, kc, vc, lc, q_p, ktb_p, vtb_p, ccol, crow4,
        tq, tk, sxzm, smt4_s, st_ssm.reshape(bs, hw, SSD_STATE), tail0, p)

    nc = length // CHUNK
    smt4_p = smt_p.reshape(2 * SUBLANES, bp, nc, CHUNK).transpose(1, 2, 0, 3)
    xbc3_p = xbc_p.reshape(bp, length, cdim)
    ssd_p, ssm_fin = _ssd(xbc3_p, z_p.reshape(bp, length, hw), sm_p.reshape(bp, length, LANES), smt4_p, p)

    mk, mv = _mem_kv(mem.reshape(bp * n_mem, d), p, _row_tile(n_mem, 512))
    y2_p = _memattn_prompt(xpf, fox_p, ssd_p.reshape(n_p, hw), mk, mv, p, tm, length, n_mem)
    flat = lambda a: a.reshape(a.shape[0], a.shape[1] * a.shape[2], a.shape[3])
    y2_s = _memattn_sample(xsf, mix_s.reshape(n_s, fw + hw), flat(mem_k), flat(mem_v), p, _row_tile(n_s, 32), t_new)

    y3_p, ffn_tail = _ffn_prompt(y2_p, p, tm, bp, length)
    x_tm = y2_s.reshape(bs, t_new, d).transpose(1, 0, 2).reshape(n_s, d)
    dff = st_ffn.shape[-1]
    buf_tm = st_ffn.transpose(1, 0, 2).reshape((FFN_CONV - 1) * bs, dff)
    y3_tm, nbuf_tm = _ffn_sample(x_tm, buf_tm, p, t_new)

    prompt = (y3_p.reshape(bp, length, d),
              heads(kt_p, bp, length),
              heads(vt_p, bp, length),
              sm_p.reshape(bp, length, LANES)[:, :, :FORGET_HEADS],
              ssm_fin.reshape(bp, SSD_HEADS, SSD_DIM, SSD_STATE),
              xbc3_p[:, length - (SSD_CONV - 1):, :],
              ffn_tail[:, SUBLANES - (FFN_CONV - 1):, :],
              mk.reshape(bp, n_mem, MEM_HEADS, MEM_DIM),
              mv.reshape(bp, n_mem, MEM_HEADS, MEM_DIM))
    sample = (y3_tm.reshape(t_new, bs, d).transpose(1, 0, 2),
              k_s.reshape(bs, t_new, FORGET_HEADS, FOX_DIM),
              v_s.reshape(bs, t_new, FORGET_HEADS, FOX_DIM),
              sm_s.reshape(bs, t_new, LANES)[:, :, :FORGET_HEADS],
              ssm_new.reshape(bs, SSD_HEADS, SSD_DIM, SSD_STATE),
              jnp.concatenate([st_conv, xbc3_s], axis=1)[:, t_new:, :],
              nbuf_tm.reshape(FFN_CONV - 1, bs, dff).transpose(1, 0, 2))
    return prompt, sample


def kernel(x_prompt, mem_prompt, x_sample, cache_fox_k, cache_fox_v, cache_fox_logf, state_ssm, state_ssm_conv, state_ffn_conv, cache_mem_k, cache_mem_v, page_table, norm_mix_g, w_in, b_forget, fox_q_norm_g, fox_k_norm_g, ssd_conv_w, ssd_conv_b, ssd_dt_bias, ssd_a_log, ssd_d, ssd_norm_g, w_out, norm_mem_g, mem_in_norm_g, w_mem_q, w_mem_kv, mem_q_norm_g, mem_k_norm_g, w_mem_o, norm_ffn_g, w_ffn_gate, w_ffn_up, ffn_conv_w, ffn_conv_b, w_ffn_down):
    depth = w_in.shape[0]
    yp, ys = x_prompt, x_sample
    kc_all = jnp.transpose(cache_fox_k, (0, 1, 3, 4, 2))
    vc_all = jnp.transpose(cache_fox_v, (0, 1, 3, 4, 2))
    lc_all = jnp.transpose(cache_fox_logf, (0, 1, 3, 2))
    pouts, souts = [], []
    for l in range(depth):
        p = _prep_layer(l, norm_mix_g, w_in, b_forget, fox_q_norm_g, fox_k_norm_g, ssd_conv_w, ssd_conv_b,
                        ssd_dt_bias, ssd_a_log, ssd_d, ssd_norm_g, w_out, norm_mem_g, mem_in_norm_g, w_mem_q,
                        w_mem_kv, mem_q_norm_g, mem_k_norm_g, w_mem_o, norm_ffn_g, w_ffn_gate, w_ffn_up,
                        ffn_conv_w, ffn_conv_b, w_ffn_down)
        po, so = _layer(yp, mem_prompt, ys, kc_all[l], vc_all[l], lc_all[l], state_ssm[l], state_ssm_conv[l],
                        state_ffn_conv[l], cache_mem_k[l], cache_mem_v[l], page_table, p)
        yp, ys = po[0], so[0]
        pouts.append(po[1:])
        souts.append(so[1:])
    stack = lambda outs, i: jnp.stack([o[i] for o in outs])
    return (yp, ys) + tuple(stack(pouts, i) for i in range(8)) + tuple(stack(souts, i) for i in range(6))
```

```python
import functools

import numpy as np
import jax
import jax.numpy as jnp
from jax import lax
from jax.experimental import pallas as pl
from jax.experimental.pallas import tpu as pltpu

F32 = jnp.float32
BF16 = jnp.bfloat16
EPS = 1e-6
LOG2E = 1.4426950408889634
FORGET_HEADS = 8
FOX_DIM = 64
SSD_HEADS = 8
SSD_DIM = 64
SSD_STATE = 128
SSD_CONV = 4
CHUNK = 128
MEM_HEADS = 4
MEM_DIM = 128
FFN_CONV = 3
LANES = 128
SUBLANES = 8
VMEM_LIMIT = 56 * 1024 * 1024
NT = (((1,), (1,)), ((), ()))
TN = (((0,), (0,)), ((), ()))


def _dot(a, b):
    return jnp.dot(a, b, preferred_element_type=F32)


def _dot_nt(a, b):
    return lax.dot_general(a, b, NT, preferred_element_type=F32)


def _dot_tn(a, b):
    return lax.dot_general(a, b, TN, preferred_element_type=F32)


def _split3(x):
    hi = x.astype(BF16)
    r1 = x - hi.astype(F32)
    mid = r1.astype(BF16)
    lo = (r1 - mid.astype(F32)).astype(BF16)
    return hi, mid, lo


def _dot3_left(x, m):
    hi, mid, lo = _split3(x)
    return _dot(hi, m) + _dot(mid, m) + _dot(lo, m)


def _dot3_right(m, x):
    hi, mid, lo = _split3(x)
    return _dot(m, hi) + _dot(m, mid) + _dot(m, lo)


def _dot2_left(x, m):
    hi = x.astype(BF16)
    lo = (x - hi.astype(F32)).astype(BF16)
    return _dot(hi, m) + _dot(lo, m)


def _dot2_right(m, x):
    hi = x.astype(BF16)
    lo = (x - hi.astype(F32)).astype(BF16)
    return _dot(m, hi) + _dot(m, lo)


def _rms(x, g):
    return x * lax.rsqrt(jnp.mean(x * x, axis=-1, keepdims=True) + EPS) * g


def _softplus(x):
    return jnp.maximum(x, 0.0) + jnp.log1p(jnp.exp(-jnp.abs(x)))


def _log_sigmoid(x):
    return -_softplus(-x)


def _silu(x):
    return x * (1.0 / (1.0 + jnp.exp(-x)))


def _iota(shape, dim):
    return lax.broadcasted_iota(jnp.int32, shape, dim)


def _div_pow2(x, d):
    assert d & (d - 1) == 0
    return lax.shift_right_logical(x, jnp.int32(d.bit_length() - 1))


def _cparams(sem):
    return pltpu.CompilerParams(dimension_semantics=sem, vmem_limit_bytes=VMEM_LIMIT)


def _const_spec(shape):
    n = len(shape)
    return pl.BlockSpec(shape, lambda *_: (0,) * n)


def _inproj_kernel(x_ref, g_ref, wq_ref, wkvt_ref, wzx_ref, wsm_ref, wsmt_ref, gq_ref, gkc_ref, red_ref, exp_ref,
                   bsm_ref, bsmt_ref,
                   q_ref, kt_ref, vt_ref, ktb_ref, vtb_ref, z_ref, xbc_ref, sm_ref, smt_ref):
    fw = FORGET_HEADS * FOX_DIM
    tm = x_ref.shape[0]
    xn = _rms(x_ref[...], g_ref[...]).astype(BF16)

    q = _dot(xn, wq_ref[...])
    rs = lax.rsqrt(_dot2_left(q * q, red_ref[...]) + EPS)
    q_ref[...] = (q * _dot2_left(rs, exp_ref[...]) * gq_ref[...] * (FOX_DIM ** -0.5 * LOG2E)).astype(BF16)

    kv = _dot_nt(wkvt_ref[...], xn)
    k3 = kv[0:fw, :].reshape(FORGET_HEADS, FOX_DIM, tm)
    k3 = k3 * lax.rsqrt(jnp.mean(k3 * k3, axis=1, keepdims=True) + EPS)
    kn = k3.reshape(fw, tm) * gkc_ref[...]
    kt_ref[...] = kn
    ktb_ref[...] = kn.astype(BF16)
    vt = kv[fw:, :]
    vt_ref[...] = vt
    vtb_ref[...] = vt.astype(BF16)

    z_ref[...] = _dot(xn, wzx_ref[:, 0:fw])
    xbc_ref[...] = _dot(xn, wzx_ref[:, fw:])
    sm = _dot(xn, wsm_ref[...]) + bsm_ref[...]
    sm_ref[...] = jnp.where(_iota(sm.shape, 1) < FORGET_HEADS, _log_sigmoid(sm), _softplus(sm))
    smt = _dot_nt(wsmt_ref[...], xn) + bsmt_ref[...]
    smt_ref[...] = jnp.where(_iota(smt.shape, 0) < FORGET_HEADS, _log_sigmoid(smt), _softplus(smt))


def _inproj(x, p, tm, batch, length):
    n, d = x.shape
    fw = FORGET_HEADS * FOX_DIM
    nzx = p['w_zx'].shape[1]
    per = length // tm
    row = lambda i: (i, 0)
    tr = lambda i: (i // per, 0, i % per)
    return pl.pallas_call(
        _inproj_kernel,
        grid=(n // tm,),
        in_specs=[
            pl.BlockSpec((tm, d), row),
            _const_spec((1, d)),
            _const_spec((d, fw)),
            _const_spec((2 * fw, d)),
            _const_spec((d, nzx)),
            _const_spec((d, LANES)),
            _const_spec((2 * SUBLANES, d)),
            _const_spec((1, fw)),
            _const_spec((fw, 1)),
            _const_spec((fw, LANES)),
            _const_spec((LANES, fw)),
            _const_spec((1, LANES)),
            _const_spec((2 * SUBLANES, 1)),
        ],
        out_specs=[
            pl.BlockSpec((tm, fw), row),
            pl.BlockSpec((None, fw, tm), tr),
            pl.BlockSpec((None, fw, tm), tr),
            pl.BlockSpec((None, fw, tm), tr),
            pl.BlockSpec((None, fw, tm), tr),
            pl.BlockSpec((tm, fw), row),
            pl.BlockSpec((tm, nzx - fw), row),
            pl.BlockSpec((tm, LANES), row),
            pl.BlockSpec((2 * SUBLANES, tm), lambda i: (0, i)),
        ],
        out_shape=[
            jax.ShapeDtypeStruct((n, fw), BF16),
            jax.ShapeDtypeStruct((batch, fw, length), F32),
            jax.ShapeDtypeStruct((batch, fw, length), F32),
            jax.ShapeDtypeStruct((batch, fw, length), BF16),
            jax.ShapeDtypeStruct((batch, fw, length), BF16),
            jax.ShapeDtypeStruct((n, fw), F32),
            jax.ShapeDtypeStruct((n, nzx - fw), F32),
            jax.ShapeDtypeStruct((n, LANES), F32),
            jax.ShapeDtypeStruct((2 * SUBLANES, n), F32),
        ],
        compiler_params=_cparams(("parallel",)),
        name="inproj",
    )(x, p['norm_mix_g'], p['w_q'], p['w_kvt'], p['w_zx'], p['w_sm'], p['w_smt'], p['gq'], p['gk_col'],
      p['head_reduce'], p['head_expand'], p['b_sm'], p['b_smt'])


def _cumsum_kernel(sm_ref, smt_ref, col_ref, row_ref):
    length = sm_ref.shape[0]
    r = _iota((LANES, LANES), 0)
    c = _iota((LANES, LANES), 1)
    lower = (c <= r).astype(BF16)
    upper = (r <= c).astype(BF16)
    carry_c = jnp.zeros((1, LANES), F32)
    carry_r = jnp.zeros((smt_ref.shape[0], 1), F32)
    for b in range(length // LANES):
        sl = slice(b * LANES, (b + 1) * LANES)
        cs = _dot3_right(lower, sm_ref[sl, :]) + carry_c
        col_ref[sl, :] = cs * LOG2E
        carry_c = cs[LANES - 1:LANES, :]
        rs = _dot3_left(smt_ref[:, sl], upper) + carry_r
        row_ref[:, sl] = rs * LOG2E
        carry_r = rs[:, LANES - 1:LANES]


def _cumsum(sm, smt, batch, length):
    n = sm.shape[0]
    return pl.pallas_call(
        _cumsum_kernel,
        grid=(batch,),
        in_specs=[pl.BlockSpec((length, LANES), lambda b: (b, 0)),
                  pl.BlockSpec((2 * SUBLANES, length), lambda b: (0, b))],
        out_specs=[pl.BlockSpec((length, LANES), lambda b: (b, 0)),
                   pl.BlockSpec((2 * SUBLANES, length), lambda b: (0, b))],
        out_shape=[jax.ShapeDtypeStruct((n, LANES), F32),
                   jax.ShapeDtypeStruct((2 * SUBLANES, n), F32)],
        compiler_params=_cparams(("parallel",)),
        name="logf_cumsum",
    )(sm, smt)


PAGES_PER_CHUNK = 32
KV_SLOTS = 2
STREAMS = 2


def _fox_kernel(pt_ref, pb_ref, ph_ref, pi_ref, pj_ref, pv_ref,
                qb_ref, knvn_ref, lft_ref, masks_ref, kc_ref, vc_ref, lc_ref,
                q_ref, kt_ref, vt_ref, cc_ref, cr_ref,
                sxzm_ref, ssmt_ref, sst_ref, stail_ref, rows_ref, acol_ref,
                pxbc_ref, pz_ref, psm_ref, psmt_ref,
                os_ref, op_ref, sfin_ref, py_ref, pfin_ref,
                kbuf, vbuf, lbuf, suf, m_s, l_s, acc_s, cn_s, knp, vnp, pm, pacc, pqa, xp, zp, smp, smtp,
                pstate, ptail, ksem, vsem, lsem,
                *, n_chunks, t_new, p_stride, p_chunks, p_per_seq):
    g = pl.program_id(0)
    n_steps = pl.num_programs(0)
    b = g // n_chunks
    c = g % n_chunks
    slot = g % KV_SLOTS
    ppc = PAGES_PER_CHUNK
    n_pages = n_chunks * ppc
    page = kbuf.shape[-1] // ppc
    heads = FORGET_HEADS
    fdim = heads * FOX_DIM

    def kv_copies(bb, cc, sl):
        cps = []
        for j in range(ppc):
            pg = pt_ref[bb, cc * ppc + j]
            cps.append(pltpu.make_async_copy(kc_ref.at[pg], kbuf.at[sl, :, :, pl.ds(j * page, page)], ksem.at[sl]))
            cps.append(pltpu.make_async_copy(vc_ref.at[pg], vbuf.at[sl, :, :, pl.ds(j * page, page)], vsem.at[sl]))
        return cps

    def lf_copies(bb, sl):
        return [pltpu.make_async_copy(lc_ref.at[pt_ref[bb, j]], lbuf.at[sl, pl.ds(j * heads, heads), :], lsem.at[sl])
                for j in range(n_pages)]

    @pl.when(g == 0)
    def _():
        for cp in lf_copies(0, 0):
            cp.start()
        for ahead in range(KV_SLOTS - 1):
            for cp in kv_copies(ahead // n_chunks, ahead % n_chunks, ahead):
                cp.start()
        pm[...] = jnp.zeros_like(pm)
        pacc[...] = jnp.zeros_like(pacc)
        pqa[...] = jnp.zeros_like(pqa)

    last_c = c == n_chunks - 1
    g_ahead = g + (KV_SLOTS - 1)

    @pl.when(g_ahead < n_steps)
    def _():
        for cp in kv_copies(g_ahead // n_chunks, g_ahead % n_chunks, g_ahead % KV_SLOTS):
            cp.start()

    bslot = b % 2

    @pl.when(jnp.logical_and(c == 0, g + n_chunks < n_steps))
    def _():
        for cp in lf_copies(b + 1, 1 - bslot):
            cp.start()

    @pl.when(c == 0)
    def _():
        pltpu.make_async_copy(lbuf.at[1 - bslot], lbuf.at[bslot], lsem.at[bslot]).wait()
        lf = lbuf[bslot]
        nr = lf.shape[0]
        later = masks_ref[nr:nr + page, 0:page]
        lpages = masks_ref[0:nr, 0:nr]
        within = _dot2_left(lf, later)
        tot = within[:, 0:1] + lf[:, 0:1]
        suf[...] = (within + _dot2_right(lpages, jnp.broadcast_to(tot, lf.shape))) * LOG2E
        rl = _iota((LANES, LANES), 0)
        cl = _iota((LANES, LANES), 1)
        cn_s[...] = _dot3_left(lft_ref[...], (rl <= cl).astype(BF16)) * LOG2E
        m_s[...] = jnp.full(m_s.shape, -jnp.inf, F32)
        l_s[...] = jnp.zeros(l_s.shape, F32)
        acc_s[...] = jnp.zeros(acc_s.shape, F32)

    cn = cn_s[...]
    cn_col = jnp.concatenate([cn[:, t:t + 1] for t in range(t_new)], axis=0)
    qb = qb_ref[...]

    other = (g + 1) % KV_SLOTS
    pltpu.make_async_copy(kbuf.at[other], kbuf.at[slot], ksem.at[slot]).wait()
    pltpu.make_async_copy(vbuf.at[other], vbuf.at[slot], vsem.at[slot]).wait()

    def online(st, s, pv):
        m = m_s[st]
        m_new = jnp.maximum(m, jnp.max(s, axis=1, keepdims=True))
        alpha = jnp.exp2(m - m_new)
        p = jnp.exp2(s - m_new)
        l_s[st] = alpha * l_s[st] + jnp.sum(p, axis=1, keepdims=True)
        acc_s[st] = alpha * acc_s[st] + pv(p.astype(BF16))
        m_s[st] = m_new

    def sample_chunk():
        hpc = ppc // STREAMS
        wdt = hpc * page
        scores = []
        for st in range(STREAMS):
            kt = kbuf[slot, :, :, pl.ds(st * wdt, wdt)].reshape(fdim, wdt).astype(BF16)
            sfx = jnp.concatenate([suf[pl.ds((c * ppc + st * hpc + j) * heads, heads), :] for j in range(hpc)],
                                  axis=1)
            scores.append(_dot(qb, kt) + jnp.concatenate([sfx] * t_new, axis=0) + cn_col)
        for st in range(STREAMS):
            vt = vbuf[slot, :, :, pl.ds(st * wdt, wdt)].reshape(fdim, wdt).astype(BF16)
            online(st, scores[st], lambda p: _dot_nt(p, vt))

    hp = ph_ref[g]
    pi = pi_ref[g]
    pj = pj_ref[g]
    kind = pv_ref[g]
    tq = q_ref.shape[0]
    first = _iota((1, LANES), 1) < FOX_DIM

    def prompt_pair(masked):
        kt = kt_ref[...]
        vt = vt_ref[...]
        cr = cr_ref[...]
        tk = kt.shape[1]
        half = FOX_DIM
        fresh = pj == 0

        def terms(x):
            hi, mid, lo = _split3(x)
            return hi.astype(F32), mid.astype(F32), lo.astype(F32)

        @pl.when(fresh)
        def _():
            cc = cc_ref[...]
            qf = q_ref[...].astype(F32)
            lane = _iota((tq, LANES), 1)
            for hh in range(2):
                cq = jnp.sum(jnp.where(lane == 2 * hp + hh, cc, 0.0), axis=1, keepdims=True)
                hi, mid, lo = terms(cq)
                e0 = half * (1 - hh)
                ext = jnp.where(lane == e0, hi, jnp.where(lane == e0 + 1, mid, jnp.where(lane == e0 + 2, lo,
                      jnp.where(jnp.logical_and(lane >= e0 + 3, lane < e0 + 6), 1.0, 0.0))))
                own = (lane < half) if hh == 0 else (lane >= half)
                pqa[hh] = jnp.where(own, qf, ext).astype(BF16)

        sub = _iota(cr.shape, 0)
        r16 = _iota((2 * SUBLANES, tk), 0)
        pad = jnp.zeros((half - 2 * SUBLANES, tk), BF16)
        ones_row = jnp.where(r16 == 0, 1.0, 0.0).astype(BF16)
        wide = lambda a: jnp.concatenate([a] * (tk // LANES), axis=1)
        outs = []
        for hh in range(2):
            ck = jnp.sum(jnp.where(sub == 2 * hp + hh, cr, 0.0), axis=0, keepdims=True)
            hi, mid, lo = terms(ck)
            kext = jnp.where(r16 < 3, 1.0, jnp.where(r16 == 3, -hi, jnp.where(r16 == 4, -mid,
                   jnp.where(r16 == 5, -lo, 0.0)))).astype(BF16)
            if hh == 0:
                kta = jnp.concatenate([kt[0:half, :], kext, pad], axis=0)
                vta = jnp.concatenate([vt[0:half, :], ones_row, pad], axis=0)
            else:
                kta = jnp.concatenate([kext, pad, kt[half:, :]], axis=0)
                vta = jnp.concatenate([ones_row, pad, vt[half:, :]], axis=0)
            s = _dot(pqa[hh], kta)
            if masked:
                s = jnp.where(_iota(s.shape, 1) + (pj * tk - pi * tq) <= _iota(s.shape, 0), s, -jnp.inf)
            m = jnp.where(fresh, -jnp.inf, pm[hh])
            acc = jnp.where(fresh, 0.0, pacc[hh])
            m_new = jnp.maximum(m, jnp.max(s, axis=1, keepdims=True))
            p = jnp.exp2(s - wide(m_new)).astype(BF16)
            acc = jnp.exp2(m - m_new) * acc + _dot_nt(p, vta)
            if masked:
                e0 = half * (1 - hh)
                outs.append(acc / acc[:, e0:e0 + 1])
            else:
                pm[hh] = m_new
                pacc[hh] = acc
        if masked:
            op_ref[...] = jnp.where(first, outs[0], outs[1])

    sample_chunk()

    @pl.when(kind == 2)
    def _():
        prompt_pair(True)

    @pl.when(kind == 1)
    def _():
        prompt_pair(False)

    p_idx = g // p_stride

    @pl.when(jnp.logical_and(g % p_stride == 0, p_idx < p_chunks))
    def _():
        _ssd_prompt_chunk(p_idx % p_per_seq, p_per_seq - 1, pxbc_ref, pz_ref, psm_ref, psmt_ref,
                          (rows_ref, acol_ref), py_ref, pfin_ref, pstate, ptail)

    @pl.when(c == min(1, n_chunks - 1))
    def _():
        cdim, hw = xp.shape[1], zp.shape[1]
        os_ref[:, fdim:] = _ssd_short_chunk(
            sxzm_ref[:, 0:cdim], sxzm_ref[:, cdim:cdim + hw], sxzm_ref[:, cdim + hw:], ssmt_ref[...],
            (xp, zp, smp, smtp), stail_ref[...], sst_ref, sfin_ref, (rows_ref, acol_ref))

    @pl.when(last_c)
    def _():
        knp[...] = jnp.zeros_like(knp)
        vnp[...] = jnp.zeros_like(vnp)
        knp[0:t_new, :] = knvn_ref[:, 0:fdim]
        vnp[0:t_new, :] = knvn_ref[:, fdim:]
        kn = knp[...].astype(BF16)
        vn = vnp[...].astype(BF16)
        s2 = _dot_nt(qb, kn) + cn_col - jnp.concatenate([cn] * t_new, axis=0)
        tok = _div_pow2(_iota(s2.shape, 0), heads)
        s2 = jnp.where(_iota(s2.shape, 1) <= tok, s2, -jnp.inf)
        online(0, s2, lambda p: _dot(p, vn))
        m_all = functools.reduce(jnp.maximum, [m_s[st] for st in range(STREAMS)])
        wts = [jnp.exp2(m_s[st] - m_all) for st in range(STREAMS)]
        o = (sum(wts[st] * acc_s[st] for st in range(STREAMS))
             / sum(wts[st] * l_s[st] for st in range(STREAMS)))
        keep = _div_pow2(_iota((heads, fdim), 1), FOX_DIM) == _iota((heads, fdim), 0)
        os_ref[:, 0:fdim] = jnp.concatenate(
            [jnp.sum(jnp.where(keep, o[t * heads:(t + 1) * heads, :], 0.0), axis=0, keepdims=True)
             for t in range(t_new)], axis=0)


def _pair_schedule(batch, n_hp, nq, tq, tk, n_steps, n_chunks):
    pairs = []
    for b in range(batch):
        for h in range(n_hp):
            for i in range(nq):
                j_last = ((i + 1) * tq - 1) // tk
                pairs += [(b, h, i, j, 2 if j == j_last else 1) for j in range(j_last + 1)]
    n_idle = n_steps - len(pairs)
    assert n_idle >= 0, "more prompt attention pairs than sample chunks"
    steps = np.arange(n_steps)
    chunk = steps % n_chunks
    seq = steps // n_chunks
    first_chunk = chunk == 0
    n_seq = n_steps // n_chunks
    if n_idle <= n_seq:
        pick = (seq * n_idle) // n_seq != ((seq + 1) * n_idle) // n_seq
        cost = np.where(np.logical_and(first_chunk, pick), 0, 1)
    else:
        cost = np.where(first_chunk, 0, np.where(chunk == n_chunks - 1, 1 + seq % 2, 3))
    idle = np.zeros(n_steps, bool)
    idle[np.argsort(cost, kind="stable")[:n_idle]] = True
    rows, k = [], 0
    for g in range(n_steps):
        if idle[g]:
            rows.append(pairs[min(k, len(pairs) - 1)][:4] + (0,))
        else:
            rows.append(pairs[k])
            k += 1
    return [jnp.asarray(np.array(col, np.int32)) for col in zip(*rows)]


def _fox(page_table, qblk, knvn, lft, kc, vc, lc, q, ktb, vtb, ccol, crow4, tq, tk,
         sxzm, smt4_s, st0, tail0, xbc_p, z_p, sm_p, smt4_p, p):
    cdim = tail0.shape[2]
    hw = st0.shape[1]
    bs, rows, fdim = qblk.shape
    t_new = rows // FORGET_HEADS
    n_pages = page_table.shape[1]
    page = kc.shape[-1]
    n_chunks = n_pages // PAGES_PER_CHUNK
    pc = PAGES_PER_CHUNK * page
    n_steps = bs * n_chunks
    bp, _, length = ktb.shape
    nq = length // tq
    n_hp = fdim // LANES
    sched = _pair_schedule(bp, n_hp, nq, tq, tk, n_steps, n_chunks)
    pos = np.arange(page)
    rows_ph = np.arange(n_pages * FORGET_HEADS)
    nr = rows_ph.size
    masks_np = np.zeros((nr + page, max(nr, page)), np.float32)
    masks_np[:nr, :nr] = ((rows_ph[None, :] % FORGET_HEADS == rows_ph[:, None] % FORGET_HEADS)
                          & (rows_ph[None, :] // FORGET_HEADS > rows_ph[:, None] // FORGET_HEADS))
    masks_np[nr:, :page] = pos[:, None] > pos[None, :]
    masks = jnp.asarray(masks_np, BF16)
    per_s = lambda g, *_: (g // n_chunks, 0, 0)
    p_per_seq = length // CHUNK
    p_chunks = bp * p_per_seq
    assert p_chunks <= n_steps, "more prompt SSD chunks than grid steps"
    p_stride = n_steps // p_chunks
    p_of = lambda g: jnp.minimum(g // p_stride, p_chunks - 1)
    p_blk = lambda g, *_: (p_of(g) // p_per_seq, p_of(g) % p_per_seq, 0)
    grid_spec = pltpu.PrefetchScalarGridSpec(
        num_scalar_prefetch=6,
        grid=(n_steps,),
        in_specs=[
            pl.BlockSpec((None, rows, fdim), per_s),
            pl.BlockSpec((None, t_new, 2 * fdim), per_s),
            pl.BlockSpec((None, FORGET_HEADS, LANES), per_s),
            pl.BlockSpec(masks.shape, lambda g, *_: (0, 0)),
            pl.BlockSpec(memory_space=pl.ANY),
            pl.BlockSpec(memory_space=pl.ANY),
            pl.BlockSpec(memory_space=pl.ANY),
            pl.BlockSpec((tq, LANES), lambda g, pt, pb, ph, pi, pj, pv: (pb[g] * nq + pi[g], ph[g])),
            pl.BlockSpec((None, LANES, tk), lambda g, pt, pb, ph, pi, pj, pv: (pb[g], ph[g], pj[g])),
            pl.BlockSpec((None, LANES, tk), lambda g, pt, pb, ph, pi, pj, pv: (pb[g], ph[g], pj[g])),
            pl.BlockSpec((tq, LANES), lambda g, pt, pb, ph, pi, pj, pv: (pb[g] * nq + pi[g], 0)),
            pl.BlockSpec((None, None, 2 * SUBLANES, tk), lambda g, pt, pb, ph, pi, pj, pv: (pb[g], pj[g], 0, 0)),
            pl.BlockSpec((None, t_new, cdim + hw + LANES), per_s),
            pl.BlockSpec((None, None, 2 * SUBLANES, t_new), lambda g, *_: (g // n_chunks, 0, 0, 0)),
            pl.BlockSpec((None, hw, SSD_STATE), per_s),
            pl.BlockSpec((None, SUBLANES, cdim), per_s),
        ] + _ssd_const_specs(cdim) + [
            pl.BlockSpec((None, CHUNK, cdim), p_blk),
            pl.BlockSpec((None, CHUNK, hw), p_blk),
            pl.BlockSpec((None, CHUNK, LANES), p_blk),
            pl.BlockSpec((None, None, 2 * SUBLANES, CHUNK), lambda g, *_: p_blk(g) + (0,)),
        ],
        out_specs=[
            pl.BlockSpec((None, t_new, fdim + hw), per_s),
            pl.BlockSpec((tq, LANES), lambda g, pt, pb, ph, pi, pj, pv: (pb[g] * nq + pi[g], ph[g])),
            pl.BlockSpec((None, hw, SSD_STATE), per_s),
            pl.BlockSpec((None, CHUNK, hw), p_blk),
            pl.BlockSpec((None, hw, SSD_STATE), lambda g, *_: (p_of(g) // p_per_seq, 0, 0)),
        ],
        scratch_shapes=[
            pltpu.VMEM((KV_SLOTS, FORGET_HEADS, FOX_DIM, pc), F32),
            pltpu.VMEM((KV_SLOTS, FORGET_HEADS, FOX_DIM, pc), F32),
            pltpu.VMEM((2, n_pages * FORGET_HEADS, page), F32),
            pltpu.VMEM((n_pages * FORGET_HEADS, page), F32),
            pltpu.VMEM((STREAMS, rows, 1), F32),
            pltpu.VMEM((STREAMS, rows, 1), F32),
            pltpu.VMEM((STREAMS, rows, fdim), F32),
            pltpu.VMEM((FORGET_HEADS, LANES), F32),
            pltpu.VMEM((LANES, fdim), F32),
            pltpu.VMEM((LANES, fdim), F32),
            pltpu.VMEM((2, tq, LANES), F32),
            pltpu.VMEM((2, tq, LANES), F32),
            pltpu.VMEM((2, tq, LANES), BF16),
        ] + _ssd_pad_scratch(cdim, hw) + [
            pltpu.VMEM((hw, SSD_STATE), F32),
            pltpu.VMEM((SUBLANES, cdim), F32),
            pltpu.SemaphoreType.DMA((KV_SLOTS,)),
            pltpu.SemaphoreType.DMA((KV_SLOTS,)),
            pltpu.SemaphoreType.DMA((2,)),
        ],
    )
    return pl.pallas_call(
        functools.partial(_fox_kernel, n_chunks=n_chunks, t_new=t_new, p_stride=p_stride, p_chunks=p_chunks,
                          p_per_seq=p_per_seq),
        grid_spec=grid_spec,
        out_shape=[jax.ShapeDtypeStruct((bs, t_new, fdim + hw), F32),
                   jax.ShapeDtypeStruct((bp * length, fdim), F32),
                   jax.ShapeDtypeStruct((bs, hw, SSD_STATE), F32),
                   jax.ShapeDtypeStruct((bp, length, hw), F32),
                   jax.ShapeDtypeStruct((bp, hw, SSD_STATE), F32)],
        compiler_params=_cparams(("arbitrary",)),
        name="fox",
    )(page_table, *sched, qblk, knvn, lft, masks, kc, vc, lc, q, ktb, vtb, ccol, crow4,
      sxzm, smt4_s, st0, tail0, *_ssd_consts(p), xbc_p, z_p, sm_p, smt4_p)


def _ssd_chunk(x, z, sm, smt, prev, state_in, state_out, rows_ref, acol_ref):
    q = CHUNK
    hw = SSD_HEADS * SSD_DIM
    gs = SSD_STATE
    cw_ref = rows_ref.at[0:SSD_CONV]
    cb_ref = rows_ref.at[SSD_CONV:SSD_CONV + 1]
    dsk_ref = rows_ref.at[SSD_CONV + 1:SSD_CONV + 2, 0:hw]
    ng_ref = rows_ref.at[SSD_CONV + 1:SSD_CONV + 2, hw:2 * hw]
    arow_ref = rows_ref.at[SSD_CONV + 2:SSD_CONV + 3, 0:LANES]

    row8 = _iota(prev.shape, 0)
    conv = cb_ref[...] + cw_ref[SSD_CONV - 1:SSD_CONV, :] * x
    for sh in range(1, SSD_CONV):
        rx = pltpu.roll(x, sh, 0)
        fix = jnp.where(row8 < sh, pltpu.roll(prev, sh, 0), rx[0:SUBLANES, :])
        xs_sh = jnp.concatenate([fix, rx[SUBLANES:, :]], axis=0)
        conv = conv + cw_ref[SSD_CONV - 1 - sh:SSD_CONV - sh, :] * xs_sh
    xc = _silu(conv)
    xs = xc[:, 0:hw]

    r = _iota((q, q), 0)
    cidx = _iota((q, q), 1)
    tril = cidx <= r
    lower = tril.astype(BF16)
    upper = (r <= cidx).astype(BF16)
    acum_c = _dot3_right(lower, sm * arow_ref[...])
    acum_r = _dot3_left(smt * acol_ref[...], upper)
    lane = _iota((1, LANES), 1)
    first = lane < SSD_DIM
    rowp = _iota((LANES, 1), 0) < SSD_DIM

    ys = []
    for pr in range(SSD_HEADS // 2):
        g = pr // (SSD_HEADS // 4)
        bg = xc[:, hw + g * gs: hw + (g + 1) * gs].astype(BF16)
        cg = xc[:, hw + 2 * gs + g * gs: hw + 2 * gs + (g + 1) * gs].astype(BF16)
        gmat = _dot_nt(cg, bg)
        xpair = xs[:, pr * LANES:(pr + 1) * LANES]
        xpb = xpair.astype(BF16)
        yd, te, ea, cd = [], [], [], []
        for hh in range(2):
            h = FORGET_HEADS + 2 * pr + hh
            ac = acum_c[:, h:h + 1]
            ar = acum_r[h:h + 1, :]
            dtr = smt[h:h + 1, :]
            dtc = sm[:, h:h + 1]
            last = acum_c[q - 1:q, h:h + 1]
            decay = jnp.exp(jnp.where(tril, ac - ar, -jnp.inf))
            sc = gmat * decay * dtr
            yd.append(_dot(sc.astype(BF16), xpb))
            te.append(jnp.exp(last - ac) * dtc)
            ea.append(jnp.exp(ac))
            cd.append(jnp.exp(last))
        ydiag = jnp.where(first, yd[0], yd[1])
        xsc = (xpair * jnp.where(first, te[0], te[1])).astype(BF16)
        cstate = _dot_tn(xsc, bg)
        prev_st = state_in[pr * LANES:(pr + 1) * LANES, :]
        yoff = _dot_nt(cg, prev_st.astype(BF16)) * jnp.where(first, ea[0], ea[1])
        state_out[pr * LANES:(pr + 1) * LANES, :] = prev_st * jnp.where(rowp, cd[0], cd[1]) + cstate
        ys.append(ydiag + yoff + dsk_ref[:, pr * LANES:(pr + 1) * LANES] * xpair)
    y = jnp.concatenate(ys, axis=1) * _silu(z)
    return _rms(y, ng_ref[...])


def _ssd_short_chunk(x, z, sm, smt, pads, prev, state_in, state_out, consts):
    t_real = x.shape[0]
    xp, zp, smp, smtp = pads
    xp[...] = jnp.zeros_like(xp)
    zp[...] = jnp.zeros_like(zp)
    smp[...] = jnp.zeros_like(smp)
    smtp[...] = jnp.zeros_like(smtp)
    xp[0:t_real, :] = x
    zp[0:t_real, :] = z
    smp[0:t_real, :] = sm
    smtp[:, 0:t_real] = smt
    y = _ssd_chunk(xp[...], zp[...], smp[...], smtp[...], prev, state_in, state_out, *consts)
    return y[0:t_real, :]


def _ssd_pad_scratch(cdim, hw):
    return [pltpu.VMEM((CHUNK, cdim), F32), pltpu.VMEM((CHUNK, hw), F32),
            pltpu.VMEM((CHUNK, LANES), F32), pltpu.VMEM((2 * SUBLANES, CHUNK), F32)]


def _ssd_param_rows(conv_w, conv_b, d_row, norm_g, a_row):
    cdim = conv_w.shape[1]
    padl = lambda v: jnp.pad(v, ((0, 0), (0, cdim - v.shape[1])))
    rows = [conv_w, conv_b, jnp.concatenate([d_row, norm_g], axis=1), padl(a_row)]
    used = sum(r.shape[0] for r in rows)
    return jnp.concatenate(rows + [jnp.zeros((SUBLANES - used, cdim), F32)], axis=0)


def _ssd_const_specs(cdim):
    return [_const_spec((SUBLANES, cdim)), _const_spec((2 * SUBLANES, 1))]


def _ssd_consts(p):
    return (p['ssd_rows'], p['a_col'])


def _ssd_prompt_chunk(c, last, xbc_ref, z_ref, sm_ref, smt_ref, consts, y_ref, fin_ref, state, tail):
    @pl.when(c == 0)
    def _():
        state[...] = jnp.zeros_like(state)
        tail[...] = jnp.zeros_like(tail)

    x = xbc_ref[...]
    y_ref[...] = _ssd_chunk(x, z_ref[...], sm_ref[...], smt_ref[...], tail[...], state, state, *consts)
    tail[...] = x[CHUNK - SUBLANES:, :]

    @pl.when(c == last)
    def _():
        fin_ref[...] = state[...]


def _mem_kv_kernel(m_ref, g_ref, w_ref, gk_ref, k_ref, v_ref):
    mw = MEM_HEADS * MEM_DIM
    xn = _rms(m_ref[...], g_ref[...]).astype(BF16)
    kv = _dot(xn, w_ref[...])
    for h in range(MEM_HEADS):
        sl = slice(h * MEM_DIM, (h + 1) * MEM_DIM)
        k_ref[:, sl] = _rms(kv[:, sl], gk_ref[...])
    v_ref[...] = kv[:, mw:]


def _mem_kv(mem, p, tm):
    n, d = mem.shape
    mw = MEM_HEADS * MEM_DIM
    row = lambda i: (i, 0)
    return pl.pallas_call(
        _mem_kv_kernel,
        grid=(n // tm,),
        in_specs=[pl.BlockSpec((tm, d), row), _const_spec((1, d)), _const_spec((d, 2 * mw)),
                  _const_spec((1, MEM_DIM))],
        out_specs=[pl.BlockSpec((tm, mw), row), pl.BlockSpec((tm, mw), row)],
        out_shape=[jax.ShapeDtypeStruct((n, mw), F32), jax.ShapeDtypeStruct((n, mw), F32)],
        compiler_params=_cparams(("parallel",)),
        name="mem_kv",
    )(mem, p['mem_in_norm_g'], p['w_mem_kv'], p['mem_k_norm_g'])


def _mix_out(x_ref, mix_refs, wout_ref, g_ref, wq_ref, gq_ref):
    y1 = x_ref[...]
    lo = 0
    for m_ref in mix_refs:
        hi = lo + m_ref.shape[-1]
        y1 = y1 + _dot(m_ref[...].astype(BF16), wout_ref[lo:hi, :])
        lo = hi
    xn = _rms(y1, g_ref[...]).astype(BF16)
    q = _dot(xn, wq_ref[...])
    qs = []
    for h in range(MEM_HEADS):
        qh = _rms(q[:, h * MEM_DIM:(h + 1) * MEM_DIM], gq_ref[...]) * MEM_DIM ** -0.5
        qs.append(qh)
    return y1, qs


def _memattn_prompt_kernel(x_ref, fo_ref, so_ref, wout_ref, g_ref, wq_ref, gq_ref, mk_ref, mv_ref, wo_ref, o_ref):
    y1, qs = _mix_out(x_ref, (fo_ref, so_ref), wout_ref, g_ref, wq_ref, gq_ref)
    outs = []
    for h in range(MEM_HEADS):
        sl = slice(h * MEM_DIM, (h + 1) * MEM_DIM)
        s = _dot_nt(qs[h].astype(BF16), mk_ref[:, sl].astype(BF16))
        p = jnp.exp(s - jnp.max(s, axis=1, keepdims=True))
        o = _dot(p.astype(BF16), mv_ref[:, sl].astype(BF16)) / jnp.sum(p, axis=1, keepdims=True)
        outs.append(o.astype(BF16))
    o_ref[...] = y1 + _dot(jnp.concatenate(outs, axis=1), wo_ref[...])


def _memattn_prompt(x, fo, so, mk, mv, p, tm, length, n_mem):
    n, d = x.shape
    fw = fo.shape[1]
    mw = MEM_HEADS * MEM_DIM
    per = length // tm
    row = lambda i: (i, 0)
    return pl.pallas_call(
        _memattn_prompt_kernel,
        grid=(n // tm,),
        in_specs=[pl.BlockSpec((tm, d), row), pl.BlockSpec((tm, fw), row), pl.BlockSpec((tm, fw), row),
                  _const_spec((2 * fw, d)), _const_spec((1, d)), _const_spec((d, mw)), _const_spec((1, MEM_DIM)),
                  pl.BlockSpec((n_mem, mw), lambda i: (i // per, 0)),
                  pl.BlockSpec((n_mem, mw), lambda i: (i // per, 0)),
                  _const_spec((mw, d))],
        out_specs=pl.BlockSpec((tm, d), row),
        out_shape=jax.ShapeDtypeStruct((n, d), F32),
        compiler_params=_cparams(("parallel",)),
        name="memattn_prompt",
    )(x, fo, so, p['w_out'], p['norm_mem_g'], p['w_mem_q'], p['mem_q_norm_g'], mk, mv, p['w_mem_o'])


def _memattn_sample_kernel(x_ref, fs_ref, wout_ref, g_ref, wq_ref, gq_ref, mk_ref, mv_ref, wo_ref, o_ref,
                           *, t_new):
    y1, qs = _mix_out(x_ref, (fs_ref,), wout_ref, g_ref, wq_ref, gq_ref)
    tm = y1.shape[0]
    per = SUBLANES // t_new
    nrow = MEM_HEADS * SUBLANES
    ncol = mk_ref.shape[1]
    row = _iota((nrow, 1), 0)
    mine = [_div_pow2(lax.bitwise_and(row, jnp.int32(SUBLANES - 1)), t_new) == u for u in range(per)]
    same_head = (lax.bitwise_and(_iota((nrow, ncol), 1), jnp.int32(MEM_HEADS - 1))
                 == _div_pow2(_iota((nrow, ncol), 0), SUBLANES))
    tiles = []
    for r in range(tm // SUBLANES):
        qst = jnp.concatenate([qs[h][r * SUBLANES:(r + 1) * SUBLANES, :] for h in range(MEM_HEADS)], axis=0)
        qst = qst.astype(BF16)
        s = None
        for u in range(per):
            su = _dot_nt(qst, mk_ref[r * per + u].astype(BF16))
            s = su if s is None else jnp.where(mine[u], su, s)
        s = jnp.where(same_head, s, -jnp.inf)
        p = jnp.exp(s - jnp.max(s, axis=1, keepdims=True))
        pb = p.astype(BF16)
        o = None
        for u in range(per):
            ou = _dot(pb, mv_ref[r * per + u].astype(BF16))
            o = ou if o is None else jnp.where(mine[u], ou, o)
        o = o / jnp.sum(p, axis=1, keepdims=True)
        tiles.append(jnp.concatenate([o[h * SUBLANES:(h + 1) * SUBLANES, :] for h in range(MEM_HEADS)], axis=1))
    o_ref[...] = y1 + _dot(jnp.concatenate(tiles, axis=0).astype(BF16), wo_ref[...])


def _memattn_sample(x, fs, mk, mv, p, tm, t_new):
    n, d = x.shape
    mixw = fs.shape[1]
    mw = MEM_HEADS * MEM_DIM
    n_flat = mk.shape[1]
    bb = tm // t_new
    row = lambda i: (i, 0)
    return pl.pallas_call(
        functools.partial(_memattn_sample_kernel, t_new=t_new),
        grid=(n // tm,),
        in_specs=[pl.BlockSpec((tm, d), row), pl.BlockSpec((tm, mixw), row),
                  _const_spec((mixw, d)), _const_spec((1, d)), _const_spec((d, mw)), _const_spec((1, MEM_DIM)),
                  pl.BlockSpec((bb, n_flat, MEM_DIM), lambda i: (i, 0, 0)),
                  pl.BlockSpec((bb, n_flat, MEM_DIM), lambda i: (i, 0, 0)),
                  _const_spec((mw, d))],
        out_specs=pl.BlockSpec((tm, d), row),
        out_shape=jax.ShapeDtypeStruct((n, d), F32),
        compiler_params=_cparams(("parallel",)),
        name="memattn_sample",
    )(x, fs, p['w_out'], p['norm_mem_g'], p['w_mem_q'], p['mem_q_norm_g'], mk, mv, p['w_mem_o'])


FF_CHUNK = 1024


def _ff_chunks(dff):
    return [(s, min(s + FF_CHUNK, dff)) for s in range(0, dff, FF_CHUNK)]


def _ffn_prompt_kernel(x_ref, g_ref, wg_ref, wu_ref, cw_ref, cb_ref, wd_ref, o_ref, tail_ref, halo, *, per):
    i = pl.program_id(0)
    tm = x_ref.shape[0]
    dff = wg_ref.shape[1]

    @pl.when(i % per == 0)
    def _():
        halo[...] = jnp.zeros_like(halo)

    x = x_ref[...]
    xn = _rms(x, g_ref[...]).astype(BF16)
    row8 = _iota((SUBLANES, 1), 0)
    acc = x
    for lo, hi in _ff_chunks(dff):
        gate = _dot(xn, wg_ref[:, lo:hi])
        up = _dot(xn, wu_ref[:, lo:hi])
        prev = halo[:, lo:hi]
        conv = cb_ref[:, lo:hi] + cw_ref[FFN_CONV - 1:FFN_CONV, lo:hi] * gate
        for sh in range(1, FFN_CONV):
            rg = pltpu.roll(gate, sh, 0)
            fix = jnp.where(row8 < sh, pltpu.roll(prev, sh, 0), rg[0:SUBLANES, :])
            conv = conv + cw_ref[FFN_CONV - 1 - sh:FFN_CONV - sh, lo:hi] * jnp.concatenate([fix, rg[SUBLANES:, :]], axis=0)
        halo[:, lo:hi] = gate[tm - SUBLANES:tm, :]
        hmid = (_silu(conv) * up).astype(BF16)
        acc = acc + _dot(hmid, wd_ref[lo:hi, :])
    o_ref[...] = acc
    tail_ref[...] = halo[...]


def _ffn_prompt(x, p, tm, batch, length):
    n, d = x.shape
    dff = p['w_ffn_gate'].shape[1]
    per = length // tm
    row = lambda i: (i, 0)
    return pl.pallas_call(
        functools.partial(_ffn_prompt_kernel, per=per),
        grid=(n // tm,),
        in_specs=[pl.BlockSpec((tm, d), row), _const_spec((1, d)),
                  pl.BlockSpec((d, dff), lambda i: (0, 0), pipeline_mode=pl.Buffered(1)),
                  pl.BlockSpec((d, dff), lambda i: (0, 0), pipeline_mode=pl.Buffered(1)),
                  _const_spec((FFN_CONV, dff)), _const_spec((1, dff)),
                  pl.BlockSpec((dff, d), lambda i: (0, 0), pipeline_mode=pl.Buffered(1))],
        out_specs=[pl.BlockSpec((tm, d), row),
                   pl.BlockSpec((None, SUBLANES, dff), lambda i: (i // per, 0, 0))],
        out_shape=[jax.ShapeDtypeStruct((n, d), F32),
                   jax.ShapeDtypeStruct((batch, SUBLANES, dff), F32)],
        scratch_shapes=[pltpu.VMEM((SUBLANES, dff), F32)],
        compiler_params=_cparams(("arbitrary",)),
        name="ffn_prompt",
    )(x, p['norm_ffn_g'], p['w_ffn_gate'], p['w_ffn_up'], p['ffn_conv_w'], p['ffn_conv_b'], p['w_ffn_down'])


def _ffn_sample_kernel(x_ref, buf_ref, g_ref, wg_ref, wu_ref, cw_ref, cb_ref, wd_ref, o_ref, nbuf_ref, *, t_new):
    nb = x_ref.shape[0] // t_new
    dff = wg_ref.shape[1]
    x = x_ref[...]
    xn = _rms(x, g_ref[...]).astype(BF16)
    acc = x
    for lo, hi in _ff_chunks(dff):
        gate = _dot(xn, wg_ref[:, lo:hi])
        up = _dot(xn, wu_ref[:, lo:hi])
        ext = jnp.concatenate([buf_ref[:, lo:hi], gate], axis=0)
        conv = cb_ref[:, lo:hi]
        for j in range(FFN_CONV):
            conv = conv + cw_ref[j:j + 1, lo:hi] * ext[j * nb:(j + t_new) * nb, :]
        nbuf_ref[:, lo:hi] = ext[t_new * nb:, :]
        hmid = (_silu(conv) * up).astype(BF16)
        acc = acc + _dot(hmid, wd_ref[lo:hi, :])
    o_ref[...] = acc


def _ffn_sample(x_tm, buf_tm, p, t_new):
    n, d = x_tm.shape
    dff = p['w_ffn_gate'].shape[1]
    nbr = buf_tm.shape[0]
    return pl.pallas_call(
        functools.partial(_ffn_sample_kernel, t_new=t_new),
        grid=(1,),
        in_specs=[_const_spec((n, d)), _const_spec((nbr, dff)), _const_spec((1, d)),
                  pl.BlockSpec((d, dff), lambda i: (0, 0), pipeline_mode=pl.Buffered(1)),
                  pl.BlockSpec((d, dff), lambda i: (0, 0), pipeline_mode=pl.Buffered(1)),
                  _const_spec((FFN_CONV, dff)), _const_spec((1, dff)),
                  pl.BlockSpec((dff, d), lambda i: (0, 0), pipeline_mode=pl.Buffered(1))],
        out_specs=[_const_spec((n, d)), _const_spec((nbr, dff))],
        out_shape=[jax.ShapeDtypeStruct((n, d), F32), jax.ShapeDtypeStruct((nbr, dff), F32)],
        compiler_params=_cparams(("arbitrary",)),
        name="ffn_sample",
    )(x_tm, buf_tm, p['norm_ffn_g'], p['w_ffn_gate'], p['w_ffn_up'], p['ffn_conv_w'], p['ffn_conv_b'],
      p['w_ffn_down'])


def _prep_layer(l, norm_mix_g, w_in, b_forget, fox_q_norm_g, fox_k_norm_g, ssd_conv_w, ssd_conv_b, ssd_dt_bias,
                ssd_a_log, ssd_d, ssd_norm_g, w_out, norm_mem_g, mem_in_norm_g, w_mem_q, w_mem_kv, mem_q_norm_g,
                mem_k_norm_g, w_mem_o, norm_ffn_g, w_ffn_gate, w_ffn_up, ffn_conv_w, ffn_conv_b, w_ffn_down):
    fw = FORGET_HEADS * FOX_DIM
    hw = SSD_HEADS * SSD_DIM
    w = w_in[l]
    o_f = 3 * fw
    o_z = o_f + FORGET_HEADS
    o_x = o_z + hw
    o_dt = w.shape[1] - SSD_HEADS
    w_small = jnp.concatenate([w[:, o_f:o_z], w[:, o_dt:]], axis=1)
    b_small = jnp.concatenate([b_forget[l], ssd_dt_bias[l]]).astype(F32)
    a_neg = -jnp.exp(ssd_a_log[l].astype(F32))
    a16 = jnp.concatenate([jnp.zeros((FORGET_HEADS,), F32), a_neg])
    head_of = jnp.arange(fw) // FOX_DIM
    onehot = (head_of[:, None] == jnp.arange(LANES)[None, :]).astype(F32)
    row = lambda v: v.reshape(1, -1).astype(F32)
    return {
        'norm_mix_g': row(norm_mix_g[l]),
        'w_q': w[:, :fw].astype(BF16),
        'w_kvt': w[:, fw:o_f].T.astype(BF16),
        'w_zx': w[:, o_z:o_dt].astype(BF16),
        'w_sm': jnp.pad(w_small, ((0, 0), (0, LANES - w_small.shape[1]))).astype(BF16),
        'w_smt': w_small.T.astype(BF16),
        'b_sm': jnp.pad(b_small, (0, LANES - b_small.shape[0])).reshape(1, LANES),
        'b_smt': b_small.reshape(-1, 1),
        'gq': row(jnp.tile(fox_q_norm_g[l], FORGET_HEADS)),
        'gk_col': jnp.tile(fox_k_norm_g[l], FORGET_HEADS).reshape(-1, 1).astype(F32),
        'head_reduce': (onehot / FOX_DIM).astype(BF16),
        'head_expand': onehot.T.astype(BF16),
        'ssd_rows': _ssd_param_rows(ssd_conv_w[l].astype(F32), row(ssd_conv_b[l]), row(jnp.repeat(ssd_d[l], SSD_DIM)),
                                    row(ssd_norm_g[l]), row(a16)),
        'a_col': a16.reshape(-1, 1),
        'w_out': w_out[l].astype(BF16),
        'norm_mem_g': row(norm_mem_g[l]),
        'mem_in_norm_g': row(mem_in_norm_g[l]),
        'w_mem_q': w_mem_q[l].astype(BF16),
        'w_mem_kv': w_mem_kv[l].astype(BF16),
        'mem_q_norm_g': row(mem_q_norm_g[l]),
        'mem_k_norm_g': row(mem_k_norm_g[l]),
        'w_mem_o': w_mem_o[l].astype(BF16),
        'norm_ffn_g': row(norm_ffn_g[l]),
        'w_ffn_gate': w_ffn_gate[l].astype(BF16),
        'w_ffn_up': w_ffn_up[l].astype(BF16),
        'ffn_conv_w': ffn_conv_w[l].astype(F32),
        'ffn_conv_b': row(ffn_conv_b[l]),
        'w_ffn_down': w_ffn_down[l].astype(BF16),
    }


def _row_tile(n, want):
    t = min(n, want)
    while n % t:
        t //= 2
    return t


def _layer(xp, mem, xs, kc, vc, lc, st_ssm, st_conv, st_ffn, mem_k, mem_v, page_table, p):
    fw = FORGET_HEADS * FOX_DIM
    hw = SSD_HEADS * SSD_DIM
    heads = lambda a, b_, t_: a.reshape(b_, FORGET_HEADS, FOX_DIM, t_).transpose(0, 3, 1, 2)

    bp, length, d = xp.shape
    n_p = bp * length
    n_mem = mem.shape[1]
    tm = _row_tile(length, 512)
    xpf = xp.reshape(n_p, d)
    q_p, kt_p, vt_p, ktb_p, vtb_p, z_p, xbc_p, sm_p, smt_p = _inproj(xpf, p, tm, bp, length)
    bs, t_new, _ = xs.shape
    n_s = bs * t_new
    xsf = xs.reshape(n_s, d)
    q_s, kt_s, vt_s, _, _, z_s, xbc_s, sm_s, smt_s = _inproj(xsf, p, _row_tile(n_s, 512), 1, n_s)

    ccol, crow = _cumsum(sm_p, smt_p, bp, length)
    tq = _row_tile(length, 512)
    nq = length // tq
    tk = _row_tile(length, 1024)
    crow4 = crow.reshape(2 * SUBLANES, bp, length // tk, tk).transpose(1, 2, 0, 3)
    k_s = kt_s[0].T.reshape(bs, t_new, fw)
    v_s = vt_s[0].T.reshape(bs, t_new, fw)
    own = (jnp.arange(fw) // FOX_DIM)[None, :] == jnp.arange(FORGET_HEADS)[:, None]
    qblk = jnp.where(own[None, None], q_s.reshape(bs, t_new, 1, fw), jnp.zeros((), q_s.dtype))
    qblk = qblk.reshape(bs, t_new * FORGET_HEADS, fw)
    lft = jnp.pad(smt_s[:FORGET_HEADS].reshape(FORGET_HEADS, bs, t_new).transpose(1, 0, 2),
                  ((0, 0), (0, 0), (0, LANES - t_new)))
    cdim = xbc_p.shape[1]
    xbc3_s = xbc_s.reshape(bs, t_new, cdim)
    smt4_s = smt_s.reshape(2 * SUBLANES, bs, 1, t_new).transpose(1, 2, 0, 3)
    tail0 = jnp.pad(st_conv, ((0, 0), (SUBLANES - (SSD_CONV - 1), 0), (0, 0)))
    sxzm = jnp.concatenate([xbc_s, z_s, sm_s], axis=1).reshape(bs, t_new, cdim + hw + LANES)
    nc = length // CHUNK
    smt4_p = smt_p.reshape(2 * SUBLANES, bp, nc, CHUNK).transpose(1, 2, 0, 3)
    xbc3_p = xbc_p.reshape(bp, length, cdim)
    mix_s, fox_p, ssm_new, ssd_p, ssm_fin = _fox(
        page_table, qblk, jnp.concatenate([k_s, v_s], axis=2), lft, kc, vc, lc, q_p, ktb_p, vtb_p, ccol, crow4,
        tq, tk, sxzm, smt4_s, st_ssm.reshape(bs, hw, SSD_STATE), tail0,
        xbc3_p, z_p.reshape(bp, length, hw), sm_p.reshape(bp, length, LANES), smt4_p, p)

    mk, mv = _mem_kv(mem.reshape(bp * n_mem, d), p, _row_tile(n_mem, 512))
    y2_p = _memattn_prompt(xpf, fox_p, ssd_p.reshape(n_p, hw), mk, mv, p, tm, length, n_mem)
    flat = lambda a: a.reshape(a.shape[0], a.shape[1] * a.shape[2], a.shape[3])
    y2_s = _memattn_sample(xsf, mix_s.reshape(n_s, fw + hw), flat(mem_k), flat(mem_v), p, _row_tile(n_s, 32), t_new)

    y3_p, ffn_tail = _ffn_prompt(y2_p, p, tm, bp, length)
    x_tm = y2_s.reshape(bs, t_new, d).transpose(1, 0, 2).reshape(n_s, d)
    dff = st_ffn.shape[-1]
    buf_tm = st_ffn.transpose(1, 0, 2).reshape((FFN_CONV - 1) * bs, dff)
    y3_tm, nbuf_tm = _ffn_sample(x_tm, buf_tm, p, t_new)

    prompt = (y3_p.reshape(bp, length, d),
              heads(kt_p, bp, length),
              heads(vt_p, bp, length),
              sm_p.reshape(bp, length, LANES)[:, :, :FORGET_HEADS],
              ssm_fin.reshape(bp, SSD_HEADS, SSD_DIM, SSD_STATE),
              xbc3_p[:, length - (SSD_CONV - 1):, :],
              ffn_tail[:, SUBLANES - (FFN_CONV - 1):, :],
              mk.reshape(bp, n_mem, MEM_HEADS, MEM_DIM),
              mv.reshape(bp, n_mem, MEM_HEADS, MEM_DIM))
    sample = (y3_tm.reshape(t_new, bs, d).transpose(1, 0, 2),
              k_s.reshape(bs, t_new, FORGET_HEADS, FOX_DIM),
              v_s.reshape(bs, t_new, FORGET_HEADS, FOX_DIM),
              sm_s.reshape(bs, t_new, LANES)[:, :, :FORGET_HEADS],
              ssm_new.reshape(bs, SSD_HEADS, SSD_DIM, SSD_STATE),
              jnp.concatenate([st_conv, xbc3_s], axis=1)[:, t_new:, :],
              nbuf_tm.reshape(FFN_CONV - 1, bs, dff).transpose(1, 0, 2))
    return prompt, sample


def kernel(x_prompt, mem_prompt, x_sample, cache_fox_k, cache_fox_v, cache_fox_logf, state_ssm, state_ssm_conv, state_ffn_conv, cache_mem_k, cache_mem_v, page_table, norm_mix_g, w_in, b_forget, fox_q_norm_g, fox_k_norm_g, ssd_conv_w, ssd_conv_b, ssd_dt_bias, ssd_a_log, ssd_d, ssd_norm_g, w_out, norm_mem_g, mem_in_norm_g, w_mem_q, w_mem_kv, mem_q_norm_g, mem_k_norm_g, w_mem_o, norm_ffn_g, w_ffn_gate, w_ffn_up, ffn_conv_w, ffn_conv_b, w_ffn_down):
    depth = w_in.shape[0]
    yp, ys = x_prompt, x_sample
    kc_all = jnp.transpose(cache_fox_k, (0, 1, 3, 4, 2))
    vc_all = jnp.transpose(cache_fox_v, (0, 1, 3, 4, 2))
    lc_all = jnp.transpose(cache_fox_logf, (0, 1, 3, 2))
    pouts, souts = [], []
    for l in range(depth):
        p = _prep_layer(l, norm_mix_g, w_in, b_forget, fox_q_norm_g, fox_k_norm_g, ssd_conv_w, ssd_conv_b,
                        ssd_dt_bias, ssd_a_log, ssd_d, ssd_norm_g, w_out, norm_mem_g, mem_in_norm_g, w_mem_q,
                        w_mem_kv, mem_q_norm_g, mem_k_norm_g, w_mem_o, norm_ffn_g, w_ffn_gate, w_ffn_up,
                        ffn_conv_w, ffn_conv_b, w_ffn_down)
        po, so = _layer(yp, mem_prompt, ys, kc_all[l], vc_all[l], lc_all[l], state_ssm[l], state_ssm_conv[l],
                        state_ffn_conv[l], cache_mem_k[l], cache_mem_v[l], page_table, p)
        yp, ys = po[0], so[0]
        pouts.append(po[1:])
        souts.append(so[1:])
    stack = lambda outs, i: jnp.stack([o[i] for o in outs])
    return (yp, ys) + tuple(stack(pouts, i) for i in range(8)) + tuple(stack(souts, i) for i in range(6))
```

```python
import functools

import numpy as np
import jax
import jax.numpy as jnp
from jax import lax
from jax.experimental import pallas as pl
from jax.experimental.pallas import tpu as pltpu

F32 = jnp.float32
BF16 = jnp.bfloat16
EPS = 1e-6
LOG2E = 1.4426950408889634
FORGET_HEADS = 8
FOX_DIM = 64
SSD_HEADS = 8
SSD_DIM = 64
SSD_STATE = 128
SSD_CONV = 4
CHUNK = 128
MEM_HEADS = 4
MEM_DIM = 128
FFN_CONV = 3
LANES = 128
SUBLANES = 8
VMEM_LIMIT = 56 * 1024 * 1024
NT = (((1,), (1,)), ((), ()))
TN = (((0,), (0,)), ((), ()))


def _dot(a, b):
    return jnp.dot(a, b, preferred_element_type=F32)


def _dot_nt(a, b):
    return lax.dot_general(a, b, NT, preferred_element_type=F32)


def _dot_tn(a, b):
    return lax.dot_general(a, b, TN, preferred_element_type=F32)


def _split3(x):
    hi = x.astype(BF16)
    r1 = x - hi.astype(F32)
    mid = r1.astype(BF16)
    lo = (r1 - mid.astype(F32)).astype(BF16)
    return hi, mid, lo


def _dot3_left(x, m):
    hi, mid, lo = _split3(x)
    return _dot(hi, m) + _dot(mid, m) + _dot(lo, m)


def _dot3_right(m, x):
    hi, mid, lo = _split3(x)
    return _dot(m, hi) + _dot(m, mid) + _dot(m, lo)


def _dot2_left(x, m):
    hi = x.astype(BF16)
    lo = (x - hi.astype(F32)).astype(BF16)
    return _dot(hi, m) + _dot(lo, m)


def _dot2_right(m, x):
    hi = x.astype(BF16)
    lo = (x - hi.astype(F32)).astype(BF16)
    return _dot(m, hi) + _dot(m, lo)


def _rms(x, g):
    return x * lax.rsqrt(jnp.mean(x * x, axis=-1, keepdims=True) + EPS) * g


def _softplus(x):
    return jnp.maximum(x, 0.0) + jnp.log1p(jnp.exp(-jnp.abs(x)))


def _log_sigmoid(x):
    return -_softplus(-x)


def _silu(x):
    return x * (1.0 / (1.0 + jnp.exp(-x)))


def _iota(shape, dim):
    return lax.broadcasted_iota(jnp.int32, shape, dim)


def _div_pow2(x, d):
    assert d & (d - 1) == 0
    return lax.shift_right_logical(x, jnp.int32(d.bit_length() - 1))


def _cparams(sem):
    return pltpu.CompilerParams(dimension_semantics=sem, vmem_limit_bytes=VMEM_LIMIT)


def _const_spec(shape):
    n = len(shape)
    return pl.BlockSpec(shape, lambda *_: (0,) * n)


def _inproj_kernel(x_ref, g_ref, wq_ref, wkvt_ref, wzx_ref, wsm_ref, wsmt_ref, gq_ref, gkc_ref, red_ref, exp_ref,
                   bsm_ref, bsmt_ref,
                   q_ref, kt_ref, vt_ref, ktb_ref, vtb_ref, z_ref, xbc_ref, sm_ref, smt_ref):
    fw = FORGET_HEADS * FOX_DIM
    tm = x_ref.shape[0]
    xn = _rms(x_ref[...], g_ref[...]).astype(BF16)

    q = _dot(xn, wq_ref[...])
    rs = lax.rsqrt(_dot2_left(q * q, red_ref[...]) + EPS)
    q_ref[...] = (q * _dot2_left(rs, exp_ref[...]) * gq_ref[...] * (FOX_DIM ** -0.5 * LOG2E)).astype(BF16)

    kv = _dot_nt(wkvt_ref[...], xn)
    k3 = kv[0:fw, :].reshape(FORGET_HEADS, FOX_DIM, tm)
    k3 = k3 * lax.rsqrt(jnp.mean(k3 * k3, axis=1, keepdims=True) + EPS)
    kn = k3.reshape(fw, tm) * gkc_ref[...]
    kt_ref[...] = kn
    ktb_ref[...] = kn.astype(BF16)
    vt = kv[fw:, :]
    vt_ref[...] = vt
    vtb_ref[...] = vt.astype(BF16)

    z_ref[...] = _dot(xn, wzx_ref[:, 0:fw])
    xbc_ref[...] = _dot(xn, wzx_ref[:, fw:])
    sm = _dot(xn, wsm_ref[...]) + bsm_ref[...]
    sm_ref[...] = jnp.where(_iota(sm.shape, 1) < FORGET_HEADS, _log_sigmoid(sm), _softplus(sm))
    smt = _dot_nt(wsmt_ref[...], xn) + bsmt_ref[...]
    smt_ref[...] = jnp.where(_iota(smt.shape, 0) < FORGET_HEADS, _log_sigmoid(smt), _softplus(smt))


def _inproj(x, p, tm, batch, length):
    n, d = x.shape
    fw = FORGET_HEADS * FOX_DIM
    nzx = p['w_zx'].shape[1]
    per = length // tm
    row = lambda i: (i, 0)
    tr = lambda i: (i // per, 0, i % per)
    return pl.pallas_call(
        _inproj_kernel,
        grid=(n // tm,),
        in_specs=[
            pl.BlockSpec((tm, d), row),
            _const_spec((1, d)),
            _const_spec((d, fw)),
            _const_spec((2 * fw, d)),
            _const_spec((d, nzx)),
            _const_spec((d, LANES)),
            _const_spec((2 * SUBLANES, d)),
            _const_spec((1, fw)),
            _const_spec((fw, 1)),
            _const_spec((fw, LANES)),
            _const_spec((LANES, fw)),
            _const_spec((1, LANES)),
            _const_spec((2 * SUBLANES, 1)),
        ],
        out_specs=[
            pl.BlockSpec((tm, fw), row),
            pl.BlockSpec((None, fw, tm), tr),
            pl.BlockSpec((None, fw, tm), tr),
            pl.BlockSpec((None, fw, tm), tr),
            pl.BlockSpec((None, fw, tm), tr),
            pl.BlockSpec((tm, fw), row),
            pl.BlockSpec((tm, nzx - fw), row),
            pl.BlockSpec((tm, LANES), row),
            pl.BlockSpec((2 * SUBLANES, tm), lambda i: (0, i)),
        ],
        out_shape=[
            jax.ShapeDtypeStruct((n, fw), BF16),
            jax.ShapeDtypeStruct((batch, fw, length), F32),
            jax.ShapeDtypeStruct((batch, fw, length), F32),
            jax.ShapeDtypeStruct((batch, fw, length), BF16),
            jax.ShapeDtypeStruct((batch, fw, length), BF16),
            jax.ShapeDtypeStruct((n, fw), F32),
            jax.ShapeDtypeStruct((n, nzx - fw), F32),
            jax.ShapeDtypeStruct((n, LANES), F32),
            jax.ShapeDtypeStruct((2 * SUBLANES, n), F32),
        ],
        compiler_params=_cparams(("parallel",)),
        name="inproj",
    )(x, p['norm_mix_g'], p['w_q'], p['w_kvt'], p['w_zx'], p['w_sm'], p['w_smt'], p['gq'], p['gk_col'],
      p['head_reduce'], p['head_expand'], p['b_sm'], p['b_smt'])


def _cumsum_kernel(sm_ref, smt_ref, col_ref, row_ref):
    length = sm_ref.shape[0]
    r = _iota((LANES, LANES), 0)
    c = _iota((LANES, LANES), 1)
    lower = (c <= r).astype(BF16)
    upper = (r <= c).astype(BF16)
    carry_c = jnp.zeros((1, LANES), F32)
    carry_r = jnp.zeros((smt_ref.shape[0], 1), F32)
    for b in range(length // LANES):
        sl = slice(b * LANES, (b + 1) * LANES)
        cs = _dot3_right(lower, sm_ref[sl, :]) + carry_c
        col_ref[sl, :] = cs * LOG2E
        carry_c = cs[LANES - 1:LANES, :]
        rs = _dot3_left(smt_ref[:, sl], upper) + carry_r
        row_ref[:, sl] = rs * LOG2E
        carry_r = rs[:, LANES - 1:LANES]


def _cumsum(sm, smt, batch, length):
    n = sm.shape[0]
    return pl.pallas_call(
        _cumsum_kernel,
        grid=(batch,),
        in_specs=[pl.BlockSpec((length, LANES), lambda b: (b, 0)),
                  pl.BlockSpec((2 * SUBLANES, length), lambda b: (0, b))],
        out_specs=[pl.BlockSpec((length, LANES), lambda b: (b, 0)),
                   pl.BlockSpec((2 * SUBLANES, length), lambda b: (0, b))],
        out_shape=[jax.ShapeDtypeStruct((n, LANES), F32),
                   jax.ShapeDtypeStruct((2 * SUBLANES, n), F32)],
        compiler_params=_cparams(("parallel",)),
        name="logf_cumsum",
    )(sm, smt)


PAGES_PER_CHUNK = 32
KV_SLOTS = 2
STREAMS = 2


def _fox_kernel(pt_ref, pb_ref, ph_ref, pi_ref, pj_ref, pv_ref,
                qb_ref, knvn_ref, lft_ref, masks_ref, kc_ref, vc_ref, lc_ref,
                q_ref, kt_ref, vt_ref, cc_ref, cr_ref,
                sxzm_ref, ssmt_ref, sst_ref, stail_ref, rows_ref, acol_ref,
                pxbc_ref, pz_ref, psm_ref, psmt_ref,
                os_ref, op_ref, sfin_ref, py_ref, pfin_ref,
                kbuf, vbuf, lbuf, suf, m_s, l_s, acc_s, cn_s, knp, vnp, pm, pacc, pqa, xp, zp, smp, smtp,
                pstate, ptail, ksem, vsem, lsem,
                *, n_chunks, t_new, p_stride, p_chunks, p_per_seq):
    g = pl.program_id(0)
    n_steps = pl.num_programs(0)
    b = g // n_chunks
    c = g % n_chunks
    slot = g % KV_SLOTS
    ppc = PAGES_PER_CHUNK
    n_pages = n_chunks * ppc
    page = kbuf.shape[-1] // ppc
    heads = FORGET_HEADS
    fdim = heads * FOX_DIM

    def kv_copies(bb, cc, sl):
        cps = []
        for j in range(ppc):
            pg = pt_ref[bb, cc * ppc + j]
            cps.append(pltpu.make_async_copy(kc_ref.at[pg], kbuf.at[sl, :, :, pl.ds(j * page, page)], ksem.at[sl]))
            cps.append(pltpu.make_async_copy(vc_ref.at[pg], vbuf.at[sl, :, :, pl.ds(j * page, page)], vsem.at[sl]))
        return cps

    def lf_copies(bb, sl):
        return [pltpu.make_async_copy(lc_ref.at[pt_ref[bb, j]], lbuf.at[sl, pl.ds(j * heads, heads), :], lsem.at[sl])
                for j in range(n_pages)]

    @pl.when(g == 0)
    def _():
        for cp in lf_copies(0, 0):
            cp.start()
        for ahead in range(KV_SLOTS - 1):
            for cp in kv_copies(ahead // n_chunks, ahead % n_chunks, ahead):
                cp.start()
        pm[...] = jnp.zeros_like(pm)
        pacc[...] = jnp.zeros_like(pacc)
        pqa[...] = jnp.zeros_like(pqa)

    last_c = c == n_chunks - 1
    g_ahead = g + (KV_SLOTS - 1)

    @pl.when(g_ahead < n_steps)
    def _():
        for n, cp in enumerate(kv_copies(g_ahead // n_chunks, g_ahead % n_chunks, g_ahead % KV_SLOTS)):
            cp.start(priority=n % 2)

    bslot = b % 2

    @pl.when(jnp.logical_and(c == 0, g + n_chunks < n_steps))
    def _():
        for cp in lf_copies(b + 1, 1 - bslot):
            cp.start()

    @pl.when(c == 0)
    def _():
        pltpu.make_async_copy(lbuf.at[1 - bslot], lbuf.at[bslot], lsem.at[bslot]).wait()
        lf = lbuf[bslot]
        nr = lf.shape[0]
        later = masks_ref[nr:nr + page, 0:page]
        lpages = masks_ref[0:nr, 0:nr]
        within = _dot2_left(lf, later)
        tot = within[:, 0:1] + lf[:, 0:1]
        suf[...] = (within + _dot2_right(lpages, jnp.broadcast_to(tot, lf.shape))) * LOG2E
        rl = _iota((LANES, LANES), 0)
        cl = _iota((LANES, LANES), 1)
        cn_s[...] = _dot3_left(lft_ref[...], (rl <= cl).astype(BF16)) * LOG2E
        m_s[...] = jnp.full(m_s.shape, -jnp.inf, F32)
        l_s[...] = jnp.zeros(l_s.shape, F32)
        acc_s[...] = jnp.zeros(acc_s.shape, F32)

    cn = cn_s[...]
    cn_col = jnp.concatenate([cn[:, t:t + 1] for t in range(t_new)], axis=0)
    qb = qb_ref[...]

    other = (g + 1) % KV_SLOTS
    pltpu.make_async_copy(kbuf.at[other], kbuf.at[slot], ksem.at[slot]).wait()
    pltpu.make_async_copy(vbuf.at[other], vbuf.at[slot], vsem.at[slot]).wait()

    def online(st, s, pv):
        m = m_s[st]
        m_new = jnp.maximum(m, jnp.max(s, axis=1, keepdims=True))
        alpha = jnp.exp2(m - m_new)
        p = jnp.exp2(s - m_new)
        l_s[st] = alpha * l_s[st] + jnp.sum(p, axis=1, keepdims=True)
        acc_s[st] = alpha * acc_s[st] + pv(p.astype(BF16))
        m_s[st] = m_new

    def sample_chunk():
        hpc = ppc // STREAMS
        wdt = hpc * page
        scores = []
        for st in range(STREAMS):
            kt = kbuf[slot, :, :, pl.ds(st * wdt, wdt)].reshape(fdim, wdt).astype(BF16)
            sfx = jnp.concatenate([suf[pl.ds((c * ppc + st * hpc + j) * heads, heads), :] for j in range(hpc)],
                                  axis=1)
            scores.append(_dot(qb, kt) + jnp.concatenate([sfx] * t_new, axis=0) + cn_col)
        for st in range(STREAMS):
            vt = vbuf[slot, :, :, pl.ds(st * wdt, wdt)].reshape(fdim, wdt).astype(BF16)
            online(st, scores[st], lambda p: _dot_nt(p, vt))

    hp = ph_ref[g]
    pi = pi_ref[g]
    pj = pj_ref[g]
    kind = pv_ref[g]
    tq = q_ref.shape[0]
    first = _iota((1, LANES), 1) < FOX_DIM

    def prompt_pair(masked):
        kt = kt_ref[...]
        vt = vt_ref[...]
        cr = cr_ref[...]
        tk = kt.shape[1]
        half = FOX_DIM
        fresh = pj == 0

        def terms(x):
            hi, mid, lo = _split3(x)
            return hi.astype(F32), mid.astype(F32), lo.astype(F32)

        @pl.when(fresh)
        def _():
            cc = cc_ref[...]
            qf = q_ref[...].astype(F32)
            lane = _iota((tq, LANES), 1)
            for hh in range(2):
                cq = jnp.sum(jnp.where(lane == 2 * hp + hh, cc, 0.0), axis=1, keepdims=True)
                hi, mid, lo = terms(cq)
                e0 = half * (1 - hh)
                ext = jnp.where(lane == e0, hi, jnp.where(lane == e0 + 1, mid, jnp.where(lane == e0 + 2, lo,
                      jnp.where(jnp.logical_and(lane >= e0 + 3, lane < e0 + 6), 1.0, 0.0))))
                own = (lane < half) if hh == 0 else (lane >= half)
                pqa[hh] = jnp.where(own, qf, ext).astype(BF16)

        sub = _iota(cr.shape, 0)
        r16 = _iota((2 * SUBLANES, tk), 0)
        pad = jnp.zeros((half - 2 * SUBLANES, tk), BF16)
        ones_row = jnp.where(r16 == 0, 1.0, 0.0).astype(BF16)
        wide = lambda a: jnp.concatenate([a] * (tk // LANES), axis=1)
        outs = []
        for hh in range(2):
            ck = jnp.sum(jnp.where(sub == 2 * hp + hh, cr, 0.0), axis=0, keepdims=True)
            hi, mid, lo = terms(ck)
            kext = jnp.where(r16 < 3, 1.0, jnp.where(r16 == 3, -hi, jnp.where(r16 == 4, -mid,
                   jnp.where(r16 == 5, -lo, 0.0)))).astype(BF16)
            if hh == 0:
                kta = jnp.concatenate([kt[0:half, :], kext, pad], axis=0)
                vta = jnp.concatenate([vt[0:half, :], ones_row, pad], axis=0)
            else:
                kta = jnp.concatenate([kext, pad, kt[half:, :]], axis=0)
                vta = jnp.concatenate([ones_row, pad, vt[half:, :]], axis=0)
            s = _dot(pqa[hh], kta)
            if masked:
                s = jnp.where(_iota(s.shape, 1) + (pj * tk - pi * tq) <= _iota(s.shape, 0), s, -jnp.inf)
            m = jnp.where(fresh, -jnp.inf, pm[hh])
            acc = jnp.where(fresh, 0.0, pacc[hh])
            m_new = jnp.maximum(m, jnp.max(s, axis=1, keepdims=True))
            p = jnp.exp2(s - wide(m_new)).astype(BF16)
            acc = jnp.exp2(m - m_new) * acc + _dot_nt(p, vta)
            if masked:
                e0 = half * (1 - hh)
                outs.append(acc / acc[:, e0:e0 + 1])
            else:
                pm[hh] = m_new
                pacc[hh] = acc
        if masked:
            op_ref[...] = jnp.where(first, outs[0], outs[1])

    sample_chunk()

    @pl.when(kind == 2)
    def _():
        prompt_pair(True)

    @pl.when(kind == 1)
    def _():
        prompt_pair(False)

    p_idx = g // p_stride

    @pl.when(jnp.logical_and(g % p_stride == 0, p_idx < p_chunks))
    def _():
        _ssd_prompt_chunk(p_idx % p_per_seq, p_per_seq - 1, pxbc_ref, pz_ref, psm_ref, psmt_ref,
                          (rows_ref, acol_ref), py_ref, pfin_ref, pstate, ptail)

    @pl.when(c == min(1, n_chunks - 1))
    def _():
        cdim, hw = xp.shape[1], zp.shape[1]
        os_ref[:, fdim:] = _ssd_short_chunk(
            sxzm_ref[:, 0:cdim], sxzm_ref[:, cdim:cdim + hw], sxzm_ref[:, cdim + hw:], ssmt_ref[...],
            (xp, zp, smp, smtp), stail_ref[...], sst_ref, sfin_ref, (rows_ref, acol_ref))

    @pl.when(last_c)
    def _():
        knp[...] = jnp.zeros_like(knp)
        vnp[...] = jnp.zeros_like(vnp)
        knp[0:t_new, :] = knvn_ref[:, 0:fdim]
        vnp[0:t_new, :] = knvn_ref[:, fdim:]
        kn = knp[...].astype(BF16)
        vn = vnp[...].astype(BF16)
        s2 = _dot_nt(qb, kn) + cn_col - jnp.concatenate([cn] * t_new, axis=0)
        tok = _div_pow2(_iota(s2.shape, 0), heads)
        s2 = jnp.where(_iota(s2.shape, 1) <= tok, s2, -jnp.inf)
        online(0, s2, lambda p: _dot(p, vn))
        m_all = functools.reduce(jnp.maximum, [m_s[st] for st in range(STREAMS)])
        wts = [jnp.exp2(m_s[st] - m_all) for st in range(STREAMS)]
        o = (sum(wts[st] * acc_s[st] for st in range(STREAMS))
             / sum(wts[st] * l_s[st] for st in range(STREAMS)))
        keep = _div_pow2(_iota((heads, fdim), 1), FOX_DIM) == _iota((heads, fdim), 0)
        os_ref[:, 0:fdim] = jnp.concatenate(
            [jnp.sum(jnp.where(keep, o[t * heads:(t + 1) * heads, :], 0.0), axis=0, keepdims=True)
             for t in range(t_new)], axis=0)


def _pair_schedule(batch, n_hp, nq, tq, tk, n_steps, n_chunks):
    pairs = []
    for b in range(batch):
        for h in range(n_hp):
            for i in range(nq):
                j_last = ((i + 1) * tq - 1) // tk
                pairs += [(b, h, i, j, 2 if j == j_last else 1) for j in range(j_last + 1)]
    n_idle = n_steps - len(pairs)
    assert n_idle >= 0, "more prompt attention pairs than sample chunks"
    steps = np.arange(n_steps)
    chunk = steps % n_chunks
    seq = steps // n_chunks
    first_chunk = chunk == 0
    n_seq = n_steps // n_chunks
    if n_idle <= n_seq:
        pick = (seq * n_idle) // n_seq != ((seq + 1) * n_idle) // n_seq
        cost = np.where(np.logical_and(first_chunk, pick), 0, 1)
    else:
        cost = np.where(first_chunk, 0, np.where(chunk == n_chunks - 1, 1 + seq % 2, 3))
    idle = np.zeros(n_steps, bool)
    idle[np.argsort(cost, kind="stable")[:n_idle]] = True
    rows, k = [], 0
    for g in range(n_steps):
        if idle[g]:
            rows.append(pairs[min(k, len(pairs) - 1)][:4] + (0,))
        else:
            rows.append(pairs[k])
            k += 1
    return [jnp.asarray(np.array(col, np.int32)) for col in zip(*rows)]


def _fox(page_table, qblk, knvn, lft, kc, vc, lc, q, ktb, vtb, ccol, crow4, tq, tk,
         sxzm, smt4_s, st0, tail0, xbc_p, z_p, sm_p, smt4_p, p):
    cdim = tail0.shape[2]
    hw = st0.shape[1]
    bs, rows, fdim = qblk.shape
    t_new = rows // FORGET_HEADS
    n_pages = page_table.shape[1]
    page = kc.shape[-1]
    n_chunks = n_pages // PAGES_PER_CHUNK
    pc = PAGES_PER_CHUNK * page
    n_steps = bs * n_chunks
    bp, _, length = ktb.shape
    nq = length // tq
    n_hp = fdim // LANES
    sched = _pair_schedule(bp, n_hp, nq, tq, tk, n_steps, n_chunks)
    pos = np.arange(page)
    rows_ph = np.arange(n_pages * FORGET_HEADS)
    nr = rows_ph.size
    masks_np = np.zeros((nr + page, max(nr, page)), np.float32)
    masks_np[:nr, :nr] = ((rows_ph[None, :] % FORGET_HEADS == rows_ph[:, None] % FORGET_HEADS)
                          & (rows_ph[None, :] // FORGET_HEADS > rows_ph[:, None] // FORGET_HEADS))
    masks_np[nr:, :page] = pos[:, None] > pos[None, :]
    masks = jnp.asarray(masks_np, BF16)
    per_s = lambda g, *_: (g // n_chunks, 0, 0)
    p_per_seq = length // CHUNK
    p_chunks = bp * p_per_seq
    assert p_chunks <= n_steps, "more prompt SSD chunks than grid steps"
    p_stride = n_steps // p_chunks
    p_of = lambda g: jnp.minimum(g // p_stride, p_chunks - 1)
    p_blk = lambda g, *_: (p_of(g) // p_per_seq, p_of(g) % p_per_seq, 0)
    grid_spec = pltpu.PrefetchScalarGridSpec(
        num_scalar_prefetch=6,
        grid=(n_steps,),
        in_specs=[
            pl.BlockSpec((None, rows, fdim), per_s),
            pl.BlockSpec((None, t_new, 2 * fdim), per_s),
            pl.BlockSpec((None, FORGET_HEADS, LANES), per_s),
            pl.BlockSpec(masks.shape, lambda g, *_: (0, 0)),
            pl.BlockSpec(memory_space=pl.ANY),
            pl.BlockSpec(memory_space=pl.ANY),
            pl.BlockSpec(memory_space=pl.ANY),
            pl.BlockSpec((tq, LANES), lambda g, pt, pb, ph, pi, pj, pv: (pb[g] * nq + pi[g], ph[g])),
            pl.BlockSpec((None, LANES, tk), lambda g, pt, pb, ph, pi, pj, pv: (pb[g], ph[g], pj[g])),
            pl.BlockSpec((None, LANES, tk), lambda g, pt, pb, ph, pi, pj, pv: (pb[g], ph[g], pj[g])),
            pl.BlockSpec((tq, LANES), lambda g, pt, pb, ph, pi, pj, pv: (pb[g] * nq + pi[g], 0)),
            pl.BlockSpec((None, None, 2 * SUBLANES, tk), lambda g, pt, pb, ph, pi, pj, pv: (pb[g], pj[g], 0, 0)),
            pl.BlockSpec((None, t_new, cdim + hw + LANES), per_s),
            pl.BlockSpec((None, None, 2 * SUBLANES, t_new), lambda g, *_: (g // n_chunks, 0, 0, 0)),
            pl.BlockSpec((None, hw, SSD_STATE), per_s),
            pl.BlockSpec((None, SUBLANES, cdim), per_s),
        ] + _ssd_const_specs(cdim) + [
            pl.BlockSpec((None, CHUNK, cdim), p_blk),
            pl.BlockSpec((None, CHUNK, hw), p_blk),
            pl.BlockSpec((None, CHUNK, LANES), p_blk),
            pl.BlockSpec((None, None, 2 * SUBLANES, CHUNK), lambda g, *_: p_blk(g) + (0,)),
        ],
        out_specs=[
            pl.BlockSpec((None, t_new, fdim + hw), per_s),
            pl.BlockSpec((tq, LANES), lambda g, pt, pb, ph, pi, pj, pv: (pb[g] * nq + pi[g], ph[g])),
            pl.BlockSpec((None, hw, SSD_STATE), per_s),
            pl.BlockSpec((None, CHUNK, hw), p_blk),
            pl.BlockSpec((None, hw, SSD_STATE), lambda g, *_: (p_of(g) // p_per_seq, 0, 0)),
        ],
        scratch_shapes=[
            pltpu.VMEM((KV_SLOTS, FORGET_HEADS, FOX_DIM, pc), F32),
            pltpu.VMEM((KV_SLOTS, FORGET_HEADS, FOX_DIM, pc), F32),
            pltpu.VMEM((2, n_pages * FORGET_HEADS, page), F32),
            pltpu.VMEM((n_pages * FORGET_HEADS, page), F32),
            pltpu.VMEM((STREAMS, rows, 1), F32),
            pltpu.VMEM((STREAMS, rows, 1), F32),
            pltpu.VMEM((STREAMS, rows, fdim), F32),
            pltpu.VMEM((FORGET_HEADS, LANES), F32),
            pltpu.VMEM((LANES, fdim), F32),
            pltpu.VMEM((LANES, fdim), F32),
            pltpu.VMEM((2, tq, LANES), F32),
            pltpu.VMEM((2, tq, LANES), F32),
            pltpu.VMEM((2, tq, LANES), BF16),
        ] + _ssd_pad_scratch(cdim, hw) + [
            pltpu.VMEM((hw, SSD_STATE), F32),
            pltpu.VMEM((SUBLANES, cdim), F32),
            pltpu.SemaphoreType.DMA((KV_SLOTS,)),
            pltpu.SemaphoreType.DMA((KV_SLOTS,)),
            pltpu.SemaphoreType.DMA((2,)),
        ],
    )
    return pl.pallas_call(
        functools.partial(_fox_kernel, n_chunks=n_chunks, t_new=t_new, p_stride=p_stride, p_chunks=p_chunks,
                          p_per_seq=p_per_seq),
        grid_spec=grid_spec,
        out_shape=[jax.ShapeDtypeStruct((bs, t_new, fdim + hw), F32),
                   jax.ShapeDtypeStruct((bp * length, fdim), F32),
                   jax.ShapeDtypeStruct((bs, hw, SSD_STATE), F32),
                   jax.ShapeDtypeStruct((bp, length, hw), F32),
                   jax.ShapeDtypeStruct((bp, hw, SSD_STATE), F32)],
        compiler_params=_cparams(("arbitrary",)),
        name="fox",
    )(page_table, *sched, qblk, knvn, lft, masks, kc, vc, lc, q, ktb, vtb, ccol, crow4,
      sxzm, smt4_s, st0, tail0, *_ssd_consts(p), xbc_p, z_p, sm_p, smt4_p)


def _ssd_chunk(x, z, sm, smt, prev, state_in, state_out, rows_ref, acol_ref):
    q = CHUNK
    hw = SSD_HEADS * SSD_DIM
    gs = SSD_STATE
    cw_ref = rows_ref.at[0:SSD_CONV]
    cb_ref = rows_ref.at[SSD_CONV:SSD_CONV + 1]
    dsk_ref = rows_ref.at[SSD_CONV + 1:SSD_CONV + 2, 0:hw]
    ng_ref = rows_ref.at[SSD_CONV + 1:SSD_CONV + 2, hw:2 * hw]
    arow_ref = rows_ref.at[SSD_CONV + 2:SSD_CONV + 3, 0:LANES]

    row8 = _iota(prev.shape, 0)
    conv = cb_ref[...] + cw_ref[SSD_CONV - 1:SSD_CONV, :] * x
    for sh in range(1, SSD_CONV):
        rx = pltpu.roll(x, sh, 0)
        fix = jnp.where(row8 < sh, pltpu.roll(prev, sh, 0), rx[0:SUBLANES, :])
        xs_sh = jnp.concatenate([fix, rx[SUBLANES:, :]], axis=0)
        conv = conv + cw_ref[SSD_CONV - 1 - sh:SSD_CONV - sh, :] * xs_sh
    xc = _silu(conv)
    xs = xc[:, 0:hw]

    r = _iota((q, q), 0)
    cidx = _iota((q, q), 1)
    tril = cidx <= r
    lower = tril.astype(BF16)
    upper = (r <= cidx).astype(BF16)
    acum_c = _dot3_right(lower, sm * arow_ref[...])
    acum_r = _dot3_left(smt * acol_ref[...], upper)
    lane = _iota((1, LANES), 1)
    first = lane < SSD_DIM
    rowp = _iota((LANES, 1), 0) < SSD_DIM

    ys = []
    for pr in range(SSD_HEADS // 2):
        g = pr // (SSD_HEADS // 4)
        bg = xc[:, hw + g * gs: hw + (g + 1) * gs].astype(BF16)
        cg = xc[:, hw + 2 * gs + g * gs: hw + 2 * gs + (g + 1) * gs].astype(BF16)
        gmat = _dot_nt(cg, bg)
        xpair = xs[:, pr * LANES:(pr + 1) * LANES]
        xpb = xpair.astype(BF16)
        yd, te, ea, cd = [], [], [], []
        for hh in range(2):
            h = FORGET_HEADS + 2 * pr + hh
            ac = acum_c[:, h:h + 1]
            ar = acum_r[h:h + 1, :]
            dtr = smt[h:h + 1, :]
            dtc = sm[:, h:h + 1]
            last = acum_c[q - 1:q, h:h + 1]
            decay = jnp.exp(jnp.where(tril, ac - ar, -jnp.inf))
            sc = gmat * decay * dtr
            yd.append(_dot(sc.astype(BF16), xpb))
            te.append(jnp.exp(last - ac) * dtc)
            ea.append(jnp.exp(ac))
            cd.append(jnp.exp(last))
        ydiag = jnp.where(first, yd[0], yd[1])
        xsc = (xpair * jnp.where(first, te[0], te[1])).astype(BF16)
        cstate = _dot_tn(xsc, bg)
        prev_st = state_in[pr * LANES:(pr + 1) * LANES, :]
        yoff = _dot_nt(cg, prev_st.astype(BF16)) * jnp.where(first, ea[0], ea[1])
        state_out[pr * LANES:(pr + 1) * LANES, :] = prev_st * jnp.where(rowp, cd[0], cd[1]) + cstate
        ys.append(ydiag + yoff + dsk_ref[:, pr * LANES:(pr + 1) * LANES] * xpair)
    y = jnp.concatenate(ys, axis=1) * _silu(z)
    return _rms(y, ng_ref[...])


def _ssd_short_chunk(x, z, sm, smt, pads, prev, state_in, state_out, consts):
    t_real = x.shape[0]
    xp, zp, smp, smtp = pads
    xp[...] = jnp.zeros_like(xp)
    zp[...] = jnp.zeros_like(zp)
    smp[...] = jnp.zeros_like(smp)
    smtp[...] = jnp.zeros_like(smtp)
    xp[0:t_real, :] = x
    zp[0:t_real, :] = z
    smp[0:t_real, :] = sm
    smtp[:, 0:t_real] = smt
    y = _ssd_chunk(xp[...], zp[...], smp[...], smtp[...], prev, state_in, state_out, *consts)
    return y[0:t_real, :]


def _ssd_pad_scratch(cdim, hw):
    return [pltpu.VMEM((CHUNK, cdim), F32), pltpu.VMEM((CHUNK, hw), F32),
            pltpu.VMEM((CHUNK, LANES), F32), pltpu.VMEM((2 * SUBLANES, CHUNK), F32)]


def _ssd_param_rows(conv_w, conv_b, d_row, norm_g, a_row):
    cdim = conv_w.shape[1]
    padl = lambda v: jnp.pad(v, ((0, 0), (0, cdim - v.shape[1])))
    rows = [conv_w, conv_b, jnp.concatenate([d_row, norm_g], axis=1), padl(a_row)]
    used = sum(r.shape[0] for r in rows)
    return jnp.concatenate(rows + [jnp.zeros((SUBLANES - used, cdim), F32)], axis=0)


def _ssd_const_specs(cdim):
    return [_const_spec((SUBLANES, cdim)), _const_spec((2 * SUBLANES, 1))]


def _ssd_consts(p):
    return (p['ssd_rows'], p['a_col'])


def _ssd_prompt_chunk(c, last, xbc_ref, z_ref, sm_ref, smt_ref, consts, y_ref, fin_ref, state, tail):
    @pl.when(c == 0)
    def _():
        state[...] = jnp.zeros_like(state)
        tail[...] = jnp.zeros_like(tail)

    x = xbc_ref[...]
    y_ref[...] = _ssd_chunk(x, z_ref[...], sm_ref[...], smt_ref[...], tail[...], state, state, *consts)
    tail[...] = x[CHUNK - SUBLANES:, :]

    @pl.when(c == last)
    def _():
        fin_ref[...] = state[...]


def _mem_kv_kernel(m_ref, g_ref, w_ref, gk_ref, k_ref, v_ref):
    mw = MEM_HEADS * MEM_DIM
    xn = _rms(m_ref[...], g_ref[...]).astype(BF16)
    kv = _dot(xn, w_ref[...])
    for h in range(MEM_HEADS):
        sl = slice(h * MEM_DIM, (h + 1) * MEM_DIM)
        k_ref[:, sl] = _rms(kv[:, sl], gk_ref[...])
    v_ref[...] = kv[:, mw:]


def _mem_kv(mem, p, tm):
    n, d = mem.shape
    mw = MEM_HEADS * MEM_DIM
    row = lambda i: (i, 0)
    return pl.pallas_call(
        _mem_kv_kernel,
        grid=(n // tm,),
        in_specs=[pl.BlockSpec((tm, d), row), _const_spec((1, d)), _const_spec((d, 2 * mw)),
                  _const_spec((1, MEM_DIM))],
        out_specs=[pl.BlockSpec((tm, mw), row), pl.BlockSpec((tm, mw), row)],
        out_shape=[jax.ShapeDtypeStruct((n, mw), F32), jax.ShapeDtypeStruct((n, mw), F32)],
        compiler_params=_cparams(("parallel",)),
        name="mem_kv",
    )(mem, p['mem_in_norm_g'], p['w_mem_kv'], p['mem_k_norm_g'])


def _mix_out(x_ref, mix_refs, wout_ref, g_ref, wq_ref, gq_ref):
    y1 = x_ref[...]
    lo = 0
    for m_ref in mix_refs:
        hi = lo + m_ref.shape[-1]
        y1 = y1 + _dot(m_ref[...].astype(BF16), wout_ref[lo:hi, :])
        lo = hi
    xn = _rms(y1, g_ref[...]).astype(BF16)
    q = _dot(xn, wq_ref[...])
    qs = []
    for h in range(MEM_HEADS):
        qh = _rms(q[:, h * MEM_DIM:(h + 1) * MEM_DIM], gq_ref[...]) * MEM_DIM ** -0.5
        qs.append(qh)
    return y1, qs


def _memattn_prompt_kernel(x_ref, fo_ref, so_ref, wout_ref, g_ref, wq_ref, gq_ref, mk_ref, mv_ref, wo_ref, o_ref):
    y1, qs = _mix_out(x_ref, (fo_ref, so_ref), wout_ref, g_ref, wq_ref, gq_ref)
    outs = []
    for h in range(MEM_HEADS):
        sl = slice(h * MEM_DIM, (h + 1) * MEM_DIM)
        s = _dot_nt(qs[h].astype(BF16), mk_ref[:, sl].astype(BF16))
        p = jnp.exp(s - jnp.max(s, axis=1, keepdims=True))
        o = _dot(p.astype(BF16), mv_ref[:, sl].astype(BF16)) / jnp.sum(p, axis=1, keepdims=True)
        outs.append(o.astype(BF16))
    o_ref[...] = y1 + _dot(jnp.concatenate(outs, axis=1), wo_ref[...])


def _memattn_prompt(x, fo, so, mk, mv, p, tm, length, n_mem):
    n, d = x.shape
    fw = fo.shape[1]
    mw = MEM_HEADS * MEM_DIM
    per = length // tm
    row = lambda i: (i, 0)
    return pl.pallas_call(
        _memattn_prompt_kernel,
        grid=(n // tm,),
        in_specs=[pl.BlockSpec((tm, d), row), pl.BlockSpec((tm, fw), row), pl.BlockSpec((tm, fw), row),
                  _const_spec((2 * fw, d)), _const_spec((1, d)), _const_spec((d, mw)), _const_spec((1, MEM_DIM)),
                  pl.BlockSpec((n_mem, mw), lambda i: (i // per, 0)),
                  pl.BlockSpec((n_mem, mw), lambda i: (i // per, 0)),
                  _const_spec((mw, d))],
        out_specs=pl.BlockSpec((tm, d), row),
        out_shape=jax.ShapeDtypeStruct((n, d), F32),
        compiler_params=_cparams(("parallel",)),
        name="memattn_prompt",
    )(x, fo, so, p['w_out'], p['norm_mem_g'], p['w_mem_q'], p['mem_q_norm_g'], mk, mv, p['w_mem_o'])


def _memattn_sample_kernel(x_ref, fs_ref, wout_ref, g_ref, wq_ref, gq_ref, mk_ref, mv_ref, wo_ref, o_ref,
                           *, t_new):
    y1, qs = _mix_out(x_ref, (fs_ref,), wout_ref, g_ref, wq_ref, gq_ref)
    tm = y1.shape[0]
    per = SUBLANES // t_new
    nrow = MEM_HEADS * SUBLANES
    ncol = mk_ref.shape[1]
    row = _iota((nrow, 1), 0)
    mine = [_div_pow2(lax.bitwise_and(row, jnp.int32(SUBLANES - 1)), t_new) == u for u in range(per)]
    same_head = (lax.bitwise_and(_iota((nrow, ncol), 1), jnp.int32(MEM_HEADS - 1))
                 == _div_pow2(_iota((nrow, ncol), 0), SUBLANES))
    tiles = []
    for r in range(tm // SUBLANES):
        qst = jnp.concatenate([qs[h][r * SUBLANES:(r + 1) * SUBLANES, :] for h in range(MEM_HEADS)], axis=0)
        qst = qst.astype(BF16)
        s = None
        for u in range(per):
            su = _dot_nt(qst, mk_ref[r * per + u].astype(BF16))
            s = su if s is None else jnp.where(mine[u], su, s)
        s = jnp.where(same_head, s, -jnp.inf)
        p = jnp.exp(s - jnp.max(s, axis=1, keepdims=True))
        pb = p.astype(BF16)
        o = None
        for u in range(per):
            ou = _dot(pb, mv_ref[r * per + u].astype(BF16))
            o = ou if o is None else jnp.where(mine[u], ou, o)
        o = o / jnp.sum(p, axis=1, keepdims=True)
        tiles.append(jnp.concatenate([o[h * SUBLANES:(h + 1) * SUBLANES, :] for h in range(MEM_HEADS)], axis=1))
    o_ref[...] = y1 + _dot(jnp.concatenate(tiles, axis=0).astype(BF16), wo_ref[...])


def _memattn_sample(x, fs, mk, mv, p, tm, t_new):
    n, d = x.shape
    mixw = fs.shape[1]
    mw = MEM_HEADS * MEM_DIM
    n_flat = mk.shape[1]
    bb = tm // t_new
    row = lambda i: (i, 0)
    return pl.pallas_call(
        functools.partial(_memattn_sample_kernel, t_new=t_new),
        grid=(n // tm,),
        in_specs=[pl.BlockSpec((tm, d), row), pl.BlockSpec((tm, mixw), row),
                  _const_spec((mixw, d)), _const_spec((1, d)), _const_spec((d, mw)), _const_spec((1, MEM_DIM)),
                  pl.BlockSpec((bb, n_flat, MEM_DIM), lambda i: (i, 0, 0)),
                  pl.BlockSpec((bb, n_flat, MEM_DIM), lambda i: (i, 0, 0)),
                  _const_spec((mw, d))],
        out_specs=pl.BlockSpec((tm, d), row),
        out_shape=jax.ShapeDtypeStruct((n, d), F32),
        compiler_params=_cparams(("parallel",)),
        name="memattn_sample",
    )(x, fs, p['w_out'], p['norm_mem_g'], p['w_mem_q'], p['mem_q_norm_g'], mk, mv, p['w_mem_o'])


FF_CHUNK = 1024


def _ff_chunks(dff):
    return [(s, min(s + FF_CHUNK, dff)) for s in range(0, dff, FF_CHUNK)]


def _ffn_prompt_kernel(x_ref, g_ref, wg_ref, wu_ref, cw_ref, cb_ref, wd_ref, o_ref, tail_ref, halo, *, per):
    i = pl.program_id(0)
    tm = x_ref.shape[0]
    dff = wg_ref.shape[1]

    @pl.when(i % per == 0)
    def _():
        halo[...] = jnp.zeros_like(halo)

    x = x_ref[...]
    xn = _rms(x, g_ref[...]).astype(BF16)
    row8 = _iota((SUBLANES, 1), 0)
    acc = x
    for lo, hi in _ff_chunks(dff):
        gate = _dot(xn, wg_ref[:, lo:hi])
        up = _dot(xn, wu_ref[:, lo:hi])
        prev = halo[:, lo:hi]
        conv = cb_ref[:, lo:hi] + cw_ref[FFN_CONV - 1:FFN_CONV, lo:hi] * gate
        for sh in range(1, FFN_CONV):
            rg = pltpu.roll(gate, sh, 0)
            fix = jnp.where(row8 < sh, pltpu.roll(prev, sh, 0), rg[0:SUBLANES, :])
            conv = conv + cw_ref[FFN_CONV - 1 - sh:FFN_CONV - sh, lo:hi] * jnp.concatenate([fix, rg[SUBLANES:, :]], axis=0)
        halo[:, lo:hi] = gate[tm - SUBLANES:tm, :]
        hmid = (_silu(conv) * up).astype(BF16)
        acc = acc + _dot(hmid, wd_ref[lo:hi, :])
    o_ref[...] = acc
    tail_ref[...] = halo[...]


def _ffn_prompt(x, p, tm, batch, length):
    n, d = x.shape
    dff = p['w_ffn_gate'].shape[1]
    per = length // tm
    row = lambda i: (i, 0)
    return pl.pallas_call(
        functools.partial(_ffn_prompt_kernel, per=per),
        grid=(n // tm,),
        in_specs=[pl.BlockSpec((tm, d), row), _const_spec((1, d)),
                  pl.BlockSpec((d, dff), lambda i: (0, 0), pipeline_mode=pl.Buffered(1)),
                  pl.BlockSpec((d, dff), lambda i: (0, 0), pipeline_mode=pl.Buffered(1)),
                  _const_spec((FFN_CONV, dff)), _const_spec((1, dff)),
                  pl.BlockSpec((dff, d), lambda i: (0, 0), pipeline_mode=pl.Buffered(1))],
        out_specs=[pl.BlockSpec((tm, d), row),
                   pl.BlockSpec((None, SUBLANES, dff), lambda i: (i // per, 0, 0))],
        out_shape=[jax.ShapeDtypeStruct((n, d), F32),
                   jax.ShapeDtypeStruct((batch, SUBLANES, dff), F32)],
        scratch_shapes=[pltpu.VMEM((SUBLANES, dff), F32)],
        compiler_params=_cparams(("arbitrary",)),
        name="ffn_prompt",
    )(x, p['norm_ffn_g'], p['w_ffn_gate'], p['w_ffn_up'], p['ffn_conv_w'], p['ffn_conv_b'], p['w_ffn_down'])


def _ffn_sample_kernel(x_ref, buf_ref, g_ref, wg_ref, wu_ref, cw_ref, cb_ref, wd_ref, o_ref, nbuf_ref, *, t_new):
    nb = x_ref.shape[0] // t_new
    dff = wg_ref.shape[1]
    x = x_ref[...]
    xn = _rms(x, g_ref[...]).astype(BF16)
    acc = x
    for lo, hi in _ff_chunks(dff):
        gate = _dot(xn, wg_ref[:, lo:hi])
        up = _dot(xn, wu_ref[:, lo:hi])
        ext = jnp.concatenate([buf_ref[:, lo:hi], gate], axis=0)
        conv = cb_ref[:, lo:hi]
        for j in range(FFN_CONV):
            conv = conv + cw_ref[j:j + 1, lo:hi] * ext[j * nb:(j + t_new) * nb, :]
        nbuf_ref[:, lo:hi] = ext[t_new * nb:, :]
        hmid = (_silu(conv) * up).astype(BF16)
        acc = acc + _dot(hmid, wd_ref[lo:hi, :])
    o_ref[...] = acc


def _ffn_sample(x_tm, buf_tm, p, t_new):
    n, d = x_tm.shape
    dff = p['w_ffn_gate'].shape[1]
    nbr = buf_tm.shape[0]
    return pl.pallas_call(
        functools.partial(_ffn_sample_kernel, t_new=t_new),
        grid=(1,),
        in_specs=[_const_spec((n, d)), _const_spec((nbr, dff)), _const_spec((1, d)),
                  pl.BlockSpec((d, dff), lambda i: (0, 0), pipeline_mode=pl.Buffered(1)),
                  pl.BlockSpec((d, dff), lambda i: (0, 0), pipeline_mode=pl.Buffered(1)),
                  _const_spec((FFN_CONV, dff)), _const_spec((1, dff)),
                  pl.BlockSpec((dff, d), lambda i: (0, 0), pipeline_mode=pl.Buffered(1))],
        out_specs=[_const_spec((n, d)), _const_spec((nbr, dff))],
        out_shape=[jax.ShapeDtypeStruct((n, d), F32), jax.ShapeDtypeStruct((nbr, dff), F32)],
        compiler_params=_cparams(("arbitrary",)),
        name="ffn_sample",
    )(x_tm, buf_tm, p['norm_ffn_g'], p['w_ffn_gate'], p['w_ffn_up'], p['ffn_conv_w'], p['ffn_conv_b'],
      p['w_ffn_down'])


def _prep_layer(l, norm_mix_g, w_in, b_forget, fox_q_norm_g, fox_k_norm_g, ssd_conv_w, ssd_conv_b, ssd_dt_bias,
                ssd_a_log, ssd_d, ssd_norm_g, w_out, norm_mem_g, mem_in_norm_g, w_mem_q, w_mem_kv, mem_q_norm_g,
                mem_k_norm_g, w_mem_o, norm_ffn_g, w_ffn_gate, w_ffn_up, ffn_conv_w, ffn_conv_b, w_ffn_down):
    fw = FORGET_HEADS * FOX_DIM
    hw = SSD_HEADS * SSD_DIM
    w = w_in[l]
    o_f = 3 * fw
    o_z = o_f + FORGET_HEADS
    o_x = o_z + hw
    o_dt = w.shape[1] - SSD_HEADS
    w_small = jnp.concatenate([w[:, o_f:o_z], w[:, o_dt:]], axis=1)
    b_small = jnp.concatenate([b_forget[l], ssd_dt_bias[l]]).astype(F32)
    a_neg = -jnp.exp(ssd_a_log[l].astype(F32))
    a16 = jnp.concatenate([jnp.zeros((FORGET_HEADS,), F32), a_neg])
    head_of = jnp.arange(fw) // FOX_DIM
    onehot = (head_of[:, None] == jnp.arange(LANES)[None, :]).astype(F32)
    row = lambda v: v.reshape(1, -1).astype(F32)
    return {
        'norm_mix_g': row(norm_mix_g[l]),
        'w_q': w[:, :fw].astype(BF16),
        'w_kvt': w[:, fw:o_f].T.astype(BF16),
        'w_zx': w[:, o_z:o_dt].astype(BF16),
        'w_sm': jnp.pad(w_small, ((0, 0), (0, LANES - w_small.shape[1]))).astype(BF16),
        'w_smt': w_small.T.astype(BF16),
        'b_sm': jnp.pad(b_small, (0, LANES - b_small.shape[0])).reshape(1, LANES),
        'b_smt': b_small.reshape(-1, 1),
        'gq': row(jnp.tile(fox_q_norm_g[l], FORGET_HEADS)),
        'gk_col': jnp.tile(fox_k_norm_g[l], FORGET_HEADS).reshape(-1, 1).astype(F32),
        'head_reduce': (onehot / FOX_DIM).astype(BF16),
        'head_expand': onehot.T.astype(BF16),
        'ssd_rows': _ssd_param_rows(ssd_conv_w[l].astype(F32), row(ssd_conv_b[l]), row(jnp.repeat(ssd_d[l], SSD_DIM)),
                                    row(ssd_norm_g[l]), row(a16)),
        'a_col': a16.reshape(-1, 1),
        'w_out': w_out[l].astype(BF16),
        'norm_mem_g': row(norm_mem_g[l]),
        'mem_in_norm_g': row(mem_in_norm_g[l]),
        'w_mem_q': w_mem_q[l].astype(BF16),
        'w_mem_kv': w_mem_kv[l].astype(BF16),
        'mem_q_norm_g': row(mem_q_norm_g[l]),
        'mem_k_norm_g': row(mem_k_norm_g[l]),
        'w_mem_o': w_mem_o[l].astype(BF16),
        'norm_ffn_g': row(norm_ffn_g[l]),
        'w_ffn_gate': w_ffn_gate[l].astype(BF16),
        'w_ffn_up': w_ffn_up[l].astype(BF16),
        'ffn_conv_w': ffn_conv_w[l].astype(F32),
        'ffn_conv_b': row(ffn_conv_b[l]),
        'w_ffn_down': w_ffn_down[l].astype(BF16),
    }


def _row_tile(n, want):
    t = min(n, want)
    while n % t:
        t //= 2
    return t


def _layer(xp, mem, xs, kc, vc, lc, st_ssm, st_conv, st_ffn, mem_k, mem_v, page_table, p):
    fw = FORGET_HEADS * FOX_DIM
    hw = SSD_HEADS * SSD_DIM
    heads = lambda a, b_, t_: a.reshape(b_, FORGET_HEADS, FOX_DIM, t_).transpose(0, 3, 1, 2)

    bp, length, d = xp.shape
    n_p = bp * length
    n_mem = mem.shape[1]
    tm = _row_tile(length, 512)
    xpf = xp.reshape(n_p, d)
    q_p, kt_p, vt_p, ktb_p, vtb_p, z_p, xbc_p, sm_p, smt_p = _inproj(xpf, p, tm, bp, length)
    bs, t_new, _ = xs.shape
    n_s = bs * t_new
    xsf = xs.reshape(n_s, d)
    q_s, kt_s, vt_s, _, _, z_s, xbc_s, sm_s, smt_s = _inproj(xsf, p, _row_tile(n_s, 512), 1, n_s)

    ccol, crow = _cumsum(sm_p, smt_p, bp, length)
    tq = _row_tile(length, 512)
    nq = length // tq
    tk = _row_tile(length, 1024)
    crow4 = crow.reshape(2 * SUBLANES, bp, length // tk, tk).transpose(1, 2, 0, 3)
    k_s = kt_s[0].T.reshape(bs, t_new, fw)
    v_s = vt_s[0].T.reshape(bs, t_new, fw)
    own = (jnp.arange(fw) // FOX_DIM)[None, :] == jnp.arange(FORGET_HEADS)[:, None]
    qblk = jnp.where(own[None, None], q_s.reshape(bs, t_new, 1, fw), jnp.zeros((), q_s.dtype))
    qblk = qblk.reshape(bs, t_new * FORGET_HEADS, fw)
    lft = jnp.pad(smt_s[:FORGET_HEADS].reshape(FORGET_HEADS, bs, t_new).transpose(1, 0, 2),
                  ((0, 0), (0, 0), (0, LANES - t_new)))
    cdim = xbc_p.shape[1]
    xbc3_s = xbc_s.reshape(bs, t_new, cdim)
    smt4_s = smt_s.reshape(2 * SUBLANES, bs, 1, t_new).transpose(1, 2, 0, 3)
    tail0 = jnp.pad(st_conv, ((0, 0), (SUBLANES - (SSD_CONV - 1), 0), (0, 0)))
    sxzm = jnp.concatenate([xbc_s, z_s, sm_s], axis=1).reshape(bs, t_new, cdim + hw + LANES)
    nc = length // CHUNK
    smt4_p = smt_p.reshape(2 * SUBLANES, bp, nc, CHUNK).transpose(1, 2, 0, 3)
    xbc3_p = xbc_p.reshape(bp, length, cdim)
    mix_s, fox_p, ssm_new, ssd_p, ssm_fin = _fox(
        page_table, qblk, jnp.concatenate([k_s, v_s], axis=2), lft, kc, vc, lc, q_p, ktb_p, vtb_p, ccol, crow4,
        tq, tk, sxzm, smt4_s, st_ssm.reshape(bs, hw, SSD_STATE), tail0,
        xbc3_p, z_p.reshape(bp, length, hw), sm_p.reshape(bp, length, LANES), smt4_p, p)

    mk, mv = _mem_kv(mem.reshape(bp * n_mem, d), p, _row_tile(n_mem, 512))
    y2_p = _memattn_prompt(xpf, fox_p, ssd_p.reshape(n_p, hw), mk, mv, p, tm, length, n_mem)
    flat = lambda a: a.reshape(a.shape[0], a.shape[1] * a.shape[2], a.shape[3])
    y2_s = _memattn_sample(xsf, mix_s.reshape(n_s, fw + hw), flat(mem_k), flat(mem_v), p, _row_tile(n_s, 32), t_new)

    y3_p, ffn_tail = _ffn_prompt(y2_p, p, tm, bp, length)
    x_tm = y2_s.reshape(bs, t_new, d).transpose(1, 0, 2).reshape(n_s, d)
    dff = st_ffn.shape[-1]
    buf_tm = st_ffn.transpose(1, 0, 2).reshape((FFN_CONV - 1) * bs, dff)
    y3_tm, nbuf_tm = _ffn_sample(x_tm, buf_tm, p, t_new)

    prompt = (y3_p.reshape(bp, length, d),
              heads(kt_p, bp, length),
              heads(vt_p, bp, length),
              sm_p.reshape(bp, length, LANES)[:, :, :FORGET_HEADS],
              ssm_fin.reshape(bp, SSD_HEADS, SSD_DIM, SSD_STATE),
              xbc3_p[:, length - (SSD_CONV - 1):, :],
              ffn_tail[:, SUBLANES - (FFN_CONV - 1):, :],
              mk.reshape(bp, n_mem, MEM_HEADS, MEM_DIM),
              mv.reshape(bp, n_mem, MEM_HEADS, MEM_DIM))
    sample = (y3_tm.reshape(t_new, bs, d).transpose(1, 0, 2),
              k_s.reshape(bs, t_new, FORGET_HEADS, FOX_DIM),
              v_s.reshape(bs, t_new, FORGET_HEADS, FOX_DIM),
              sm_s.reshape(bs, t_new, LANES)[:, :, :FORGET_HEADS],
              ssm_new.reshape(bs, SSD_HEADS, SSD_DIM, SSD_STATE),
              jnp.concatenate([st_conv, xbc3_s], axis=1)[:, t_new:, :],
              nbuf_tm.reshape(FFN_CONV - 1, bs, dff).transpose(1, 0, 2))
    return prompt, sample


def kernel(x_prompt, mem_prompt, x_sample, cache_fox_k, cache_fox_v, cache_fox_logf, state_ssm, state_ssm_conv, state_ffn_conv, cache_mem_k, cache_mem_v, page_table, norm_mix_g, w_in, b_forget, fox_q_norm_g, fox_k_norm_g, ssd_conv_w, ssd_conv_b, ssd_dt_bias, ssd_a_log, ssd_d, ssd_norm_g, w_out, norm_mem_g, mem_in_norm_g, w_mem_q, w_mem_kv, mem_q_norm_g, mem_k_norm_g, w_mem_o, norm_ffn_g, w_ffn_gate, w_ffn_up, ffn_conv_w, ffn_conv_b, w_ffn_down):
    depth = w_in.shape[0]
    yp, ys = x_prompt, x_sample
    kc_all = jnp.transpose(cache_fox_k, (0, 1, 3, 4, 2))
    vc_all = jnp.transpose(cache_fox_v, (0, 1, 3, 4, 2))
    lc_all = jnp.transpose(cache_fox_logf, (0, 1, 3, 2))
    pouts, souts = [], []
    for l in range(depth):
        p = _prep_layer(l, norm_mix_g, w_in, b_forget, fox_q_norm_g, fox_k_norm_g, ssd_conv_w, ssd_conv_b,
                        ssd_dt_bias, ssd_a_log, ssd_d, ssd_norm_g, w_out, norm_mem_g, mem_in_norm_g, w_mem_q,
                        w_mem_kv, mem_q_norm_g, mem_k_norm_g, w_mem_o, norm_ffn_g, w_ffn_gate, w_ffn_up,
                        ffn_conv_w, ffn_conv_b, w_ffn_down)
        po, so = _layer(yp, mem_prompt, ys, kc_all[l], vc_all[l], lc_all[l], state_ssm[l], state_ssm_conv[l],
                        state_ffn_conv[l], cache_mem_k[l], cache_mem_v[l], page_table, p)
        yp, ys = po[0], so[0]
        pouts.append(po[1:])
        souts.append(so[1:])
    stack = lambda outs, i: jnp.stack([o[i] for o in outs])
    return (yp, ys) + tuple(stack(pouts, i) for i in range(8)) + tuple(stack(souts, i) for i in range(6))
```
